```python
import jax, jax.numpy as jnp
from jax import lax
import numpy as np

D_MODEL = 2048
BATCH = 32
SEQ = 256
DEPTH = 1
DEC_BATCH = 2
DEC_SEQ = 1024
PAST_LEN = 512

GRID_W = 64
HEAD_DIM = 128
D_MIX = D_MODEL
D_ATTN = D_MIX // 2
D_CONV = D_MIX - D_ATTN
N_HEADS = D_ATTN // HEAD_DIM
N_KV_HEADS = 2
GQA_GROUP = N_HEADS // N_KV_HEADS
D_KV = N_KV_HEADS * HEAD_DIM
CONV_WIDTH = 3
N_EXPERTS = 16
EC_CAPACITY_FACTOR = 2
D_FF_EXPERT = D_MODEL // 2
Q_BLOCK = 128
ROPE_THETA = 10000.0
RMS_EPS = 1e-6
D_IN_PROJ = D_ATTN + 2 * D_KV + 3 * D_CONV
N_MOD = 6

kernel_name = "hybrid_attn_shortconv_ec_moe_dit_step"


def rms_norm(x, g):
    xf = x.astype(jnp.float32)
    y = xf * lax.rsqrt(jnp.mean(xf * xf, axis=-1, keepdims=True) + RMS_EPS)
    return (y * g.astype(jnp.float32)).astype(x.dtype)


def ada_modulation(cond, w_ada, b_ada):
    m = jax.nn.silu(cond) @ w_ada + b_ada
    return [p[:, None, :] for p in jnp.split(m, N_MOD, axis=-1)]


def axial_rope(x, rows):
    t = jnp.arange(rows * GRID_W)
    row = (t // GRID_W).astype(jnp.float32)
    col = (t % GRID_W).astype(jnp.float32)
    half = HEAD_DIM // 2
    inv_freq = ROPE_THETA ** (-jnp.arange(0, half, 2, dtype=jnp.float32) / half)

    def rotate(xa, pos):
        ang = pos[:, None] * inv_freq[None, :]
        cos = jnp.cos(ang)[None, :, None, :]
        sin = jnp.sin(ang)[None, :, None, :]
        x1, x2 = jnp.split(xa, 2, axis=-1)
        return jnp.concatenate([x1 * cos - x2 * sin, x1 * sin + x2 * cos], axis=-1)

    xf = x.astype(jnp.float32)
    out = jnp.concatenate([rotate(xf[..., :half], row), rotate(xf[..., half:], col)], axis=-1)
    return out.astype(x.dtype)


def block_attention(q, k, v):
    B, T = q.shape[0], q.shape[1]
    nb = T // Q_BLOCK
    qb = q.reshape(B, nb, Q_BLOCK, N_KV_HEADS, GQA_GROUP, HEAD_DIM).transpose(1, 0, 2, 3, 4, 5)
    scale = HEAD_DIM ** -0.5

    def one_block(qblk):
        s = jnp.einsum('bqkgd,bskd->bkgqs', qblk, k).astype(jnp.float32) * scale
        p = jax.nn.softmax(s, axis=-1).astype(v.dtype)
        return jnp.einsum('bkgqs,bskd->bqkgd', p, v)

    o = lax.map(one_block, qb)
    return o.transpose(1, 0, 2, 3, 4, 5).reshape(B, T, D_ATTN)


def short_conv(u, w):
    up = jnp.pad(u, ((0, 0), (1, 1), (0, 0)))
    return up[:, :-2] * w[0] + up[:, 1:-1] * w[1] + up[:, 2:] * w[2]


def ec_moe(h, w_router, w_gate, w_up, w_down):
    B, T, D = h.shape
    cap = EC_CAPACITY_FACTOR * T // N_EXPERTS
    logits = jnp.einsum('btd,de->bte', h, w_router).astype(jnp.float32)
    aff = jax.nn.softmax(logits, axis=-1)
    gate, idx = lax.top_k(jnp.swapaxes(aff, 1, 2), cap)
    xs = jax.vmap(lambda hb, ib: hb[ib])(h, idx)
    a = jnp.einsum('becd,edf->becf', xs, w_gate)
    u = jnp.einsum('becd,edf->becf', xs, w_up)
    y = jnp.einsum('becf,efd->becd', jax.nn.silu(a) * u, w_down)
    y = y * gate[..., None].astype(y.dtype)
    return jax.vmap(lambda ib, yb: jnp.zeros((T, D), yb.dtype).at[ib.reshape(-1)].add(yb.reshape(-1, D)))(idx, y)


def trunk_layer(x, cond, ctx_kv, rows, w_ada, b_ada, g_norm1, w_in, g_q, g_k, conv_w,
                g_attn_out, g_conv_out, w_out, g_norm2, w_router, w_gate, w_up, w_down):
    B, T, _ = x.shape
    sh1, sc1, g1, sh2, sc2, g2 = ada_modulation(cond, w_ada, b_ada)
    h = rms_norm(x, g_norm1) * (1 + sc1) + sh1
    proj = h @ w_in
    q, k, v, bg, cg, u = jnp.split(
        proj, [D_ATTN, D_ATTN + D_KV, D_ATTN + 2 * D_KV, D_ATTN + 2 * D_KV + D_CONV,
               D_ATTN + 2 * D_KV + 2 * D_CONV], axis=-1)
    q = rms_norm(q.reshape(B, T, N_HEADS, HEAD_DIM), g_q)
    k = rms_norm(k.reshape(B, T, N_KV_HEADS, HEAD_DIM), g_k)
    v = v.reshape(B, T, N_KV_HEADS, HEAD_DIM)
    if ctx_kv is None:
        attn = block_attention(q, k, v)
    else:
        q = axial_rope(q, rows)
        k = axial_rope(k, rows)
        k_ctx, v_ctx = ctx_kv
        attn = block_attention(q, jnp.concatenate([k_ctx, k], axis=1),
                               jnp.concatenate([v_ctx, v], axis=1))
    conv_out = bg * short_conv(cg * u, conv_w)
    mixed = jnp.concatenate([rms_norm(attn, g_attn_out), rms_norm(conv_out, g_conv_out)], axis=-1)
    x = x + g1 * (mixed @ w_out)
    h2 = rms_norm(x, g_norm2) * (1 + sc2) + sh2
    x = x + g2 * ec_moe(h2, w_router, w_gate, w_up, w_down)
    return x, k, v


def setup_inputs(seed: int = 0) -> dict:
    key = jax.random.key(seed)
    ks = jax.random.split(key, 24)
    f32 = jnp.float32
    nrm = lambda k, shape, s: jax.random.normal(k, shape, f32) * s
    d_in_s = D_MODEL ** -0.5
    return {
        "x_prompt": nrm(ks[0], (BATCH, SEQ, D_MODEL), 1.0),
        "x_sample": nrm(ks[1], (DEC_BATCH, DEC_SEQ, D_MODEL), 1.0),
        "cache_k": nrm(ks[2], (DEC_BATCH, DEPTH, PAST_LEN, N_KV_HEADS, HEAD_DIM), 1.0),
        "cache_v": nrm(ks[3], (DEC_BATCH, DEPTH, PAST_LEN, N_KV_HEADS, HEAD_DIM), 1.0),
        "c": nrm(ks[4], (DEC_BATCH, D_MODEL), 1.0),
        "c_ctx": nrm(ks[5], (D_MODEL,), 1.0),
        "w_ada": nrm(ks[6], (DEPTH, D_MODEL, N_MOD * D_MODEL), 0.5 * d_in_s),
        "b_ada": nrm(ks[7], (DEPTH, N_MOD * D_MODEL), 0.02),
        "g_norm1": 1.0 + nrm(ks[8], (DEPTH, D_MODEL), 0.02),
        "w_in": nrm(ks[9], (DEPTH, D_MODEL, D_IN_PROJ), d_in_s),
        "g_q": 1.0 + nrm(ks[10], (DEPTH, HEAD_DIM), 0.02),
        "g_k": 1.0 + nrm(ks[11], (DEPTH, HEAD_DIM), 0.02),
        "conv_w": nrm(ks[12], (DEPTH, CONV_WIDTH, D_CONV), CONV_WIDTH ** -0.5),
        "g_attn_out": 1.0 + nrm(ks[13], (DEPTH, D_ATTN), 0.02),
        "g_conv_out": 1.0 + nrm(ks[14], (DEPTH, D_CONV), 0.02),
        "w_out": nrm(ks[15], (DEPTH, D_MIX, D_MODEL), D_MIX ** -0.5),
        "g_norm2": 1.0 + nrm(ks[16], (DEPTH, D_MODEL), 0.02),
        "w_router": nrm(ks[17], (DEPTH, D_MODEL, N_EXPERTS), d_in_s),
        "w_gate": nrm(ks[18], (DEPTH, N_EXPERTS, D_MODEL, D_FF_EXPERT), d_in_s),
        "w_up": nrm(ks[19], (DEPTH, N_EXPERTS, D_MODEL, D_FF_EXPERT), d_in_s),
        "w_down": nrm(ks[20], (DEPTH, N_EXPERTS, D_FF_EXPERT, D_MODEL), D_FF_EXPERT ** -0.5),
    }


def reference(x_prompt, x_sample, cache_k, cache_v, c, c_ctx, w_ada, b_ada, g_norm1, w_in,
              g_q, g_k, conv_w, g_attn_out, g_conv_out, w_out, g_norm2, w_router, w_gate,
              w_up, w_down):
    rows = x_sample.shape[1] // GRID_W
    y_prompt = x_prompt
    y_sample = x_sample
    new_k = []
    new_v = []
    for l in range(DEPTH):
        params = (w_ada[l], b_ada[l], g_norm1[l], w_in[l], g_q[l], g_k[l], conv_w[l],
                  g_attn_out[l], g_conv_out[l], w_out[l], g_norm2[l], w_router[l],
                  w_gate[l], w_up[l], w_down[l])
        y_prompt, k_ctx, v_ctx = trunk_layer(y_prompt, c_ctx[None, :], None, 0, *params)
        new_k.append(k_ctx)
        new_v.append(v_ctx)
        y_sample, _, _ = trunk_layer(y_sample, c, (cache_k[:, l], cache_v[:, l]), rows, *params)
    state_k = jnp.stack(new_k, axis=1)
    state_v = jnp.stack(new_v, axis=1)
    return (y_prompt, y_sample, state_k, state_v)
```

```python
import functools

import numpy as np
import jax
import jax.numpy as jnp
from jax import lax
from jax.experimental import pallas as pl
from jax.experimental.pallas import tpu as pltpu

D_MODEL = 2048
GRID_W = 64
HEAD_DIM = 128
D_ATTN = 1024
D_CONV = 1024
N_HEADS = 8
N_KV_HEADS = 2
GQA_GROUP = N_HEADS // N_KV_HEADS
D_KV = N_KV_HEADS * HEAD_DIM
N_EXPERTS = 16
EC_CAPACITY_FACTOR = 2
D_FF = 1024
ROPE_THETA = 10000.0
RMS_EPS = 1e-6
N_MOD = 6
D_IN_PROJ = D_ATTN + 2 * D_KV + 3 * D_CONV

TOKEN_TILE = 256
ADA_COL_TILE = 1024
FF_TILE = 256
FFN_ROW_CHUNK = 256
COND_ROWS = 8
HALO_ROWS = 8
LATENT_ROUTE_EXPERTS = 4
VMEM_LIMIT = 52 * 1024 * 1024

_BF16 = jnp.bfloat16
_F32 = jnp.float32


def _params(n_axes):
    return pltpu.CompilerParams(dimension_semantics=("arbitrary",) * n_axes,
                                vmem_limit_bytes=VMEM_LIMIT)


def _resident(shape):
    return pl.BlockSpec(shape, lambda *_: (0,) * len(shape), pipeline_mode=pl.Buffered(1))


def _dot(a, b):
    return jnp.dot(a, b, preferred_element_type=_F32)


def _dot_nt(a, b):
    return lax.dot_general(a, b, (((1,), (1,)), ((), ())), preferred_element_type=_F32)


def _rms(x):
    return x * lax.rsqrt(jnp.mean(x * x, axis=-1, keepdims=True) + RMS_EPS)


def _split_bf16(x):
    hi = x.astype(_BF16)
    lo = (x - hi.astype(_F32)).astype(_BF16)
    return hi, lo


def _ada_kernel(c_ref, w_ref, b_ref, o_ref):
    c = c_ref[...]
    a = c * jax.nn.sigmoid(c)
    a_hi, a_lo = _split_bf16(a)
    w = w_ref[...]
    w_hi, w_lo = _split_bf16(w)
    o_ref[...] = _dot(a_hi, w_hi) + (_dot(a_lo, w_hi) + _dot(a_hi, w_lo)) + b_ref[...]


def _ada(cond, w_ada, b_ada):
    per_mod = D_MODEL // ADA_COL_TILE
    return pl.pallas_call(
        _ada_kernel,
        grid=(N_MOD, per_mod),
        in_specs=[
            pl.BlockSpec((COND_ROWS, D_MODEL), lambda j, h: (0, 0)),
            pl.BlockSpec((D_MODEL, ADA_COL_TILE), lambda j, h: (0, j * per_mod + h)),
            pl.BlockSpec((1, ADA_COL_TILE), lambda j, h: (0, j * per_mod + h)),
        ],
        out_specs=pl.BlockSpec((None, COND_ROWS, ADA_COL_TILE), lambda j, h: (j, 0, h)),
        out_shape=jax.ShapeDtypeStruct((N_MOD, COND_ROWS, D_MODEL), _F32),
        compiler_params=_params(2),
        name="ada",
    )(cond, w_ada, b_ada.reshape(1, N_MOD * D_MODEL))


def _rope_tables(seq):
    t = jnp.arange(seq)
    row = (t // GRID_W).astype(_F32)
    col = (t % GRID_W).astype(_F32)
    half = HEAD_DIM // 2
    inv_freq = ROPE_THETA ** (-jnp.arange(0, half, 2, dtype=_F32) / half)
    ang_r = row[:, None] * inv_freq[None, :]
    ang_c = col[:, None] * inv_freq[None, :]
    cos = jnp.concatenate([jnp.cos(ang_r)] * 2 + [jnp.cos(ang_c)] * 2, axis=-1)
    sin = jnp.concatenate([-jnp.sin(ang_r), jnp.sin(ang_r), -jnp.sin(ang_c), jnp.sin(ang_c)], axis=-1)
    return cos, sin


def _cond_row(latent, seq_tiles):
    if not latent:
        return 0
    return 1 + pl.program_id(0) // seq_tiles


def _inproj_kernel(latent, seq_tiles, x_ref, mod_ref, g1_ref, w_ref, gq_ref, gk_ref, *rest):
    if latent:
        cos_ref, sin_ref, q_ref, k_ref, v_ref, bg_ref, cu_ref = rest
    else:
        q_ref, k_ref, v_ref, bg_ref, cu_ref = rest
    r = _cond_row(latent, seq_tiles)
    sh = mod_ref[0, pl.ds(r, 1), :]
    sc = mod_ref[1, pl.ds(r, 1), :]
    h = _rms(x_ref[...]) * g1_ref[...] * (1.0 + sc) + sh
    hb = h.astype(_BF16)

    if latent:
        cos = cos_ref[...]
        sin = sin_ref[...]
        lane = lax.broadcasted_iota(jnp.int32, cos.shape, 1)
        first = (lane % (HEAD_DIM // 2)) < (HEAD_DIM // 4)

    def head_norm(xh, g):
        xh = _rms(xh) * g
        if latent:
            partner = jnp.where(first, pltpu.roll(xh, HEAD_DIM - HEAD_DIM // 4, axis=1),
                                pltpu.roll(xh, HEAD_DIM // 4, axis=1))
            xh = xh * cos + partner * sin
        return xh

    scale = HEAD_DIM ** -0.5
    q = _dot(hb, w_ref[:, 0:D_ATTN])
    for hd in range(N_HEADS):
        cols = slice(hd * HEAD_DIM, (hd + 1) * HEAD_DIM)
        q_ref[:, cols] = (head_norm(q[:, cols], gq_ref[...]) * scale).astype(q_ref.dtype)
    k = _dot(hb, w_ref[:, D_ATTN:D_ATTN + D_KV])
    for hd in range(N_KV_HEADS):
        cols = slice(hd * HEAD_DIM, (hd + 1) * HEAD_DIM)
        k_ref[:, cols] = head_norm(k[:, cols], gk_ref[...])
    o = D_ATTN + D_KV
    v_ref[...] = _dot(hb, w_ref[:, o:o + D_KV])
    o += D_KV
    bg_ref[...] = _dot(hb, w_ref[:, o:o + D_CONV])
    o += D_CONV
    cg = _dot(hb, w_ref[:, o:o + D_CONV])
    o += D_CONV
    cu_ref[...] = cg * _dot(hb, w_ref[:, o:o + D_CONV])


def _inproj(x, mod, g_norm1, w_in_b, g_q, g_k, latent, seq):
    n_tok = x.shape[0]
    seq_tiles = seq // TOKEN_TILE
    tile = lambda w: pl.BlockSpec((TOKEN_TILE, w), lambda i: (i, 0))
    in_specs = [
        tile(D_MODEL),
        _resident((N_MOD, COND_ROWS, D_MODEL)),
        _resident((1, D_MODEL)),
        _resident((D_MODEL, D_IN_PROJ)),
        _resident((1, HEAD_DIM)),
        _resident((1, HEAD_DIM)),
    ]
    args = [x, mod, g_norm1.reshape(1, D_MODEL), w_in_b, g_q.reshape(1, HEAD_DIM), g_k.reshape(1, HEAD_DIM)]
    if latent:
        cos, sin = _rope_tables(seq)
        rope_spec = pl.BlockSpec((TOKEN_TILE, HEAD_DIM), lambda i: (i % seq_tiles, 0))
        in_specs += [rope_spec, rope_spec]
        args += [cos, sin]
    return pl.pallas_call(
        functools.partial(_inproj_kernel, latent, seq_tiles),
        grid=(n_tok // TOKEN_TILE,),
        in_specs=in_specs,
        out_specs=[tile(D_ATTN), tile(D_KV), tile(D_KV), tile(D_CONV), tile(D_CONV)],
        out_shape=[
            jax.ShapeDtypeStruct((n_tok, D_ATTN), _BF16),
            jax.ShapeDtypeStruct((n_tok, D_KV), _F32),
            jax.ShapeDtypeStruct((n_tok, D_KV), _F32),
            jax.ShapeDtypeStruct((n_tok, D_CONV), _F32),
            jax.ShapeDtypeStruct((n_tok, D_CONV), _F32),
        ],
        compiler_params=_params(1),
        name="inproj_lat" if latent else "inproj_ctx",
    )(*args)


def _mix_kernel(latent, seq_tiles, x_ref, q_ref, *rest):
    n_kv = 2 if latent else 1
    kv_refs = rest[:2 * n_kv]
    rest = rest[2 * n_kv:]
    bg_ref, cu_ref = rest[:2]
    rest = rest[2:]
    if latent:
        cu_prev_ref, cu_next_ref = rest[:2]
        rest = rest[2:]
    (convw_ref, ga_ref, gc_ref, wout_ref, mod_ref, g2_ref, wr_ref,
     xo_ref, h2_ref, lg_ref, attn_scr) = rest

    r = _cond_row(latent, seq_tiles)

    for kvh in range(N_KV_HEADS):
        cols = slice(kvh * HEAD_DIM, (kvh + 1) * HEAD_DIM)
        ks = [kv_refs[2 * p][:, cols].astype(_BF16) for p in range(n_kv)]
        vs = [kv_refs[2 * p + 1][:, cols].astype(_BF16) for p in range(n_kv)]
        for g in range(GQA_GROUP):
            hc = slice((kvh * GQA_GROUP + g) * HEAD_DIM, (kvh * GQA_GROUP + g + 1) * HEAD_DIM)
            qh = q_ref[:, hc]
            s = [_dot_nt(qh, kp) for kp in ks]
            m = functools.reduce(jnp.maximum, [jnp.max(sp, axis=-1, keepdims=True) for sp in s])
            p = [jnp.exp(sp - m) for sp in s]
            l = functools.reduce(jnp.add, [jnp.sum(pp, axis=-1, keepdims=True) for pp in p])
            o = functools.reduce(jnp.add, [_dot(pp.astype(_BF16), vp) for pp, vp in zip(p, vs)])
            attn_scr[:, hc] = o / l
    attn_n = _rms(attn_scr[...]) * ga_ref[...]

    cu = cu_ref[...]
    row = lax.broadcasted_iota(jnp.int32, cu.shape, 0)
    if latent:
        pos = pl.program_id(0) % seq_tiles
        edge_prev = jnp.where(pos == 0, 0.0, 1.0) * cu_prev_ref[HALO_ROWS - 1:HALO_ROWS, :]
        edge_next = jnp.where(pos == seq_tiles - 1, 0.0, 1.0) * cu_next_ref[0:1, :]
    else:
        edge_prev = jnp.zeros((1, D_CONV), _F32)
        edge_next = edge_prev
    prev = jnp.where(row == 0, edge_prev, pltpu.roll(cu, 1, axis=0))
    nxt = jnp.where(row == TOKEN_TILE - 1, edge_next, pltpu.roll(cu, TOKEN_TILE - 1, axis=0))
    conv = bg_ref[...] * (prev * convw_ref[0:1, :] + cu * convw_ref[1:2, :] + nxt * convw_ref[2:3, :])
    conv_n = _rms(conv) * gc_ref[...]

    y = (_dot(attn_n.astype(_BF16), wout_ref[0:D_ATTN, :])
         + _dot(conv_n.astype(_BF16), wout_ref[D_ATTN:D_ATTN + D_CONV, :]))
    x_new = x_ref[...] + mod_ref[2, pl.ds(r, 1), :] * y
    xo_ref[...] = x_new

    h2 = _rms(x_new) * g2_ref[...] * (1.0 + mod_ref[4, pl.ds(r, 1), :]) + mod_ref[3, pl.ds(r, 1), :]
    h2_hi, h2_lo = _split_bf16(h2)
    h2_ref[...] = h2_hi
    wr_hi, wr_lo = _split_bf16(wr_ref[...])
    lg_ref[...] = _dot_nt(wr_hi, h2_hi) + (_dot_nt(wr_hi, h2_lo) + _dot_nt(wr_lo, h2_hi))


def _mix(x, q, kv_parts, bg, cu, mod, conv_w, g_attn_out, g_conv_out, w_out_b, g_norm2, w_router_t,
         latent, seq):
    n_tok = x.shape[0]
    seq_tiles = seq // TOKEN_TILE
    n_tiles = n_tok // TOKEN_TILE
    tile = lambda w: pl.BlockSpec((TOKEN_TILE, w), lambda i: (i, 0))
    in_specs = [tile(D_MODEL), tile(D_ATTN)]
    args = [x, q]
    for k, v in kv_parts:
        s = k.shape[1]
        spec = pl.BlockSpec((None, s, D_KV), lambda i: (i // seq_tiles, 0, 0))
        in_specs += [spec, spec]
        args += [k, v]
    in_specs += [tile(D_CONV), tile(D_CONV)]
    args += [bg, cu]
    if latent:
        per_tile = TOKEN_TILE // HALO_ROWS
        last = n_tok // HALO_ROWS - 1
        in_specs += [
            pl.BlockSpec((HALO_ROWS, D_CONV), lambda i: (jnp.maximum(i * per_tile - 1, 0), 0)),
            pl.BlockSpec((HALO_ROWS, D_CONV), lambda i: (jnp.minimum((i + 1) * per_tile, last), 0)),
        ]
        args += [cu, cu]
    in_specs += [
        _resident((3, D_CONV)),
        _resident((1, D_ATTN)),
        _resident((1, D_CONV)),
        _resident((D_ATTN + D_CONV, D_MODEL)),
        _resident((N_MOD, COND_ROWS, D_MODEL)),
        _resident((1, D_MODEL)),
        _resident((N_EXPERTS, D_MODEL)),
    ]
    args += [conv_w, g_attn_out.reshape(1, D_ATTN), g_conv_out.reshape(1, D_CONV), w_out_b, mod,
             g_norm2.reshape(1, D_MODEL), w_router_t]
    return pl.pallas_call(
        functools.partial(_mix_kernel, latent, seq_tiles),
        grid=(n_tiles,),
        in_specs=in_specs,
        out_specs=[tile(D_MODEL), tile(D_MODEL), pl.BlockSpec((N_EXPERTS, TOKEN_TILE), lambda i: (0, i))],
        out_shape=[
            jax.ShapeDtypeStruct((n_tok, D_MODEL), _F32),
            jax.ShapeDtypeStruct((n_tok, D_MODEL), _BF16),
            jax.ShapeDtypeStruct((N_EXPERTS, n_tok), _F32),
        ],
        scratch_shapes=[pltpu.VMEM((TOKEN_TILE, D_ATTN), _F32)],
        compiler_params=_params(1),
        name="mix_lat" if latent else "mix_ctx",
    )(*args)


def _route_kernel(seq, cap, e_blk, lg_ref, h2_ref, xs_ref, gw_ref, aff_scr, key_scr):
    j = pl.program_id(1)

    @pl.when(j == 0)
    def _():
        lg = lg_ref[...]
        ex = jnp.exp(lg - jnp.max(lg, axis=0, keepdims=True))
        aff = ex / jnp.sum(ex, axis=0, keepdims=True)
        aff_t = jnp.concatenate([aff, jnp.zeros((HEAD_DIM - N_EXPERTS, seq), _F32)], axis=0).T

        other = lax.broadcasted_iota(jnp.int32, (seq, seq), 0)
        token = lax.broadcasted_iota(jnp.int32, (seq, seq), 1)
        earlier = jnp.where(other < token, 1.0, 0.0)
        ranks = []
        for e in range(N_EXPERTS):
            mine = aff[e:e + 1, :]
            theirs = aff_t[:, e:e + 1]
            beats = jnp.where(theirs > mine, 1.0, jnp.where(theirs == mine, earlier, 0.0))
            ranks.append(jnp.sum(beats, axis=0, keepdims=True))
        rank = jnp.concatenate(ranks, axis=0)
        sel = rank < float(cap)
        slot = _dot(jnp.where(sel, 1.0, 0.0).astype(_BF16), earlier.astype(_BF16))
        aff_scr[...] = aff
        key_scr[...] = jnp.where(sel, slot, -1.0)

    h2 = h2_ref[...]
    c_idx = lax.broadcasted_iota(jnp.int32, (cap, seq), 0).astype(_F32)
    for i in range(e_blk):
        e = j * e_blk + i
        hit = c_idx == key_scr[pl.ds(e, 1), :]
        xs_ref[i] = _dot(jnp.where(hit, 1.0, 0.0).astype(_BF16), h2).astype(xs_ref.dtype)
        gw_ref[i * cap:(i + 1) * cap, :] = jnp.where(hit, aff_scr[pl.ds(e, 1), :], 0.0).astype(gw_ref.dtype)


def _route(logits_t, h2, seq, e_blk):
    n_tok = h2.shape[0]
    n_sets = n_tok // seq
    cap = EC_CAPACITY_FACTOR * seq // N_EXPERTS
    return pl.pallas_call(
        functools.partial(_route_kernel, seq, cap, e_blk),
        grid=(n_sets, N_EXPERTS // e_blk),
        in_specs=[
            pl.BlockSpec((N_EXPERTS, seq), lambda s, j: (0, s)),
            pl.BlockSpec((seq, D_MODEL), lambda s, j: (s, 0)),
        ],
        out_specs=[
            pl.BlockSpec((e_blk, cap, D_MODEL), lambda s, j: (j, s, 0)),
            pl.BlockSpec((None, e_blk * cap, seq), lambda s, j: (s, j, 0)),
        ],
        out_shape=[
            jax.ShapeDtypeStruct((N_EXPERTS, n_sets * cap, D_MODEL), _BF16),
            jax.ShapeDtypeStruct((n_sets, N_EXPERTS * cap, seq), _BF16),
        ],
        scratch_shapes=[pltpu.VMEM((N_EXPERTS, seq), _F32), pltpu.VMEM((N_EXPERTS, seq), _F32)],
        compiler_params=_params(2),
        name="route_%d" % seq,
    )(logits_t, h2)


def _ffn_kernel(xa_ref, xb_ref, wg_ref, wu_ref, wd_ref, ya_ref, yb_ref, acc_a, acc_b):
    f = pl.program_id(1)
    wg = wg_ref[...].astype(_BF16)
    wu = wu_ref[...].astype(_BF16)
    wd = wd_ref[...].astype(_BF16)
    last = pl.num_programs(1) - 1
    for x_ref, y_ref, acc in ((xa_ref, ya_ref, acc_a), (xb_ref, yb_ref, acc_b)):

        def chunk(c, carry, x_ref=x_ref, y_ref=y_ref, acc=acc):
            rows = pl.ds(pl.multiple_of(c * FFN_ROW_CHUNK, FFN_ROW_CHUNK), FFN_ROW_CHUNK)
            x = x_ref[rows, :]
            a = _dot(x, wg)
            u = _dot(x, wu)
            part = _dot((a * jax.nn.sigmoid(a) * u).astype(_BF16), wd)

            @pl.when(f == 0)
            def _():
                acc[rows, :] = part

            @pl.when(f > 0)
            def _():
                acc[rows, :] += part

            @pl.when(f == last)
            def _():
                y_ref[rows, :] = acc[rows, :].astype(y_ref.dtype)

            return carry

        lax.fori_loop(0, x_ref.shape[0] // FFN_ROW_CHUNK, chunk, 0)


def _ffn(xs_a, xs_b, w_gate, w_up, w_down):
    rows_a, rows_b = xs_a.shape[1], xs_b.shape[1]
    rows = lambda n: pl.BlockSpec((None, n, D_MODEL), lambda e, f: (e, 0, 0))
    return pl.pallas_call(
        _ffn_kernel,
        grid=(N_EXPERTS, D_FF // FF_TILE),
        in_specs=[
            rows(rows_a), rows(rows_b),
            pl.BlockSpec((None, D_MODEL, FF_TILE), lambda e, f: (e, 0, f)),
            pl.BlockSpec((None, D_MODEL, FF_TILE), lambda e, f: (e, 0, f)),
            pl.BlockSpec((None, FF_TILE, D_MODEL), lambda e, f: (e, f, 0)),
        ],
        out_specs=[rows(rows_a), rows(rows_b)],
        out_shape=[jax.ShapeDtypeStruct(xs_a.shape, _BF16), jax.ShapeDtypeStruct(xs_b.shape, _BF16)],
        scratch_shapes=[pltpu.VMEM((rows_a, D_MODEL), _F32), pltpu.VMEM((rows_b, D_MODEL), _F32)],
        compiler_params=_params(2),
        name="ffn",
    )(xs_a, xs_b, w_gate, w_up, w_down)


def _combine_kernel(latent, x_ref, y_ref, gw_ref, mod_ref, o_ref):
    r = 1 + pl.program_id(0) if latent else 0
    n_e, cap, d = y_ref.shape
    y = y_ref[...].reshape(n_e * cap, d)
    moe = lax.dot_general(gw_ref[...], y, (((0,), (0,)), ((), ())), preferred_element_type=_F32)
    o_ref[...] = x_ref[...] + mod_ref[5, pl.ds(r, 1), :] * moe


def _combine(x_new, y, gw, mod, latent, seq):
    n_tok = x_new.shape[0]
    n_sets = n_tok // seq
    cap = EC_CAPACITY_FACTOR * seq // N_EXPERTS
    seq_tiles = seq // TOKEN_TILE
    tile = pl.BlockSpec((TOKEN_TILE, D_MODEL), lambda s, t: (s * seq_tiles + t, 0))
    return pl.pallas_call(
        functools.partial(_combine_kernel, latent),
        grid=(n_sets, seq_tiles),
        in_specs=[
            tile,
            pl.BlockSpec((N_EXPERTS, cap, D_MODEL), lambda s, t: (0, s, 0)),
            pl.BlockSpec((None, N_EXPERTS * cap, TOKEN_TILE), lambda s, t: (s, 0, t)),
            _resident((N_MOD, COND_ROWS, D_MODEL)),
        ],
        out_specs=tile,
        out_shape=jax.ShapeDtypeStruct((n_tok, D_MODEL), _F32),
        compiler_params=_params(2),
        name="combine_lat" if latent else "combine_ctx",
    )(x_new, y, gw, mod)


def kernel(x_prompt, x_sample, cache_k, cache_v, c, c_ctx, w_ada, b_ada, g_norm1, w_in, g_q, g_k, conv_w,
           g_attn_out, g_conv_out, w_out, g_norm2, w_router, w_gate, w_up, w_down):
    batch, seq, _ = x_prompt.shape
    dec_batch, dec_seq, _ = x_sample.shape
    depth = w_ada.shape[0]
    assert 1 + dec_batch <= COND_ROWS

    cond = jnp.concatenate([c_ctx[None, :], c, jnp.zeros((COND_ROWS - 1 - dec_batch, D_MODEL), _F32)], axis=0)
    xp = x_prompt.reshape(batch * seq, D_MODEL)
    xl = x_sample.reshape(dec_batch * dec_seq, D_MODEL)
    past = cache_k.shape[2]

    new_k, new_v = [], []
    for l in range(depth):
        mod = _ada(cond, w_ada[l], b_ada[l])
        w_in_b = w_in[l].astype(_BF16)
        w_out_b = w_out[l].astype(_BF16)
        w_router_t = w_router[l].T

        q_p, k_p, v_p, bg_p, cu_p = _inproj(xp, mod, g_norm1[l], w_in_b, g_q[l], g_k[l], False, seq)
        q_l, k_l, v_l, bg_l, cu_l = _inproj(xl, mod, g_norm1[l], w_in_b, g_q[l], g_k[l], True, dec_seq)

        shared = (mod, conv_w[l], g_attn_out[l], g_conv_out[l], w_out_b, g_norm2[l], w_router_t)
        kv_p = [(k_p.reshape(batch, seq, D_KV), v_p.reshape(batch, seq, D_KV))]
        kv_l = [(cache_k[:, l].reshape(dec_batch, past, D_KV), cache_v[:, l].reshape(dec_batch, past, D_KV)),
                (k_l.reshape(dec_batch, dec_seq, D_KV), v_l.reshape(dec_batch, dec_seq, D_KV))]
        xp1, h2_p, lg_p = _mix(xp, q_p, kv_p, bg_p, cu_p, *shared, False, seq)
        xl1, h2_l, lg_l = _mix(xl, q_l, kv_l, bg_l, cu_l, *shared, True, dec_seq)

        xs_p, gw_p = _route(lg_p, h2_p, seq, N_EXPERTS)
        xs_l, gw_l = _route(lg_l, h2_l, dec_seq, LATENT_ROUTE_EXPERTS)
        y_p, y_l = _ffn(xs_p, xs_l, w_gate[l], w_up[l], w_down[l])
        xp = _combine(xp1, y_p, gw_p, mod, False, seq)
        xl = _combine(xl1, y_l, gw_l, mod, True, dec_seq)

        new_k.append(k_p.reshape(batch, seq, N_KV_HEADS, HEAD_DIM))
        new_v.append(v_p.reshape(batch, seq, N_KV_HEADS, HEAD_DIM))

    return (xp.reshape(batch, seq, D_MODEL), xl.reshape(dec_batch, dec_seq, D_MODEL),
            jnp.stack(new_k, axis=1), jnp.stack(new_v, axis=1))
```

```python
import functools

import numpy as np
import jax
import jax.numpy as jnp
from jax import lax
from jax.experimental import pallas as pl
from jax.experimental.pallas import tpu as pltpu

D_MODEL = 2048
GRID_W = 64
HEAD_DIM = 128
D_ATTN = 1024
D_CONV = 1024
N_HEADS = 8
N_KV_HEADS = 2
GQA_GROUP = N_HEADS // N_KV_HEADS
D_KV = N_KV_HEADS * HEAD_DIM
N_EXPERTS = 16
EC_CAPACITY_FACTOR = 2
D_FF = 1024
ROPE_THETA = 10000.0
RMS_EPS = 1e-6
N_MOD = 6
D_IN_PROJ = D_ATTN + 2 * D_KV + 3 * D_CONV

TOKEN_TILE = 256
ADA_COL_TILE = 1024
FF_TILE = 256
FFN_ROW_CHUNK = 256
COND_ROWS = 8
HALO_ROWS = 8
LATENT_ROUTE_EXPERTS = 4
VMEM_LIMIT = 52 * 1024 * 1024

_BF16 = jnp.bfloat16
_F32 = jnp.float32


def _params(n_axes):
    return pltpu.CompilerParams(dimension_semantics=("arbitrary",) * n_axes,
                                vmem_limit_bytes=VMEM_LIMIT)


def _resident(shape):
    return pl.BlockSpec(shape, lambda *_: (0,) * len(shape), pipeline_mode=pl.Buffered(1))


def _dot(a, b):
    return jnp.dot(a, b, preferred_element_type=_F32)


def _dot_nt(a, b):
    return lax.dot_general(a, b, (((1,), (1,)), ((), ())), preferred_element_type=_F32)


def _rms(x):
    return x * lax.rsqrt(jnp.mean(x * x, axis=-1, keepdims=True) + RMS_EPS)


def _split_bf16(x):
    hi = x.astype(_BF16)
    lo = (x - hi.astype(_F32)).astype(_BF16)
    return hi, lo


def _ada_kernel(c_ref, w_ref, b_ref, o_ref):
    c = c_ref[...]
    a = c * jax.nn.sigmoid(c)
    a_hi, a_lo = _split_bf16(a)
    w = w_ref[...]
    w_hi, w_lo = _split_bf16(w)
    o_ref[...] = _dot(a_hi, w_hi) + (_dot(a_lo, w_hi) + _dot(a_hi, w_lo)) + b_ref[...]


def _ada(cond, w_ada, b_ada):
    per_mod = D_MODEL // ADA_COL_TILE
    return pl.pallas_call(
        _ada_kernel,
        grid=(N_MOD, per_mod),
        in_specs=[
            pl.BlockSpec((COND_ROWS, D_MODEL), lambda j, h: (0, 0)),
            pl.BlockSpec((D_MODEL, ADA_COL_TILE), lambda j, h: (0, j * per_mod + h)),
            pl.BlockSpec((1, ADA_COL_TILE), lambda j, h: (0, j * per_mod + h)),
        ],
        out_specs=pl.BlockSpec((None, COND_ROWS, ADA_COL_TILE), lambda j, h: (j, 0, h)),
        out_shape=jax.ShapeDtypeStruct((N_MOD, COND_ROWS, D_MODEL), _F32),
        compiler_params=_params(2),
        name="ada",
    )(cond, w_ada, b_ada.reshape(1, N_MOD * D_MODEL))


def _rope_tables(seq):
    t = jnp.arange(seq)
    row = (t // GRID_W).astype(_F32)
    col = (t % GRID_W).astype(_F32)
    half = HEAD_DIM // 2
    inv_freq = ROPE_THETA ** (-jnp.arange(0, half, 2, dtype=_F32) / half)
    ang_r = row[:, None] * inv_freq[None, :]
    ang_c = col[:, None] * inv_freq[None, :]
    cos = jnp.concatenate([jnp.cos(ang_r)] * 2 + [jnp.cos(ang_c)] * 2, axis=-1)
    sin = jnp.concatenate([-jnp.sin(ang_r), jnp.sin(ang_r), -jnp.sin(ang_c), jnp.sin(ang_c)], axis=-1)
    return cos, sin


def _cond_row(latent, seq_tiles):
    if not latent:
        return 0
    return 1 + pl.program_id(0) // seq_tiles


def _inproj_kernel(latent, seq_tiles, x_ref, mod_ref, g1_ref, w_ref, gq_ref, gk_ref, *rest):
    if latent:
        cos_ref, sin_ref, q_ref, k_ref, v_ref, bg_ref, cu_ref = rest
    else:
        q_ref, k_ref, v_ref, bg_ref, cu_ref = rest
    r = _cond_row(latent, seq_tiles)
    sh = mod_ref[0, pl.ds(r, 1), :]
    sc = mod_ref[1, pl.ds(r, 1), :]
    h = _rms(x_ref[...]) * g1_ref[...] * (1.0 + sc) + sh
    hb = h.astype(_BF16)

    if latent:
        cos = cos_ref[...]
        sin = sin_ref[...]
        lane = lax.broadcasted_iota(jnp.int32, cos.shape, 1)
        first = (lane % (HEAD_DIM // 2)) < (HEAD_DIM // 4)

    def head_norm(xh, g):
        xh = _rms(xh) * g
        if latent:
            partner = jnp.where(first, pltpu.roll(xh, HEAD_DIM - HEAD_DIM // 4, axis=1),
                                pltpu.roll(xh, HEAD_DIM // 4, axis=1))
            xh = xh * cos + partner * sin
        return xh

    scale = HEAD_DIM ** -0.5
    q = _dot(hb, w_ref[:, 0:D_ATTN])
    for hd in range(N_HEADS):
        cols = slice(hd * HEAD_DIM, (hd + 1) * HEAD_DIM)
        q_ref[:, cols] = (head_norm(q[:, cols], gq_ref[...]) * scale).astype(q_ref.dtype)
    k = _dot(hb, w_ref[:, D_ATTN:D_ATTN + D_KV])
    for hd in range(N_KV_HEADS):
        cols = slice(hd * HEAD_DIM, (hd + 1) * HEAD_DIM)
        k_ref[:, cols] = head_norm(k[:, cols], gk_ref[...])
    o = D_ATTN + D_KV
    v_ref[...] = _dot(hb, w_ref[:, o:o + D_KV])
    o += D_KV
    bg_ref[...] = _dot(hb, w_ref[:, o:o + D_CONV])
    o += D_CONV
    cg = _dot(hb, w_ref[:, o:o + D_CONV])
    o += D_CONV
    cu_ref[...] = cg * _dot(hb, w_ref[:, o:o + D_CONV])


def _inproj(x, mod, g_norm1, w_in_b, g_q, g_k, latent, seq):
    n_tok = x.shape[0]
    seq_tiles = seq // TOKEN_TILE
    tile = lambda w: pl.BlockSpec((TOKEN_TILE, w), lambda i: (i, 0))
    in_specs = [
        tile(D_MODEL),
        _resident((N_MOD, COND_ROWS, D_MODEL)),
        _resident((1, D_MODEL)),
        _resident((D_MODEL, D_IN_PROJ)),
        _resident((1, HEAD_DIM)),
        _resident((1, HEAD_DIM)),
    ]
    args = [x, mod, g_norm1.reshape(1, D_MODEL), w_in_b, g_q.reshape(1, HEAD_DIM), g_k.reshape(1, HEAD_DIM)]
    if latent:
        cos, sin = _rope_tables(seq)
        rope_spec = pl.BlockSpec((TOKEN_TILE, HEAD_DIM), lambda i: (i % seq_tiles, 0))
        in_specs += [rope_spec, rope_spec]
        args += [cos, sin]
    return pl.pallas_call(
        functools.partial(_inproj_kernel, latent, seq_tiles),
        grid=(n_tok // TOKEN_TILE,),
        in_specs=in_specs,
        out_specs=[tile(D_ATTN), tile(D_KV), tile(D_KV), tile(D_CONV), tile(D_CONV)],
        out_shape=[
            jax.ShapeDtypeStruct((n_tok, D_ATTN), _BF16),
            jax.ShapeDtypeStruct((n_tok, D_KV), _F32),
            jax.ShapeDtypeStruct((n_tok, D_KV), _F32),
            jax.ShapeDtypeStruct((n_tok, D_CONV), _F32),
            jax.ShapeDtypeStruct((n_tok, D_CONV), _F32),
        ],
        compiler_params=_params(1),
        name="inproj_lat" if latent else "inproj_ctx",
    )(*args)


def _mix_kernel(latent, seq_tiles, x_ref, q_ref, *rest):
    n_kv = 2 if latent else 1
    kv_refs = rest[:2 * n_kv]
    rest = rest[2 * n_kv:]
    bg_ref, cu_ref = rest[:2]
    rest = rest[2:]
    if latent:
        cu_prev_ref, cu_next_ref = rest[:2]
        rest = rest[2:]
    (convw_ref, ga_ref, gc_ref, wout_ref, mod_ref, g2_ref, wr_ref,
     xo_ref, h2_ref, lg_ref, attn_scr) = rest

    r = _cond_row(latent, seq_tiles)

    for kvh in range(N_KV_HEADS):
        cols = slice(kvh * HEAD_DIM, (kvh + 1) * HEAD_DIM)
        ks = [kv_refs[2 * p][:, cols].astype(_BF16) for p in range(n_kv)]
        vs = [kv_refs[2 * p + 1][:, cols].astype(_BF16) for p in range(n_kv)]
        for g in range(GQA_GROUP):
            hc = slice((kvh * GQA_GROUP + g) * HEAD_DIM, (kvh * GQA_GROUP + g + 1) * HEAD_DIM)
            qh = q_ref[:, hc]
            s = [_dot_nt(qh, kp) for kp in ks]
            m = functools.reduce(jnp.maximum, [jnp.max(sp, axis=-1, keepdims=True) for sp in s])
            p = [jnp.exp(sp - m) for sp in s]
            l = functools.reduce(jnp.add, [jnp.sum(pp, axis=-1, keepdims=True) for pp in p])
            o = functools.reduce(jnp.add, [_dot(pp.astype(_BF16), vp) for pp, vp in zip(p, vs)])
            attn_scr[:, hc] = o / l
    attn_n = _rms(attn_scr[...]) * ga_ref[...]

    cu = cu_ref[...]
    row = lax.broadcasted_iota(jnp.int32, cu.shape, 0)
    if latent:
        pos = pl.program_id(0) % seq_tiles
        edge_prev = jnp.where(pos == 0, 0.0, 1.0) * cu_prev_ref[HALO_ROWS - 1:HALO_ROWS, :]
        edge_next = jnp.where(pos == seq_tiles - 1, 0.0, 1.0) * cu_next_ref[0:1, :]
    else:
        edge_prev = jnp.zeros((1, D_CONV), _F32)
        edge_next = edge_prev
    prev = jnp.where(row == 0, edge_prev, pltpu.roll(cu, 1, axis=0))
    nxt = jnp.where(row == TOKEN_TILE - 1, edge_next, pltpu.roll(cu, TOKEN_TILE - 1, axis=0))
    conv = bg_ref[...] * (prev * convw_ref[0:1, :] + cu * convw_ref[1:2, :] + nxt * convw_ref[2:3, :])
    conv_n = _rms(conv) * gc_ref[...]

    y = (_dot(attn_n.astype(_BF16), wout_ref[0:D_ATTN, :])
         + _dot(conv_n.astype(_BF16), wout_ref[D_ATTN:D_ATTN + D_CONV, :]))
    x_new = x_ref[...] + mod_ref[2, pl.ds(r, 1), :] * y
    xo_ref[...] = x_new

    h2 = _rms(x_new) * g2_ref[...] * (1.0 + mod_ref[4, pl.ds(r, 1), :]) + mod_ref[3, pl.ds(r, 1), :]
    h2_hi, h2_lo = _split_bf16(h2)
    h2_ref[...] = h2_hi
    wr_hi, wr_lo = _split_bf16(wr_ref[...])
    lg_ref[...] = _dot_nt(wr_hi, h2_hi) + (_dot_nt(wr_hi, h2_lo) + _dot_nt(wr_lo, h2_hi))


def _mix(x, q, kv_parts, bg, cu, mod, conv_w, g_attn_out, g_conv_out, w_out_b, g_norm2, w_router_t,
         latent, seq):
    n_tok = x.shape[0]
    seq_tiles = seq // TOKEN_TILE
    n_tiles = n_tok // TOKEN_TILE
    tile = lambda w: pl.BlockSpec((TOKEN_TILE, w), lambda i: (i, 0))
    in_specs = [tile(D_MODEL), tile(D_ATTN)]
    args = [x, q]
    for k, v in kv_parts:
        s = k.shape[1]
        spec = pl.BlockSpec((None, s, D_KV), lambda i: (i // seq_tiles, 0, 0))
        in_specs += [spec, spec]
        args += [k, v]
    in_specs += [tile(D_CONV), tile(D_CONV)]
    args += [bg, cu]
    if latent:
        per_tile = TOKEN_TILE // HALO_ROWS
        last = n_tok // HALO_ROWS - 1
        in_specs += [
            pl.BlockSpec((HALO_ROWS, D_CONV), lambda i: (jnp.maximum(i * per_tile - 1, 0), 0)),
            pl.BlockSpec((HALO_ROWS, D_CONV), lambda i: (jnp.minimum((i + 1) * per_tile, last), 0)),
        ]
        args += [cu, cu]
    in_specs += [
        _resident((3, D_CONV)),
        _resident((1, D_ATTN)),
        _resident((1, D_CONV)),
        _resident((D_ATTN + D_CONV, D_MODEL)),
        _resident((N_MOD, COND_ROWS, D_MODEL)),
        _resident((1, D_MODEL)),
        _resident((N_EXPERTS, D_MODEL)),
    ]
    args += [conv_w, g_attn_out.reshape(1, D_ATTN), g_conv_out.reshape(1, D_CONV), w_out_b, mod,
             g_norm2.reshape(1, D_MODEL), w_router_t]
    return pl.pallas_call(
        functools.partial(_mix_kernel, latent, seq_tiles),
        grid=(n_tiles,),
        in_specs=in_specs,
        out_specs=[tile(D_MODEL), tile(D_MODEL), pl.BlockSpec((N_EXPERTS, TOKEN_TILE), lambda i: (0, i))],
        out_shape=[
            jax.ShapeDtypeStruct((n_tok, D_MODEL), _F32),
            jax.ShapeDtypeStruct((n_tok, D_MODEL), _BF16),
            jax.ShapeDtypeStruct((N_EXPERTS, n_tok), _F32),
        ],
        scratch_shapes=[pltpu.VMEM((TOKEN_TILE, D_ATTN), _F32)],
        compiler_params=_params(1),
        name="mix_lat" if latent else "mix_ctx",
    )(*args)


def _route_kernel(seq, cap, e_blk, lg_ref, h2_ref, xs_ref, gw_ref, aff_scr, key_scr):
    j = pl.program_id(1)

    @pl.when(j == 0)
    def _():
        lg = lg_ref[...]
        ex = jnp.exp(lg - jnp.max(lg, axis=0, keepdims=True))
        aff = ex / jnp.sum(ex, axis=0, keepdims=True)
        aff_t = jnp.concatenate([aff, jnp.zeros((HEAD_DIM - N_EXPERTS, seq), _F32)], axis=0).T

        other = lax.broadcasted_iota(jnp.int32, (seq, seq), 0)
        token = lax.broadcasted_iota(jnp.int32, (seq, seq), 1)
        earlier = jnp.where(other < token, 1.0, 0.0)
        ranks = []
        for e in range(N_EXPERTS):
            mine = aff[e:e + 1, :]
            theirs = aff_t[:, e:e + 1]
            beats = jnp.where(theirs > mine, 1.0, jnp.where(theirs == mine, earlier, 0.0))
            ranks.append(jnp.sum(beats, axis=0, keepdims=True))
        rank = jnp.concatenate(ranks, axis=0)
        sel = rank < float(cap)
        slot = _dot(jnp.where(sel, 1.0, 0.0).astype(_BF16), earlier.astype(_BF16))
        aff_scr[...] = aff
        key_scr[...] = jnp.where(sel, slot, -1.0)

    c_idx = lax.broadcasted_iota(jnp.int32, (cap, seq), 0).astype(_F32)
    onehot, gated = [], []
    for i in range(e_blk):
        e = j * e_blk + i
        hit = c_idx == key_scr[pl.ds(e, 1), :]
        onehot.append(jnp.where(hit, 1.0, 0.0))
        gated.append(jnp.where(hit, aff_scr[pl.ds(e, 1), :], 0.0))
    xs = _dot(jnp.concatenate(onehot, axis=0).astype(_BF16), h2_ref[...]).astype(xs_ref.dtype)
    for i in range(e_blk):
        xs_ref[i] = xs[i * cap:(i + 1) * cap, :]
    gw_ref[...] = jnp.concatenate(gated, axis=0).astype(gw_ref.dtype)


def _route(logits_t, h2, seq, e_blk):
    n_tok = h2.shape[0]
    n_sets = n_tok // seq
    cap = EC_CAPACITY_FACTOR * seq // N_EXPERTS
    return pl.pallas_call(
        functools.partial(_route_kernel, seq, cap, e_blk),
        grid=(n_sets, N_EXPERTS // e_blk),
        in_specs=[
            pl.BlockSpec((N_EXPERTS, seq), lambda s, j: (0, s)),
            pl.BlockSpec((seq, D_MODEL), lambda s, j: (s, 0)),
        ],
        out_specs=[
            pl.BlockSpec((e_blk, cap, D_MODEL), lambda s, j: (j, s, 0)),
            pl.BlockSpec((None, e_blk * cap, seq), lambda s, j: (s, j, 0)),
        ],
        out_shape=[
            jax.ShapeDtypeStruct((N_EXPERTS, n_sets * cap, D_MODEL), _BF16),
            jax.ShapeDtypeStruct((n_sets, N_EXPERTS * cap, seq), _BF16),
        ],
        scratch_shapes=[pltpu.VMEM((N_EXPERTS, seq), _F32), pltpu.VMEM((N_EXPERTS, seq), _F32)],
        compiler_params=_params(2),
        name="route_%d" % seq,
    )(logits_t, h2)


def _ffn_kernel(xa_ref, xb_ref, wg_ref, wu_ref, wd_ref, ya_ref, yb_ref, acc_a, acc_b):
    @pl.when(pl.program_id(1) == 0)
    def _():
        acc_a[...] = jnp.zeros_like(acc_a)
        acc_b[...] = jnp.zeros_like(acc_b)

    wg = wg_ref[...].astype(_BF16)
    wu = wu_ref[...].astype(_BF16)
    wd = wd_ref[...].astype(_BF16)
    for x_ref, y_ref, acc in ((xa_ref, ya_ref, acc_a), (xb_ref, yb_ref, acc_b)):
        for c in range(x_ref.shape[0] // FFN_ROW_CHUNK):
            rows = slice(c * FFN_ROW_CHUNK, (c + 1) * FFN_ROW_CHUNK)
            x = x_ref[rows, :]
            a = _dot(x, wg)
            u = _dot(x, wu)
            total = acc[rows, :] + _dot((a * jax.nn.sigmoid(a) * u).astype(_BF16), wd)
            acc[rows, :] = total
            y_ref[rows, :] = total.astype(y_ref.dtype)


def _ffn(xs_a, xs_b, w_gate, w_up, w_down):
    rows_a, rows_b = xs_a.shape[1], xs_b.shape[1]
    rows = lambda n: pl.BlockSpec((None, n, D_MODEL), lambda e, f: (e, 0, 0))
    return pl.pallas_call(
        _ffn_kernel,
        grid=(N_EXPERTS, D_FF // FF_TILE),
        in_specs=[
            rows(rows_a), rows(rows_b),
            pl.BlockSpec((None, D_MODEL, FF_TILE), lambda e, f: (e, 0, f)),
            pl.BlockSpec((None, D_MODEL, FF_TILE), lambda e, f: (e, 0, f)),
            pl.BlockSpec((None, FF_TILE, D_MODEL), lambda e, f: (e, f, 0)),
        ],
        out_specs=[rows(rows_a), rows(rows_b)],
        out_shape=[jax.ShapeDtypeStruct(xs_a.shape, _BF16), jax.ShapeDtypeStruct(xs_b.shape, _BF16)],
        scratch_shapes=[pltpu.VMEM((rows_a, D_MODEL), _F32), pltpu.VMEM((rows_b, D_MODEL), _F32)],
        compiler_params=_params(2),
        name="ffn",
    )(xs_a, xs_b, w_gate, w_up, w_down)


def _combine_kernel(latent, x_ref, y_ref, gw_ref, mod_ref, o_ref):
    r = 1 + pl.program_id(0) if latent else 0
    n_e, cap, d = y_ref.shape
    y = y_ref[...].reshape(n_e * cap, d)
    moe = lax.dot_general(gw_ref[...], y, (((0,), (0,)), ((), ())), preferred_element_type=_F32)
    o_ref[...] = x_ref[...] + mod_ref[5, pl.ds(r, 1), :] * moe


def _combine(x_new, y, gw, mod, latent, seq):
    n_tok = x_new.shape[0]
    n_sets = n_tok // seq
    cap = EC_CAPACITY_FACTOR * seq // N_EXPERTS
    seq_tiles = seq // TOKEN_TILE
    tile = pl.BlockSpec((TOKEN_TILE, D_MODEL), lambda s, t: (s * seq_tiles + t, 0))
    return pl.pallas_call(
        functools.partial(_combine_kernel, latent),
        grid=(n_sets, seq_tiles),
        in_specs=[
            tile,
            pl.BlockSpec((N_EXPERTS, cap, D_MODEL), lambda s, t: (0, s, 0)),
            pl.BlockSpec((None, N_EXPERTS * cap, TOKEN_TILE), lambda s, t: (s, 0, t)),
            _resident((N_MOD, COND_ROWS, D_MODEL)),
        ],
        out_specs=tile,
        out_shape=jax.ShapeDtypeStruct((n_tok, D_MODEL), _F32),
        compiler_params=_params(2),
        name="combine_lat" if latent else "combine_ctx",
    )(x_new, y, gw, mod)


def kernel(x_prompt, x_sample, cache_k, cache_v, c, c_ctx, w_ada, b_ada, g_norm1, w_in, g_q, g_k, conv_w,
           g_attn_out, g_conv_out, w_out, g_norm2, w_router, w_gate, w_up, w_down):
    batch, seq, _ = x_prompt.shape
    dec_batch, dec_seq, _ = x_sample.shape
    depth = w_ada.shape[0]
    assert 1 + dec_batch <= COND_ROWS

    cond = jnp.concatenate([c_ctx[None, :], c, jnp.zeros((COND_ROWS - 1 - dec_batch, D_MODEL), _F32)], axis=0)
    xp = x_prompt.reshape(batch * seq, D_MODEL)
    xl = x_sample.reshape(dec_batch * dec_seq, D_MODEL)
    past = cache_k.shape[2]

    new_k, new_v = [], []
    for l in range(depth):
        mod = _ada(cond, w_ada[l], b_ada[l])
        w_in_b = w_in[l].astype(_BF16)
        w_out_b = w_out[l].astype(_BF16)
        w_router_t = w_router[l].T

        q_p, k_p, v_p, bg_p, cu_p = _inproj(xp, mod, g_norm1[l], w_in_b, g_q[l], g_k[l], False, seq)
        q_l, k_l, v_l, bg_l, cu_l = _inproj(xl, mod, g_norm1[l], w_in_b, g_q[l], g_k[l], True, dec_seq)

        shared = (mod, conv_w[l], g_attn_out[l], g_conv_out[l], w_out_b, g_norm2[l], w_router_t)
        kv_p = [(k_p.reshape(batch, seq, D_KV), v_p.reshape(batch, seq, D_KV))]
        kv_l = [(cache_k[:, l].reshape(dec_batch, past, D_KV), cache_v[:, l].reshape(dec_batch, past, D_KV)),
                (k_l.reshape(dec_batch, dec_seq, D_KV), v_l.reshape(dec_batch, dec_seq, D_KV))]
        xp1, h2_p, lg_p = _mix(xp, q_p, kv_p, bg_p, cu_p, *shared, False, seq)
        xl1, h2_l, lg_l = _mix(xl, q_l, kv_l, bg_l, cu_l, *shared, True, dec_seq)

        xs_p, gw_p = _route(lg_p, h2_p, seq, N_EXPERTS)
        xs_l, gw_l = _route(lg_l, h2_l, dec_seq, LATENT_ROUTE_EXPERTS)
        y_p, y_l = _ffn(xs_p, xs_l, w_gate[l], w_up[l], w_down[l])
        xp = _combine(xp1, y_p, gw_p, mod, False, seq)
        xl = _combine(xl1, y_l, gw_l, mod, True, dec_seq)

        new_k.append(k_p.reshape(batch, seq, N_KV_HEADS, HEAD_DIM))
        new_v.append(v_p.reshape(batch, seq, N_KV_HEADS, HEAD_DIM))

    return (xp.reshape(batch, seq, D_MODEL), xl.reshape(dec_batch, dec_seq, D_MODEL),
            jnp.stack(new_k, axis=1), jnp.stack(new_v, axis=1))
```

```python
import functools

import numpy as np
import jax
import jax.numpy as jnp
from jax import lax
from jax.experimental import pallas as pl
from jax.experimental.pallas import tpu as pltpu

D_MODEL = 2048
GRID_W = 64
HEAD_DIM = 128
D_ATTN = 1024
D_CONV = 1024
N_HEADS = 8
N_KV_HEADS = 2
GQA_GROUP = N_HEADS // N_KV_HEADS
D_KV = N_KV_HEADS * HEAD_DIM
N_EXPERTS = 16
EC_CAPACITY_FACTOR = 2
D_FF = 1024
ROPE_THETA = 10000.0
RMS_EPS = 1e-6
N_MOD = 6
D_IN_PROJ = D_ATTN + 2 * D_KV + 3 * D_CONV

TOKEN_TILE = 256
ADA_COL_TILE = 1024
FF_TILE = 256
FFN_ROW_CHUNK = 256
COND_ROWS = 8
HALO_ROWS = 8
LATENT_ROUTE_EXPERTS = 4
VMEM_LIMIT = 52 * 1024 * 1024

_BF16 = jnp.bfloat16
_F32 = jnp.float32


def _params(n_axes):
    return pltpu.CompilerParams(dimension_semantics=("arbitrary",) * n_axes,
                                vmem_limit_bytes=VMEM_LIMIT)


def _resident(shape):
    return pl.BlockSpec(shape, lambda *_: (0,) * len(shape), pipeline_mode=pl.Buffered(1))


def _dot(a, b):
    return jnp.dot(a, b, preferred_element_type=_F32)


def _dot_nt(a, b):
    return lax.dot_general(a, b, (((1,), (1,)), ((), ())), preferred_element_type=_F32)


def _rms(x):
    return x * lax.rsqrt(jnp.mean(x * x, axis=-1, keepdims=True) + RMS_EPS)


def _split_bf16(x):
    hi = x.astype(_BF16)
    lo = (x - hi.astype(_F32)).astype(_BF16)
    return hi, lo


def _ada_kernel(c_ref, w_ref, b_ref, o_ref):
    c = c_ref[...]
    a = c * jax.nn.sigmoid(c)
    a_hi, a_lo = _split_bf16(a)
    w = w_ref[...]
    w_hi, w_lo = _split_bf16(w)
    o_ref[...] = _dot(a_hi, w_hi) + (_dot(a_lo, w_hi) + _dot(a_hi, w_lo)) + b_ref[...]


def _ada(cond, w_ada, b_ada):
    per_mod = D_MODEL // ADA_COL_TILE
    return pl.pallas_call(
        _ada_kernel,
        grid=(N_MOD, per_mod),
        in_specs=[
            pl.BlockSpec((COND_ROWS, D_MODEL), lambda j, h: (0, 0)),
            pl.BlockSpec((D_MODEL, ADA_COL_TILE), lambda j, h: (0, j * per_mod + h)),
            pl.BlockSpec((1, ADA_COL_TILE), lambda j, h: (0, j * per_mod + h)),
        ],
        out_specs=pl.BlockSpec((None, COND_ROWS, ADA_COL_TILE), lambda j, h: (j, 0, h)),
        out_shape=jax.ShapeDtypeStruct((N_MOD, COND_ROWS, D_MODEL), _F32),
        compiler_params=_params(2),
        name="ada",
    )(cond, w_ada, b_ada.reshape(1, N_MOD * D_MODEL))


def _rope_tables(seq):
    t = jnp.arange(seq)
    row = (t // GRID_W).astype(_F32)
    col = (t % GRID_W).astype(_F32)
    half = HEAD_DIM // 2
    inv_freq = ROPE_THETA ** (-jnp.arange(0, half, 2, dtype=_F32) / half)
    ang_r = row[:, None] * inv_freq[None, :]
    ang_c = col[:, None] * inv_freq[None, :]
    cos = jnp.concatenate([jnp.cos(ang_r)] * 2 + [jnp.cos(ang_c)] * 2, axis=-1)
    sin = jnp.concatenate([-jnp.sin(ang_r), jnp.sin(ang_r), -jnp.sin(ang_c), jnp.sin(ang_c)], axis=-1)
    return cos, sin


def _cond_row(latent, seq_tiles):
    if not latent:
        return 0
    return 1 + pl.program_id(0) // seq_tiles


def _inproj_kernel(latent, seq_tiles, x_ref, mod_ref, g1_ref, w_ref, gq_ref, gk_ref, *rest):
    if latent:
        cos_ref, sin_ref, q_ref, k_ref, v_ref, bg_ref, cu_ref = rest
    else:
        q_ref, k_ref, v_ref, bg_ref, cu_ref = rest
    r = _cond_row(latent, seq_tiles)
    sh = mod_ref[0, pl.ds(r, 1), :]
    sc = mod_ref[1, pl.ds(r, 1), :]
    h = _rms(x_ref[...]) * g1_ref[...] * (1.0 + sc) + sh
    hb = h.astype(_BF16)

    if latent:
        cos = cos_ref[...]
        sin = sin_ref[...]
        lane = lax.broadcasted_iota(jnp.int32, cos.shape, 1)
        first = (lane % (HEAD_DIM // 2)) < (HEAD_DIM // 4)

    def head_norm(xh, g):
        xh = _rms(xh) * g
        if latent:
            partner = jnp.where(first, pltpu.roll(xh, HEAD_DIM - HEAD_DIM // 4, axis=1),
                                pltpu.roll(xh, HEAD_DIM // 4, axis=1))
            xh = xh * cos + partner * sin
        return xh

    scale = HEAD_DIM ** -0.5
    q = _dot(hb, w_ref[:, 0:D_ATTN])
    for hd in range(N_HEADS):
        cols = slice(hd * HEAD_DIM, (hd + 1) * HEAD_DIM)
        q_ref[:, cols] = (head_norm(q[:, cols], gq_ref[...]) * scale).astype(q_ref.dtype)
    k = _dot(hb, w_ref[:, D_ATTN:D_ATTN + D_KV])
    for hd in range(N_KV_HEADS):
        cols = slice(hd * HEAD_DIM, (hd + 1) * HEAD_DIM)
        k_ref[:, cols] = head_norm(k[:, cols], gk_ref[...])
    o = D_ATTN + D_KV
    v_ref[...] = _dot(hb, w_ref[:, o:o + D_KV])
    o += D_KV
    bg_ref[...] = _dot(hb, w_ref[:, o:o + D_CONV])
    o += D_CONV
    cg = _dot(hb, w_ref[:, o:o + D_CONV])
    o += D_CONV
    cu_ref[...] = cg * _dot(hb, w_ref[:, o:o + D_CONV])


def _inproj(x, mod, g_norm1, w_in_b, g_q, g_k, latent, seq):
    n_tok = x.shape[0]
    seq_tiles = seq // TOKEN_TILE
    tile = lambda w: pl.BlockSpec((TOKEN_TILE, w), lambda i: (i, 0))
    in_specs = [
        tile(D_MODEL),
        _resident((N_MOD, COND_ROWS, D_MODEL)),
        _resident((1, D_MODEL)),
        _resident((D_MODEL, D_IN_PROJ)),
        _resident((1, HEAD_DIM)),
        _resident((1, HEAD_DIM)),
    ]
    args = [x, mod, g_norm1.reshape(1, D_MODEL), w_in_b, g_q.reshape(1, HEAD_DIM), g_k.reshape(1, HEAD_DIM)]
    if latent:
        cos, sin = _rope_tables(seq)
        rope_spec = pl.BlockSpec((TOKEN_TILE, HEAD_DIM), lambda i: (i % seq_tiles, 0))
        in_specs += [rope_spec, rope_spec]
        args += [cos, sin]
    return pl.pallas_call(
        functools.partial(_inproj_kernel, latent, seq_tiles),
        grid=(n_tok // TOKEN_TILE,),
        in_specs=in_specs,
        out_specs=[tile(D_ATTN), tile(D_KV), tile(D_KV), tile(D_CONV), tile(D_CONV)],
        out_shape=[
            jax.ShapeDtypeStruct((n_tok, D_ATTN), _BF16),
            jax.ShapeDtypeStruct((n_tok, D_KV), _F32),
            jax.ShapeDtypeStruct((n_tok, D_KV), _F32),
            jax.ShapeDtypeStruct((n_tok, D_CONV), _F32),
            jax.ShapeDtypeStruct((n_tok, D_CONV), _F32),
        ],
        compiler_params=_params(1),
        name="inproj_lat" if latent else "inproj_ctx",
    )(*args)


def _mix_kernel(latent, seq_tiles, n_tiles, x_ref, q_ref, *rest):
    n_kv = 2 if latent else 1
    kv_refs = rest[:2 * n_kv]
    rest = rest[2 * n_kv:]
    bg_ref, cu_ref = rest[:2]
    rest = rest[2:]
    if latent:
        cu_prev_ref, cu_next_ref = rest[:2]
        rest = rest[2:]
    (convw_ref, ga_ref, gc_ref, wout_ref, mod_ref, g2_ref, wr_ref,
     xo_ref, h2_ref, lg_ref, mixed_scr) = rest

    s = pl.program_id(0)

    @pl.when(s == 0)
    def _():
        mixed_scr[...] = jnp.zeros_like(mixed_scr)

    r = 1 + jnp.maximum(s - 1, 0) // seq_tiles if latent else 0
    y = _dot(mixed_scr[...], wout_ref[...])
    x_new = x_ref[...] + mod_ref[2, pl.ds(r, 1), :] * y
    xo_ref[...] = x_new
    h2 = _rms(x_new) * g2_ref[...] * (1.0 + mod_ref[4, pl.ds(r, 1), :]) + mod_ref[3, pl.ds(r, 1), :]
    h2_hi, h2_lo = _split_bf16(h2)
    h2_ref[...] = h2_hi
    wr_hi, wr_lo = _split_bf16(wr_ref[...])
    lg_ref[...] = _dot_nt(wr_hi, h2_hi) + (_dot_nt(wr_hi, h2_lo) + _dot_nt(wr_lo, h2_hi))

    cu = cu_ref[...]
    row = lax.broadcasted_iota(jnp.int32, cu.shape, 0)
    if latent:
        pos = jnp.minimum(s, n_tiles - 1) % seq_tiles
        edge_prev = jnp.where(pos == 0, 0.0, 1.0) * cu_prev_ref[HALO_ROWS - 1:HALO_ROWS, :]
        edge_next = jnp.where(pos == seq_tiles - 1, 0.0, 1.0) * cu_next_ref[0:1, :]
    else:
        edge_prev = jnp.zeros((1, D_CONV), _F32)
        edge_next = edge_prev
    prev = jnp.where(row == 0, edge_prev, pltpu.roll(cu, 1, axis=0))
    nxt = jnp.where(row == TOKEN_TILE - 1, edge_next, pltpu.roll(cu, TOKEN_TILE - 1, axis=0))
    conv = bg_ref[...] * (prev * convw_ref[0:1, :] + cu * convw_ref[1:2, :] + nxt * convw_ref[2:3, :])
    conv_n = (_rms(conv) * gc_ref[...]).astype(_BF16)

    heads = []
    for kvh in range(N_KV_HEADS):
        cols = slice(kvh * HEAD_DIM, (kvh + 1) * HEAD_DIM)
        ks = [kv_refs[2 * p][:, cols].astype(_BF16) for p in range(n_kv)]
        vs = [kv_refs[2 * p + 1][:, cols].astype(_BF16) for p in range(n_kv)]
        for g in range(GQA_GROUP):
            hc = slice((kvh * GQA_GROUP + g) * HEAD_DIM, (kvh * GQA_GROUP + g + 1) * HEAD_DIM)
            qh = q_ref[:, hc]
            sc = [_dot_nt(qh, kp) for kp in ks]
            m = functools.reduce(jnp.maximum, [jnp.max(sp, axis=-1, keepdims=True) for sp in sc])
            p = [jnp.exp(sp - m) for sp in sc]
            l = functools.reduce(jnp.add, [jnp.sum(pp, axis=-1, keepdims=True) for pp in p])
            o = functools.reduce(jnp.add, [_dot(pp.astype(_BF16), vp) for pp, vp in zip(p, vs)])
            heads.append(o / l)
    ssq = functools.reduce(jnp.add, [jnp.sum(h * h, axis=-1, keepdims=True) for h in heads])
    inv = lax.rsqrt(ssq * (1.0 / D_ATTN) + RMS_EPS)
    for hd, h in enumerate(heads):
        hc = slice(hd * HEAD_DIM, (hd + 1) * HEAD_DIM)
        mixed_scr[:, hc] = (h * inv * ga_ref[:, hc]).astype(_BF16)
    mixed_scr[:, D_ATTN:D_ATTN + D_CONV] = conv_n


def _mix(x, q, kv_parts, bg, cu, mod, conv_w, g_attn_out, g_conv_out, w_out_b, g_norm2, w_router_t,
         latent, seq):
    n_tok = x.shape[0]
    seq_tiles = seq // TOKEN_TILE
    n_tiles = n_tok // TOKEN_TILE
    started = lambda i: jnp.minimum(i, n_tiles - 1)
    finished = lambda i: jnp.maximum(i - 1, 0)
    start_tile = lambda w: pl.BlockSpec((TOKEN_TILE, w), lambda i: (started(i), 0))
    finish_tile = lambda w: pl.BlockSpec((TOKEN_TILE, w), lambda i: (finished(i), 0))
    in_specs = [finish_tile(D_MODEL), start_tile(D_ATTN)]
    args = [x, q]
    for k, v in kv_parts:
        spec = pl.BlockSpec((None, k.shape[1], D_KV), lambda i: (started(i) // seq_tiles, 0, 0))
        in_specs += [spec, spec]
        args += [k, v]
    in_specs += [start_tile(D_CONV), start_tile(D_CONV)]
    args += [bg, cu]
    if latent:
        per_tile = TOKEN_TILE // HALO_ROWS
        last = n_tok // HALO_ROWS - 1
        in_specs += [
            pl.BlockSpec((HALO_ROWS, D_CONV), lambda i: (jnp.maximum(started(i) * per_tile - 1, 0), 0)),
            pl.BlockSpec((HALO_ROWS, D_CONV), lambda i: (jnp.minimum((started(i) + 1) * per_tile, last), 0)),
        ]
        args += [cu, cu]
    in_specs += [
        _resident((3, D_CONV)),
        _resident((1, D_ATTN)),
        _resident((1, D_CONV)),
        _resident((D_ATTN + D_CONV, D_MODEL)),
        _resident((N_MOD, COND_ROWS, D_MODEL)),
        _resident((1, D_MODEL)),
        _resident((N_EXPERTS, D_MODEL)),
    ]
    args += [conv_w, g_attn_out.reshape(1, D_ATTN), g_conv_out.reshape(1, D_CONV), w_out_b, mod,
             g_norm2.reshape(1, D_MODEL), w_router_t]
    return pl.pallas_call(
        functools.partial(_mix_kernel, latent, seq_tiles, n_tiles),
        grid=(n_tiles + 1,),
        in_specs=in_specs,
        out_specs=[finish_tile(D_MODEL), finish_tile(D_MODEL),
                   pl.BlockSpec((N_EXPERTS, TOKEN_TILE), lambda i: (0, finished(i)))],
        out_shape=[
            jax.ShapeDtypeStruct((n_tok, D_MODEL), _F32),
            jax.ShapeDtypeStruct((n_tok, D_MODEL), _BF16),
            jax.ShapeDtypeStruct((N_EXPERTS, n_tok), _F32),
        ],
        scratch_shapes=[pltpu.VMEM((TOKEN_TILE, D_ATTN + D_CONV), _BF16)],
        compiler_params=_params(1),
        name="mix_lat" if latent else "mix_ctx",
    )(*args)


def _route_kernel(seq, cap, e_blk, lg_ref, h2_ref, xs_ref, gw_ref, aff_scr, key_scr):
    j = pl.program_id(1)

    @pl.when(j == 0)
    def _():
        lg = lg_ref[...]
        ex = jnp.exp(lg - jnp.max(lg, axis=0, keepdims=True))
        aff = ex / jnp.sum(ex, axis=0, keepdims=True)
        aff_t = jnp.concatenate([aff, jnp.zeros((HEAD_DIM - N_EXPERTS, seq), _F32)], axis=0).T

        other = lax.broadcasted_iota(jnp.int32, (seq, seq), 0)
        token = lax.broadcasted_iota(jnp.int32, (seq, seq), 1)
        earlier = jnp.where(other < token, 1.0, 0.0)
        ranks = []
        for e in range(N_EXPERTS):
            mine = aff[e:e + 1, :]
            theirs = aff_t[:, e:e + 1]
            beats = jnp.where(theirs > mine, 1.0, jnp.where(theirs == mine, earlier, 0.0))
            ranks.append(jnp.sum(beats, axis=0, keepdims=True))
        rank = jnp.concatenate(ranks, axis=0)
        sel = rank < float(cap)
        slot = _dot(jnp.where(sel, 1.0, 0.0).astype(_BF16), earlier.astype(_BF16))
        aff_scr[...] = aff
        key_scr[...] = jnp.where(sel, slot, -1.0)

    c_idx = lax.broadcasted_iota(jnp.int32, (cap, seq), 0).astype(_F32)
    onehot, gated = [], []
    for i in range(e_blk):
        e = j * e_blk + i
        hit = c_idx == key_scr[pl.ds(e, 1), :]
        onehot.append(jnp.where(hit, 1.0, 0.0))
        gated.append(jnp.where(hit, aff_scr[pl.ds(e, 1), :], 0.0))
    xs = _dot(jnp.concatenate(onehot, axis=0).astype(_BF16), h2_ref[...]).astype(xs_ref.dtype)
    for i in range(e_blk):
        xs_ref[i] = xs[i * cap:(i + 1) * cap, :]
    gw_ref[...] = jnp.concatenate(gated, axis=0).astype(gw_ref.dtype)


def _route(logits_t, h2, seq, e_blk):
    n_tok = h2.shape[0]
    n_sets = n_tok // seq
    cap = EC_CAPACITY_FACTOR * seq // N_EXPERTS
    return pl.pallas_call(
        functools.partial(_route_kernel, seq, cap, e_blk),
        grid=(n_sets, N_EXPERTS // e_blk),
        in_specs=[
            pl.BlockSpec((N_EXPERTS, seq), lambda s, j: (0, s)),
            pl.BlockSpec((seq, D_MODEL), lambda s, j: (s, 0)),
        ],
        out_specs=[
            pl.BlockSpec((e_blk, cap, D_MODEL), lambda s, j: (j, s, 0)),
            pl.BlockSpec((None, e_blk * cap, seq), lambda s, j: (s, j, 0)),
        ],
        out_shape=[
            jax.ShapeDtypeStruct((N_EXPERTS, n_sets * cap, D_MODEL), _BF16),
            jax.ShapeDtypeStruct((n_sets, N_EXPERTS * cap, seq), _BF16),
        ],
        scratch_shapes=[pltpu.VMEM((N_EXPERTS, seq), _F32), pltpu.VMEM((N_EXPERTS, seq), _F32)],
        compiler_params=_params(2),
        name="route_%d" % seq,
    )(logits_t, h2)


def _ffn_kernel(xa_ref, xb_ref, wg_ref, wu_ref, wd_ref, ya_ref, yb_ref, acc_a, acc_b):
    @pl.when(pl.program_id(1) == 0)
    def _():
        acc_a[...] = jnp.zeros_like(acc_a)
        acc_b[...] = jnp.zeros_like(acc_b)

    wg = wg_ref[...].astype(_BF16)
    wu = wu_ref[...].astype(_BF16)
    wd = wd_ref[...].astype(_BF16)
    for x_ref, y_ref, acc in ((xa_ref, ya_ref, acc_a), (xb_ref, yb_ref, acc_b)):
        for c in range(x_ref.shape[0] // FFN_ROW_CHUNK):
            rows = slice(c * FFN_ROW_CHUNK, (c + 1) * FFN_ROW_CHUNK)
            x = x_ref[rows, :]
            a = _dot(x, wg)
            u = _dot(x, wu)
            total = acc[rows, :] + _dot((a * jax.nn.sigmoid(a) * u).astype(_BF16), wd)
            acc[rows, :] = total
            y_ref[rows, :] = total.astype(y_ref.dtype)


def _ffn(xs_a, xs_b, w_gate, w_up, w_down):
    rows_a, rows_b = xs_a.shape[1], xs_b.shape[1]
    rows = lambda n: pl.BlockSpec((None, n, D_MODEL), lambda e, f: (e, 0, 0))
    return pl.pallas_call(
        _ffn_kernel,
        grid=(N_EXPERTS, D_FF // FF_TILE),
        in_specs=[
            rows(rows_a), rows(rows_b),
            pl.BlockSpec((None, D_MODEL, FF_TILE), lambda e, f: (e, 0, f)),
            pl.BlockSpec((None, D_MODEL, FF_TILE), lambda e, f: (e, 0, f)),
            pl.BlockSpec((None, FF_TILE, D_MODEL), lambda e, f: (e, f, 0)),
        ],
        out_specs=[rows(rows_a), rows(rows_b)],
        out_shape=[jax.ShapeDtypeStruct(xs_a.shape, _BF16), jax.ShapeDtypeStruct(xs_b.shape, _BF16)],
        scratch_shapes=[pltpu.VMEM((rows_a, D_MODEL), _F32), pltpu.VMEM((rows_b, D_MODEL), _F32)],
        compiler_params=_params(2),
        name="ffn",
    )(xs_a, xs_b, w_gate, w_up, w_down)


def _combine_kernel(latent, x_ref, y_ref, gw_ref, mod_ref, o_ref):
    r = 1 + pl.program_id(0) if latent else 0
    n_e, cap, d = y_ref.shape
    y = y_ref[...].reshape(n_e * cap, d)
    moe = lax.dot_general(gw_ref[...], y, (((0,), (0,)), ((), ())), preferred_element_type=_F32)
    o_ref[...] = x_ref[...] + mod_ref[5, pl.ds(r, 1), :] * moe


def _combine(x_new, y, gw, mod, latent, seq):
    n_tok = x_new.shape[0]
    n_sets = n_tok // seq
    cap = EC_CAPACITY_FACTOR * seq // N_EXPERTS
    seq_tiles = seq // TOKEN_TILE
    tile = pl.BlockSpec((TOKEN_TILE, D_MODEL), lambda s, t: (s * seq_tiles + t, 0))
    return pl.pallas_call(
        functools.partial(_combine_kernel, latent),
        grid=(n_sets, seq_tiles),
        in_specs=[
            tile,
            pl.BlockSpec((N_EXPERTS, cap, D_MODEL), lambda s, t: (0, s, 0)),
            pl.BlockSpec((None, N_EXPERTS * cap, TOKEN_TILE), lambda s, t: (s, 0, t)),
            _resident((N_MOD, COND_ROWS, D_MODEL)),
        ],
        out_specs=tile,
        out_shape=jax.ShapeDtypeStruct((n_tok, D_MODEL), _F32),
        compiler_params=_params(2),
        name="combine_lat" if latent else "combine_ctx",
    )(x_new, y, gw, mod)


def kernel(x_prompt, x_sample, cache_k, cache_v, c, c_ctx, w_ada, b_ada, g_norm1, w_in, g_q, g_k, conv_w,
           g_attn_out, g_conv_out, w_out, g_norm2, w_router, w_gate, w_up, w_down):
    batch, seq, _ = x_prompt.shape
    dec_batch, dec_seq, _ = x_sample.shape
    depth = w_ada.shape[0]
    assert 1 + dec_batch <= COND_ROWS

    cond = jnp.concatenate([c_ctx[None, :], c, jnp.zeros((COND_ROWS - 1 - dec_batch, D_MODEL), _F32)], axis=0)
    xp = x_prompt.reshape(batch * seq, D_MODEL)
    xl = x_sample.reshape(dec_batch * dec_seq, D_MODEL)
    past = cache_k.shape[2]

    new_k, new_v = [], []
    for l in range(depth):
        mod = _ada(cond, w_ada[l], b_ada[l])
        w_in_b = w_in[l].astype(_BF16)
        w_out_b = w_out[l].astype(_BF16)
        w_router_t = w_router[l].T

        q_p, k_p, v_p, bg_p, cu_p = _inproj(xp, mod, g_norm1[l], w_in_b, g_q[l], g_k[l], False, seq)
        q_l, k_l, v_l, bg_l, cu_l = _inproj(xl, mod, g_norm1[l], w_in_b, g_q[l], g_k[l], True, dec_seq)

        shared = (mod, conv_w[l], g_attn_out[l], g_conv_out[l], w_out_b, g_norm2[l], w_router_t)
        kv_p = [(k_p.reshape(batch, seq, D_KV), v_p.reshape(batch, seq, D_KV))]
        kv_l = [(cache_k[:, l].reshape(dec_batch, past, D_KV), cache_v[:, l].reshape(dec_batch, past, D_KV)),
                (k_l.reshape(dec_batch, dec_seq, D_KV), v_l.reshape(dec_batch, dec_seq, D_KV))]
        xp1, h2_p, lg_p = _mix(xp, q_p, kv_p, bg_p, cu_p, *shared, False, seq)
        xl1, h2_l, lg_l = _mix(xl, q_l, kv_l, bg_l, cu_l, *shared, True, dec_seq)

        xs_p, gw_p = _route(lg_p, h2_p, seq, N_EXPERTS)
        xs_l, gw_l = _route(lg_l, h2_l, dec_seq, LATENT_ROUTE_EXPERTS)
        y_p, y_l = _ffn(xs_p, xs_l, w_gate[l], w_up[l], w_down[l])
        xp = _combine(xp1, y_p, gw_p, mod, False, seq)
        xl = _combine(xl1, y_l, gw_l, mod, True, dec_seq)

        new_k.append(k_p.reshape(batch, seq, N_KV_HEADS, HEAD_DIM))
        new_v.append(v_p.reshape(batch, seq, N_KV_HEADS, HEAD_DIM))

    return (xp.reshape(batch, seq, D_MODEL), xl.reshape(dec_batch, dec_seq, D_MODEL),
            jnp.stack(new_k, axis=1), jnp.stack(new_v, axis=1))
```

```python
import functools

import numpy as np
import jax
import jax.numpy as jnp
from jax import lax
from jax.experimental import pallas as pl
from jax.experimental.pallas import tpu as pltpu

D_MODEL = 2048
GRID_W = 64
HEAD_DIM = 128
D_ATTN = 1024
D_CONV = 1024
N_HEADS = 8
N_KV_HEADS = 2
GQA_GROUP = N_HEADS // N_KV_HEADS
D_KV = N_KV_HEADS * HEAD_DIM
N_EXPERTS = 16
EC_CAPACITY_FACTOR = 2
D_FF = 1024
ROPE_THETA = 10000.0
RMS_EPS = 1e-6
N_MOD = 6
D_IN_PROJ = D_ATTN + 2 * D_KV + 3 * D_CONV

TOKEN_TILE = 256
ADA_COL_TILE = 1024
FF_TILE = 256
FFN_ROW_CHUNK = 256
COND_ROWS = 8
HALO_ROWS = 8
LATENT_ROUTE_EXPERTS = 4
VMEM_LIMIT = 52 * 1024 * 1024

_BF16 = jnp.bfloat16
_F32 = jnp.float32


def _params(n_axes):
    return pltpu.CompilerParams(dimension_semantics=("arbitrary",) * n_axes,
                                vmem_limit_bytes=VMEM_LIMIT)


def _resident(shape):
    return pl.BlockSpec(shape, lambda *_: (0,) * len(shape), pipeline_mode=pl.Buffered(1))


def _dot(a, b):
    return jnp.dot(a, b, preferred_element_type=_F32)


def _dot_nt(a, b):
    return lax.dot_general(a, b, (((1,), (1,)), ((), ())), preferred_element_type=_F32)


def _rms(x):
    return x * lax.rsqrt(jnp.mean(x * x, axis=-1, keepdims=True) + RMS_EPS)


def _split_bf16(x):
    hi = x.astype(_BF16)
    lo = (x - hi.astype(_F32)).astype(_BF16)
    return hi, lo


def _ada_kernel(c_ref, w_ref, b_ref, o_ref):
    c = c_ref[...]
    a = c * jax.nn.sigmoid(c)
    a_hi, a_lo = _split_bf16(a)
    w = w_ref[...]
    w_hi, w_lo = _split_bf16(w)
    o_ref[...] = _dot(a_hi, w_hi) + (_dot(a_lo, w_hi) + _dot(a_hi, w_lo)) + b_ref[...]


def _ada(cond, w_ada, b_ada):
    per_mod = D_MODEL // ADA_COL_TILE
    return pl.pallas_call(
        _ada_kernel,
        grid=(N_MOD, per_mod),
        in_specs=[
            pl.BlockSpec((COND_ROWS, D_MODEL), lambda j, h: (0, 0)),
            pl.BlockSpec((D_MODEL, ADA_COL_TILE), lambda j, h: (0, j * per_mod + h)),
            pl.BlockSpec((1, ADA_COL_TILE), lambda j, h: (0, j * per_mod + h)),
        ],
        out_specs=pl.BlockSpec((None, COND_ROWS, ADA_COL_TILE), lambda j, h: (j, 0, h)),
        out_shape=jax.ShapeDtypeStruct((N_MOD, COND_ROWS, D_MODEL), _F32),
        compiler_params=_params(2),
        name="ada",
    )(cond, w_ada, b_ada.reshape(1, N_MOD * D_MODEL))


def _rope_tables(seq):
    t = jnp.arange(seq)
    row = (t // GRID_W).astype(_F32)
    col = (t % GRID_W).astype(_F32)
    half = HEAD_DIM // 2
    inv_freq = ROPE_THETA ** (-jnp.arange(0, half, 2, dtype=_F32) / half)
    ang_r = row[:, None] * inv_freq[None, :]
    ang_c = col[:, None] * inv_freq[None, :]
    cos = jnp.concatenate([jnp.cos(ang_r)] * 2 + [jnp.cos(ang_c)] * 2, axis=-1)
    sin = jnp.concatenate([-jnp.sin(ang_r), jnp.sin(ang_r), -jnp.sin(ang_c), jnp.sin(ang_c)], axis=-1)
    return cos, sin


def _cond_row(latent, seq_tiles):
    if not latent:
        return 0
    return 1 + pl.program_id(0) // seq_tiles


def _inproj_kernel(latent, seq_tiles, x_ref, mod_ref, g1_ref, w_ref, gq_ref, gk_ref, *rest):
    if latent:
        cos_ref, sin_ref, q_ref, k_ref, v_ref, bg_ref, cu_ref = rest
    else:
        q_ref, k_ref, v_ref, bg_ref, cu_ref, kstate_ref, vstate_ref = rest
    r = _cond_row(latent, seq_tiles)
    sh = mod_ref[0, pl.ds(r, 1), :]
    sc = mod_ref[1, pl.ds(r, 1), :]
    h = _rms(x_ref[...]) * g1_ref[...] * (1.0 + sc) + sh
    hb = h.astype(_BF16)

    if latent:
        cos = cos_ref[...]
        sin = sin_ref[...]
        lane = lax.broadcasted_iota(jnp.int32, cos.shape, 1)
        first = (lane % (HEAD_DIM // 2)) < (HEAD_DIM // 4)

    def head_norm(xh, g):
        xh = _rms(xh) * g
        if latent:
            partner = jnp.where(first, pltpu.roll(xh, HEAD_DIM - HEAD_DIM // 4, axis=1),
                                pltpu.roll(xh, HEAD_DIM // 4, axis=1))
            xh = xh * cos + partner * sin
        return xh

    scale = HEAD_DIM ** -0.5
    q = _dot(hb, w_ref[:, 0:D_ATTN])
    for hd in range(N_HEADS):
        cols = slice(hd * HEAD_DIM, (hd + 1) * HEAD_DIM)
        q_ref[:, cols] = (head_norm(q[:, cols], gq_ref[...]) * scale).astype(q_ref.dtype)
    k = _dot(hb, w_ref[:, D_ATTN:D_ATTN + D_KV])
    o = D_ATTN + D_KV
    v = _dot(hb, w_ref[:, o:o + D_KV])
    for hd in range(N_KV_HEADS):
        cols = slice(hd * HEAD_DIM, (hd + 1) * HEAD_DIM)
        kh = head_norm(k[:, cols], gk_ref[...])
        k_ref[:, cols] = kh
        v_ref[:, cols] = v[:, cols]
        if not latent:
            kstate_ref[:, hd, :] = kh
            vstate_ref[:, hd, :] = v[:, cols]
    o += D_KV
    bg_ref[...] = _dot(hb, w_ref[:, o:o + D_CONV])
    o += D_CONV
    cg = _dot(hb, w_ref[:, o:o + D_CONV])
    o += D_CONV
    cu_ref[...] = cg * _dot(hb, w_ref[:, o:o + D_CONV])


def _inproj(x, mod, g_norm1, w_in_b, g_q, g_k, latent, seq):
    n_tok = x.shape[0]
    seq_tiles = seq // TOKEN_TILE
    tile = lambda w: pl.BlockSpec((TOKEN_TILE, w), lambda i: (i, 0))
    in_specs = [
        tile(D_MODEL),
        _resident((N_MOD, COND_ROWS, D_MODEL)),
        _resident((1, D_MODEL)),
        _resident((D_MODEL, D_IN_PROJ)),
        _resident((1, HEAD_DIM)),
        _resident((1, HEAD_DIM)),
    ]
    args = [x, mod, g_norm1.reshape(1, D_MODEL), w_in_b, g_q.reshape(1, HEAD_DIM), g_k.reshape(1, HEAD_DIM)]
    if latent:
        cos, sin = _rope_tables(seq)
        rope_spec = pl.BlockSpec((TOKEN_TILE, HEAD_DIM), lambda i: (i % seq_tiles, 0))
        in_specs += [rope_spec, rope_spec]
        args += [cos, sin]
    out_specs = [tile(D_ATTN), tile(D_KV), tile(D_KV), tile(D_CONV), tile(D_CONV)]
    out_shape = [
        jax.ShapeDtypeStruct((n_tok, D_ATTN), _BF16),
        jax.ShapeDtypeStruct((n_tok, D_KV), _F32),
        jax.ShapeDtypeStruct((n_tok, D_KV), _F32),
        jax.ShapeDtypeStruct((n_tok, D_CONV), _F32),
        jax.ShapeDtypeStruct((n_tok, D_CONV), _F32),
    ]
    if not latent:
        state_spec = pl.BlockSpec((TOKEN_TILE, N_KV_HEADS, HEAD_DIM), lambda i: (i, 0, 0))
        out_specs += [state_spec, state_spec]
        out_shape += [jax.ShapeDtypeStruct((n_tok, N_KV_HEADS, HEAD_DIM), _F32)] * 2
    return pl.pallas_call(
        functools.partial(_inproj_kernel, latent, seq_tiles),
        grid=(n_tok // TOKEN_TILE,),
        in_specs=in_specs,
        out_specs=out_specs,
        out_shape=out_shape,
        compiler_params=_params(1),
        name="inproj_lat" if latent else "inproj_ctx",
    )(*args)


def _mix_kernel(latent, seq_tiles, n_tiles, x_ref, q_ref, *rest):
    n_kv = 2 if latent else 1
    kv_refs = rest[:2 * n_kv]
    rest = rest[2 * n_kv:]
    bg_ref, cu_ref = rest[:2]
    rest = rest[2:]
    if latent:
        cu_prev_ref, cu_next_ref = rest[:2]
        rest = rest[2:]
    (convw_ref, ga_ref, gc_ref, wout_ref, mod_ref, g2_ref, wr_ref,
     xo_ref, h2_ref, lg_ref, mixed_scr) = rest

    s = pl.program_id(0)

    @pl.when(s == 0)
    def _():
        mixed_scr[...] = jnp.zeros_like(mixed_scr)

    r = 1 + jnp.maximum(s - 1, 0) // seq_tiles if latent else 0
    y = _dot(mixed_scr[...], wout_ref[...])
    x_new = x_ref[...] + mod_ref[2, pl.ds(r, 1), :] * y
    xo_ref[...] = x_new
    h2 = _rms(x_new) * g2_ref[...] * (1.0 + mod_ref[4, pl.ds(r, 1), :]) + mod_ref[3, pl.ds(r, 1), :]
    h2_hi, h2_lo = _split_bf16(h2)
    h2_ref[...] = h2_hi
    wr_hi, wr_lo = _split_bf16(wr_ref[...])
    lg_ref[...] = _dot_nt(wr_hi, h2_hi) + (_dot_nt(wr_hi, h2_lo) + _dot_nt(wr_lo, h2_hi))

    cu = cu_ref[...]
    row = lax.broadcasted_iota(jnp.int32, cu.shape, 0)
    if latent:
        pos = jnp.minimum(s, n_tiles - 1) % seq_tiles
        edge_prev = jnp.where(pos == 0, 0.0, 1.0) * cu_prev_ref[HALO_ROWS - 1:HALO_ROWS, :]
        edge_next = jnp.where(pos == seq_tiles - 1, 0.0, 1.0) * cu_next_ref[0:1, :]
    else:
        edge_prev = jnp.zeros((1, D_CONV), _F32)
        edge_next = edge_prev
    prev = jnp.where(row == 0, edge_prev, pltpu.roll(cu, 1, axis=0))
    nxt = jnp.where(row == TOKEN_TILE - 1, edge_next, pltpu.roll(cu, TOKEN_TILE - 1, axis=0))
    conv = bg_ref[...] * (prev * convw_ref[0:1, :] + cu * convw_ref[1:2, :] + nxt * convw_ref[2:3, :])
    conv_n = (_rms(conv) * gc_ref[...]).astype(_BF16)

    heads = []
    for kvh in range(N_KV_HEADS):
        cols = slice(kvh * HEAD_DIM, (kvh + 1) * HEAD_DIM)
        ks = [kv_refs[2 * p][:, cols].astype(_BF16) for p in range(n_kv)]
        vs = [kv_refs[2 * p + 1][:, cols].astype(_BF16) for p in range(n_kv)]
        for g in range(GQA_GROUP):
            hc = slice((kvh * GQA_GROUP + g) * HEAD_DIM, (kvh * GQA_GROUP + g + 1) * HEAD_DIM)
            qh = q_ref[:, hc]
            sc = [_dot_nt(qh, kp) for kp in ks]
            m = functools.reduce(jnp.maximum, [jnp.max(sp, axis=-1, keepdims=True) for sp in sc])
            p = [jnp.exp(sp - m) for sp in sc]
            l = functools.reduce(jnp.add, [jnp.sum(pp, axis=-1, keepdims=True) for pp in p])
            o = functools.reduce(jnp.add, [_dot(pp.astype(_BF16), vp) for pp, vp in zip(p, vs)])
            heads.append(o / l)
    ssq = functools.reduce(jnp.add, [jnp.sum(h * h, axis=-1, keepdims=True) for h in heads])
    inv = lax.rsqrt(ssq * (1.0 / D_ATTN) + RMS_EPS)
    for hd, h in enumerate(heads):
        hc = slice(hd * HEAD_DIM, (hd + 1) * HEAD_DIM)
        mixed_scr[:, hc] = (h * inv * ga_ref[:, hc]).astype(_BF16)
    mixed_scr[:, D_ATTN:D_ATTN + D_CONV] = conv_n


def _mix(x, q, kv_parts, bg, cu, mod, conv_w, g_attn_out, g_conv_out, w_out_b, g_norm2, w_router_t,
         latent, seq):
    n_tok = x.shape[0]
    seq_tiles = seq // TOKEN_TILE
    n_tiles = n_tok // TOKEN_TILE
    started = lambda i: jnp.minimum(i, n_tiles - 1)
    finished = lambda i: jnp.maximum(i - 1, 0)
    start_tile = lambda w: pl.BlockSpec((TOKEN_TILE, w), lambda i: (started(i), 0))
    finish_tile = lambda w: pl.BlockSpec((TOKEN_TILE, w), lambda i: (finished(i), 0))
    in_specs = [finish_tile(D_MODEL), start_tile(D_ATTN)]
    args = [x, q]
    for k, v in kv_parts:
        spec = pl.BlockSpec((None, k.shape[1], D_KV), lambda i: (started(i) // seq_tiles, 0, 0))
        in_specs += [spec, spec]
        args += [k, v]
    in_specs += [start_tile(D_CONV), start_tile(D_CONV)]
    args += [bg, cu]
    if latent:
        per_tile = TOKEN_TILE // HALO_ROWS
        last = n_tok // HALO_ROWS - 1
        in_specs += [
            pl.BlockSpec((HALO_ROWS, D_CONV), lambda i: (jnp.maximum(started(i) * per_tile - 1, 0), 0)),
            pl.BlockSpec((HALO_ROWS, D_CONV), lambda i: (jnp.minimum((started(i) + 1) * per_tile, last), 0)),
        ]
        args += [cu, cu]
    in_specs += [
        _resident((3, D_CONV)),
        _resident((1, D_ATTN)),
        _resident((1, D_CONV)),
        _resident((D_ATTN + D_CONV, D_MODEL)),
        _resident((N_MOD, COND_ROWS, D_MODEL)),
        _resident((1, D_MODEL)),
        _resident((N_EXPERTS, D_MODEL)),
    ]
    args += [conv_w, g_attn_out.reshape(1, D_ATTN), g_conv_out.reshape(1, D_CONV), w_out_b, mod,
             g_norm2.reshape(1, D_MODEL), w_router_t]
    return pl.pallas_call(
        functools.partial(_mix_kernel, latent, seq_tiles, n_tiles),
        grid=(n_tiles + 1,),
        in_specs=in_specs,
        out_specs=[finish_tile(D_MODEL), finish_tile(D_MODEL),
                   pl.BlockSpec((N_EXPERTS, TOKEN_TILE), lambda i: (0, finished(i)))],
        out_shape=[
            jax.ShapeDtypeStruct((n_tok, D_MODEL), _F32),
            jax.ShapeDtypeStruct((n_tok, D_MODEL), _BF16),
            jax.ShapeDtypeStruct((N_EXPERTS, n_tok), _F32),
        ],
        scratch_shapes=[pltpu.VMEM((TOKEN_TILE, D_ATTN + D_CONV), _BF16)],
        compiler_params=_params(1),
        name="mix_lat" if latent else "mix_ctx",
    )(*args)


def _route_kernel(seq, cap, e_blk, lg_ref, h2_ref, xs_ref, gw_ref, aff_scr, key_scr):
    j = pl.program_id(1)

    @pl.when(j == 0)
    def _():
        lg = lg_ref[...]
        ex = jnp.exp(lg - jnp.max(lg, axis=0, keepdims=True))
        aff = ex / jnp.sum(ex, axis=0, keepdims=True)
        aff_t = jnp.concatenate([aff, jnp.zeros((HEAD_DIM - N_EXPERTS, seq), _F32)], axis=0).T

        other = lax.broadcasted_iota(jnp.int32, (seq, seq), 0)
        token = lax.broadcasted_iota(jnp.int32, (seq, seq), 1)
        earlier = jnp.where(other < token, 1.0, 0.0)
        ranks = []
        for e in range(N_EXPERTS):
            mine = aff[e:e + 1, :]
            theirs = aff_t[:, e:e + 1]
            beats = jnp.where(theirs > mine, 1.0, jnp.where(theirs == mine, earlier, 0.0))
            ranks.append(jnp.sum(beats, axis=0, keepdims=True))
        rank = jnp.concatenate(ranks, axis=0)
        sel = rank < float(cap)
        slot = _dot(jnp.where(sel, 1.0, 0.0).astype(_BF16), earlier.astype(_BF16))
        aff_scr[...] = aff
        key_scr[...] = jnp.where(sel, slot, -1.0)

    c_idx = lax.broadcasted_iota(jnp.int32, (cap, seq), 0).astype(_F32)
    onehot, gated = [], []
    for i in range(e_blk):
        e = j * e_blk + i
        hit = c_idx == key_scr[pl.ds(e, 1), :]
        onehot.append(jnp.where(hit, 1.0, 0.0))
        gated.append(jnp.where(hit, aff_scr[pl.ds(e, 1), :], 0.0))
    xs = _dot(jnp.concatenate(onehot, axis=0).astype(_BF16), h2_ref[...]).astype(xs_ref.dtype)
    for i in range(e_blk):
        xs_ref[i] = xs[i * cap:(i + 1) * cap, :]
    gw_ref[...] = jnp.concatenate(gated, axis=0).astype(gw_ref.dtype)


def _route(logits_t, h2, seq, e_blk):
    n_tok = h2.shape[0]
    n_sets = n_tok // seq
    cap = EC_CAPACITY_FACTOR * seq // N_EXPERTS
    return pl.pallas_call(
        functools.partial(_route_kernel, seq, cap, e_blk),
        grid=(n_sets, N_EXPERTS // e_blk),
        in_specs=[
            pl.BlockSpec((N_EXPERTS, seq), lambda s, j: (0, s)),
            pl.BlockSpec((seq, D_MODEL), lambda s, j: (s, 0)),
        ],
        out_specs=[
            pl.BlockSpec((e_blk, cap, D_MODEL), lambda s, j: (j, s, 0)),
            pl.BlockSpec((None, e_blk * cap, seq), lambda s, j: (s, j, 0)),
        ],
        out_shape=[
            jax.ShapeDtypeStruct((N_EXPERTS, n_sets * cap, D_MODEL), _BF16),
            jax.ShapeDtypeStruct((n_sets, N_EXPERTS * cap, seq), _BF16),
        ],
        scratch_shapes=[pltpu.VMEM((N_EXPERTS, seq), _F32), pltpu.VMEM((N_EXPERTS, seq), _F32)],
        compiler_params=_params(2),
        name="route_%d" % seq,
    )(logits_t, h2)


def _ffn_kernel(xa_ref, xb_ref, wg_ref, wu_ref, wd_ref, ya_ref, yb_ref, acc_a, acc_b):
    def run(first):
        wg = wg_ref[...].astype(_BF16)
        wu = wu_ref[...].astype(_BF16)
        wd = wd_ref[...].astype(_BF16)
        for x_ref, y_ref, acc in ((xa_ref, ya_ref, acc_a), (xb_ref, yb_ref, acc_b)):
            for c in range(x_ref.shape[0] // FFN_ROW_CHUNK):
                rows = slice(c * FFN_ROW_CHUNK, (c + 1) * FFN_ROW_CHUNK)
                x = x_ref[rows, :]
                a = _dot(x, wg)
                u = _dot(x, wu)
                total = _dot((a * jax.nn.sigmoid(a) * u).astype(_BF16), wd)
                if not first:
                    total = acc[rows, :] + total
                acc[rows, :] = total
                y_ref[rows, :] = total.astype(y_ref.dtype)

    @pl.when(pl.program_id(1) == 0)
    def _():
        run(True)

    @pl.when(pl.program_id(1) > 0)
    def _():
        run(False)


def _ffn(xs_a, xs_b, w_gate, w_up, w_down):
    rows_a, rows_b = xs_a.shape[1], xs_b.shape[1]
    rows = lambda n: pl.BlockSpec((None, n, D_MODEL), lambda e, f: (e, 0, 0))
    return pl.pallas_call(
        _ffn_kernel,
        grid=(N_EXPERTS, D_FF // FF_TILE),
        in_specs=[
            rows(rows_a), rows(rows_b),
            pl.BlockSpec((None, D_MODEL, FF_TILE), lambda e, f: (e, 0, f)),
            pl.BlockSpec((None, D_MODEL, FF_TILE), lambda e, f: (e, 0, f)),
            pl.BlockSpec((None, FF_TILE, D_MODEL), lambda e, f: (e, f, 0)),
        ],
        out_specs=[rows(rows_a), rows(rows_b)],
        out_shape=[jax.ShapeDtypeStruct(xs_a.shape, _BF16), jax.ShapeDtypeStruct(xs_b.shape, _BF16)],
        scratch_shapes=[pltpu.VMEM((rows_a, D_MODEL), _F32), pltpu.VMEM((rows_b, D_MODEL), _F32)],
        compiler_params=_params(2),
        name="ffn",
    )(xs_a, xs_b, w_gate, w_up, w_down)


def _combine_kernel(latent, x_ref, y_ref, gw_ref, mod_ref, o_ref):
    r = 1 + pl.program_id(0) if latent else 0
    n_e, cap, d = y_ref.shape
    y = y_ref[...].reshape(n_e * cap, d)
    moe = lax.dot_general(gw_ref[...], y, (((0,), (0,)), ((), ())), preferred_element_type=_F32)
    o_ref[...] = x_ref[...] + mod_ref[5, pl.ds(r, 1), :] * moe


def _combine(x_new, y, gw, mod, latent, seq):
    n_tok = x_new.shape[0]
    n_sets = n_tok // seq
    cap = EC_CAPACITY_FACTOR * seq // N_EXPERTS
    seq_tiles = seq // TOKEN_TILE
    tile = pl.BlockSpec((TOKEN_TILE, D_MODEL), lambda s, t: (s * seq_tiles + t, 0))
    return pl.pallas_call(
        functools.partial(_combine_kernel, latent),
        grid=(n_sets, seq_tiles),
        in_specs=[
            tile,
            pl.BlockSpec((N_EXPERTS, cap, D_MODEL), lambda s, t: (0, s, 0)),
            pl.BlockSpec((None, N_EXPERTS * cap, TOKEN_TILE), lambda s, t: (s, 0, t)),
            _resident((N_MOD, COND_ROWS, D_MODEL)),
        ],
        out_specs=tile,
        out_shape=jax.ShapeDtypeStruct((n_tok, D_MODEL), _F32),
        compiler_params=_params(2),
        name="combine_lat" if latent else "combine_ctx",
    )(x_new, y, gw, mod)


def kernel(x_prompt, x_sample, cache_k, cache_v, c, c_ctx, w_ada, b_ada, g_norm1, w_in, g_q, g_k, conv_w,
           g_attn_out, g_conv_out, w_out, g_norm2, w_router, w_gate, w_up, w_down):
    batch, seq, _ = x_prompt.shape
    dec_batch, dec_seq, _ = x_sample.shape
    depth = w_ada.shape[0]
    assert 1 + dec_batch <= COND_ROWS

    cond = jnp.concatenate([c_ctx[None, :], c, jnp.zeros((COND_ROWS - 1 - dec_batch, D_MODEL), _F32)], axis=0)
    xp = x_prompt.reshape(batch * seq, D_MODEL)
    xl = x_sample.reshape(dec_batch * dec_seq, D_MODEL)
    past = cache_k.shape[2]

    new_k, new_v = [], []
    for l in range(depth):
        mod = _ada(cond, w_ada[l], b_ada[l])
        w_in_b = w_in[l].astype(_BF16)
        w_out_b = w_out[l].astype(_BF16)
        w_router_t = w_router[l].T

        q_p, k_p, v_p, bg_p, cu_p, kstate, vstate = _inproj(xp, mod, g_norm1[l], w_in_b, g_q[l], g_k[l],
                                                            False, seq)
        q_l, k_l, v_l, bg_l, cu_l = _inproj(xl, mod, g_norm1[l], w_in_b, g_q[l], g_k[l], True, dec_seq)

        shared = (mod, conv_w[l], g_attn_out[l], g_conv_out[l], w_out_b, g_norm2[l], w_router_t)
        kv_p = [(k_p.reshape(batch, seq, D_KV), v_p.reshape(batch, seq, D_KV))]
        kv_l = [(cache_k[:, l].reshape(dec_batch, past, D_KV), cache_v[:, l].reshape(dec_batch, past, D_KV)),
                (k_l.reshape(dec_batch, dec_seq, D_KV), v_l.reshape(dec_batch, dec_seq, D_KV))]
        xp1, h2_p, lg_p = _mix(xp, q_p, kv_p, bg_p, cu_p, *shared, False, seq)
        xl1, h2_l, lg_l = _mix(xl, q_l, kv_l, bg_l, cu_l, *shared, True, dec_seq)

        xs_p, gw_p = _route(lg_p, h2_p, seq, N_EXPERTS)
        xs_l, gw_l = _route(lg_l, h2_l, dec_seq, LATENT_ROUTE_EXPERTS)
        y_p, y_l = _ffn(xs_p, xs_l, w_gate[l], w_up[l], w_down[l])
        xp = _combine(xp1, y_p, gw_p, mod, False, seq)
        xl = _combine(xl1, y_l, gw_l, mod, True, dec_seq)

        new_k.append(kstate.reshape(batch, seq, N_KV_HEADS, HEAD_DIM))
        new_v.append(vstate.reshape(batch, seq, N_KV_HEADS, HEAD_DIM))

    return (xp.reshape(batch, seq, D_MODEL), xl.reshape(dec_batch, dec_seq, D_MODEL),
            jnp.stack(new_k, axis=1), jnp.stack(new_v, axis=1))
```

```python
import functools

import numpy as np
import jax
import jax.numpy as jnp
from jax import lax
from jax.experimental import pallas as pl
from jax.experimental.pallas import tpu as pltpu

D_MODEL = 2048
GRID_W = 64
HEAD_DIM = 128
D_ATTN = 1024
D_CONV = 1024
N_HEADS = 8
N_KV_HEADS = 2
GQA_GROUP = N_HEADS // N_KV_HEADS
D_KV = N_KV_HEADS * HEAD_DIM
N_EXPERTS = 16
EC_CAPACITY_FACTOR = 2
D_FF = 1024
ROPE_THETA = 10000.0
RMS_EPS = 1e-6
LOG2_E = 1.4426950408889634
N_MOD = 6
D_IN_PROJ = D_ATTN + 2 * D_KV + 3 * D_CONV

TOKEN_TILE = 256
ADA_COL_TILE = 1024
FF_TILE = 256
FFN_ROW_CHUNK = 256
COND_ROWS = 8
HALO_ROWS = 8
LATENT_ROUTE_EXPERTS = 4
VMEM_LIMIT = 52 * 1024 * 1024

_BF16 = jnp.bfloat16
_F32 = jnp.float32


def _params(n_axes):
    return pltpu.CompilerParams(dimension_semantics=("arbitrary",) * n_axes,
                                vmem_limit_bytes=VMEM_LIMIT)


def _resident(shape):
    return pl.BlockSpec(shape, lambda *_: (0,) * len(shape), pipeline_mode=pl.Buffered(1))


def _dot(a, b):
    return jnp.dot(a, b, preferred_element_type=_F32)


def _dot_nt(a, b):
    return lax.dot_general(a, b, (((1,), (1,)), ((), ())), preferred_element_type=_F32)


def _rms(x):
    return x * lax.rsqrt(jnp.mean(x * x, axis=-1, keepdims=True) + RMS_EPS)


def _split_bf16(x):
    hi = x.astype(_BF16)
    lo = (x - hi.astype(_F32)).astype(_BF16)
    return hi, lo


def _ada_kernel(c_ref, w_ref, b_ref, o_ref):
    c = c_ref[...]
    a = c * jax.nn.sigmoid(c)
    a_hi, a_lo = _split_bf16(a)
    w = w_ref[...]
    w_hi, w_lo = _split_bf16(w)
    o_ref[...] = _dot(a_hi, w_hi) + (_dot(a_lo, w_hi) + _dot(a_hi, w_lo)) + b_ref[...]


def _ada(cond, w_ada, b_ada):
    per_mod = D_MODEL // ADA_COL_TILE
    return pl.pallas_call(
        _ada_kernel,
        grid=(N_MOD, per_mod),
        in_specs=[
            pl.BlockSpec((COND_ROWS, D_MODEL), lambda j, h: (0, 0)),
            pl.BlockSpec((D_MODEL, ADA_COL_TILE), lambda j, h: (0, j * per_mod + h)),
            pl.BlockSpec((1, ADA_COL_TILE), lambda j, h: (0, j * per_mod + h)),
        ],
        out_specs=pl.BlockSpec((None, COND_ROWS, ADA_COL_TILE), lambda j, h: (j, 0, h)),
        out_shape=jax.ShapeDtypeStruct((N_MOD, COND_ROWS, D_MODEL), _F32),
        compiler_params=_params(2),
        name="ada",
    )(cond, w_ada, b_ada.reshape(1, N_MOD * D_MODEL))


def _rope_tables(seq):
    t = jnp.arange(seq)
    row = (t // GRID_W).astype(_F32)
    col = (t % GRID_W).astype(_F32)
    half = HEAD_DIM // 2
    inv_freq = ROPE_THETA ** (-jnp.arange(0, half, 2, dtype=_F32) / half)
    ang_r = row[:, None] * inv_freq[None, :]
    ang_c = col[:, None] * inv_freq[None, :]
    cos = jnp.concatenate([jnp.cos(ang_r)] * 2 + [jnp.cos(ang_c)] * 2, axis=-1)
    sin = jnp.concatenate([-jnp.sin(ang_r), jnp.sin(ang_r), -jnp.sin(ang_c), jnp.sin(ang_c)], axis=-1)
    return cos, sin


def _cond_row(latent, seq_tiles):
    if not latent:
        return 0
    return 1 + pl.program_id(0) // seq_tiles


def _inproj_kernel(latent, seq_tiles, x_ref, mod_ref, g1_ref, w_ref, gq_ref, gk_ref, *rest):
    if latent:
        cos_ref, sin_ref, q_ref, k_ref, v_ref, bg_ref, cu_ref = rest
    else:
        q_ref, k_ref, v_ref, bg_ref, cu_ref, kstate_ref, vstate_ref = rest
    r = _cond_row(latent, seq_tiles)
    sh = mod_ref[0, pl.ds(r, 1), :]
    sc = mod_ref[1, pl.ds(r, 1), :]
    h = _rms(x_ref[...]) * (g1_ref[...] * (1.0 + sc)) + sh
    hb = h.astype(_BF16)

    if latent:
        cos = cos_ref[...]
        sin = sin_ref[...]
        lane = lax.broadcasted_iota(jnp.int32, cos.shape, 1)
        first = (lane % (HEAD_DIM // 2)) < (HEAD_DIM // 4)

    def head_norm(xh, g):
        xh = _rms(xh) * g
        if latent:
            partner = jnp.where(first, pltpu.roll(xh, HEAD_DIM - HEAD_DIM // 4, axis=1),
                                pltpu.roll(xh, HEAD_DIM // 4, axis=1))
            xh = xh * cos + partner * sin
        return xh

    gq = gq_ref[...] * (HEAD_DIM ** -0.5 * LOG2_E)
    q = _dot(hb, w_ref[:, 0:D_ATTN])
    for hd in range(N_HEADS):
        cols = slice(hd * HEAD_DIM, (hd + 1) * HEAD_DIM)
        q_ref[:, cols] = head_norm(q[:, cols], gq).astype(q_ref.dtype)
    k = _dot(hb, w_ref[:, D_ATTN:D_ATTN + D_KV])
    o = D_ATTN + D_KV
    v = _dot(hb, w_ref[:, o:o + D_KV])
    for hd in range(N_KV_HEADS):
        cols = slice(hd * HEAD_DIM, (hd + 1) * HEAD_DIM)
        kh = head_norm(k[:, cols], gk_ref[...])
        k_ref[:, cols] = kh
        v_ref[:, cols] = v[:, cols]
        if not latent:
            kstate_ref[:, hd, :] = kh
            vstate_ref[:, hd, :] = v[:, cols]
    o += D_KV
    bg_ref[...] = _dot(hb, w_ref[:, o:o + D_CONV])
    o += D_CONV
    cg = _dot(hb, w_ref[:, o:o + D_CONV])
    o += D_CONV
    cu_ref[...] = cg * _dot(hb, w_ref[:, o:o + D_CONV])


def _inproj(x, mod, g_norm1, w_in_b, g_q, g_k, latent, seq):
    n_tok = x.shape[0]
    seq_tiles = seq // TOKEN_TILE
    tile = lambda w: pl.BlockSpec((TOKEN_TILE, w), lambda i: (i, 0))
    in_specs = [
        tile(D_MODEL),
        _resident((N_MOD, COND_ROWS, D_MODEL)),
        _resident((1, D_MODEL)),
        _resident((D_MODEL, D_IN_PROJ)),
        _resident((1, HEAD_DIM)),
        _resident((1, HEAD_DIM)),
    ]
    args = [x, mod, g_norm1.reshape(1, D_MODEL), w_in_b, g_q.reshape(1, HEAD_DIM), g_k.reshape(1, HEAD_DIM)]
    if latent:
        cos, sin = _rope_tables(seq)
        rope_spec = pl.BlockSpec((TOKEN_TILE, HEAD_DIM), lambda i: (i % seq_tiles, 0))
        in_specs += [rope_spec, rope_spec]
        args += [cos, sin]
    out_specs = [tile(D_ATTN), tile(D_KV), tile(D_KV), tile(D_CONV), tile(D_CONV)]
    out_shape = [
        jax.ShapeDtypeStruct((n_tok, D_ATTN), _BF16),
        jax.ShapeDtypeStruct((n_tok, D_KV), _F32),
        jax.ShapeDtypeStruct((n_tok, D_KV), _F32),
        jax.ShapeDtypeStruct((n_tok, D_CONV), _F32),
        jax.ShapeDtypeStruct((n_tok, D_CONV), _F32),
    ]
    if not latent:
        state_spec = pl.BlockSpec((TOKEN_TILE, N_KV_HEADS, HEAD_DIM), lambda i: (i, 0, 0))
        out_specs += [state_spec, state_spec]
        out_shape += [jax.ShapeDtypeStruct((n_tok, N_KV_HEADS, HEAD_DIM), _F32)] * 2
    return pl.pallas_call(
        functools.partial(_inproj_kernel, latent, seq_tiles),
        grid=(n_tok // TOKEN_TILE,),
        in_specs=in_specs,
        out_specs=out_specs,
        out_shape=out_shape,
        compiler_params=_params(1),
        name="inproj_lat" if latent else "inproj_ctx",
    )(*args)


def _mix_kernel(latent, seq_tiles, n_tiles, x_ref, q_ref, *rest):
    n_kv = 2 if latent else 1
    kv_refs = rest[:2 * n_kv]
    rest = rest[2 * n_kv:]
    bg_ref, cu_ref = rest[:2]
    rest = rest[2:]
    if latent:
        cu_prev_ref, cu_next_ref = rest[:2]
        rest = rest[2:]
    (convw_ref, ga_ref, gc_ref, wout_ref, mod_ref, g2_ref, wr_ref,
     xo_ref, h2_ref, lg_ref, mixed_scr) = rest

    s = pl.program_id(0)

    @pl.when(s == 0)
    def _():
        mixed_scr[...] = jnp.zeros_like(mixed_scr)

    r = 1 + jnp.maximum(s - 1, 0) // seq_tiles if latent else 0
    y = _dot(mixed_scr[...], wout_ref[...])
    x_new = x_ref[...] + mod_ref[2, pl.ds(r, 1), :] * y
    xo_ref[...] = x_new
    h2 = _rms(x_new) * (g2_ref[...] * (1.0 + mod_ref[4, pl.ds(r, 1), :])) + mod_ref[3, pl.ds(r, 1), :]
    h2_hi = h2.astype(_BF16)
    h2_ref[...] = h2_hi
    lg_ref[...] = _dot_nt(wr_ref[...].astype(_BF16), h2_hi)

    cu = cu_ref[...]
    if latent:
        pos = jnp.minimum(s, n_tiles - 1) % seq_tiles
        edge_prev = jnp.where(pos == 0, 0.0, 1.0) * cu_prev_ref[HALO_ROWS - 1:HALO_ROWS, :]
        edge_next = jnp.where(pos == seq_tiles - 1, 0.0, 1.0) * cu_next_ref[0:1, :]
    else:
        edge_prev = jnp.zeros((1, D_CONV), _F32)
        edge_next = edge_prev
    prev = pltpu.roll(cu, 1, axis=0)
    nxt = pltpu.roll(cu, TOKEN_TILE - 1, axis=0)
    sub = lax.broadcasted_iota(jnp.int32, (HALO_ROWS, D_CONV), 0)
    prev = jnp.concatenate([jnp.where(sub == 0, edge_prev, prev[0:HALO_ROWS]), prev[HALO_ROWS:]], axis=0)
    nxt = jnp.concatenate([nxt[:TOKEN_TILE - HALO_ROWS],
                           jnp.where(sub == HALO_ROWS - 1, edge_next, nxt[TOKEN_TILE - HALO_ROWS:])], axis=0)
    conv = bg_ref[...] * (prev * convw_ref[0:1, :] + cu * convw_ref[1:2, :] + nxt * convw_ref[2:3, :])
    conv_n = (_rms(conv) * gc_ref[...]).astype(_BF16)

    row_sum_on_mxu = latent
    heads = []
    for kvh in range(N_KV_HEADS):
        cols = slice(kvh * HEAD_DIM, (kvh + 1) * HEAD_DIM)
        ks = [kv_refs[2 * p][:, cols].astype(_BF16) for p in range(n_kv)]
        vs = [kv_refs[2 * p + 1][:, cols].astype(_BF16) for p in range(n_kv)]
        if row_sum_on_mxu:
            vs = [jnp.concatenate([vp, jnp.ones_like(vp)], axis=1) for vp in vs]
        for g in range(GQA_GROUP):
            hc = slice((kvh * GQA_GROUP + g) * HEAD_DIM, (kvh * GQA_GROUP + g + 1) * HEAD_DIM)
            qh = q_ref[:, hc]
            sc = [_dot_nt(qh, kp) for kp in ks]
            m = functools.reduce(jnp.maximum, [jnp.max(sp, axis=-1, keepdims=True) for sp in sc])
            if row_sum_on_mxu:
                ol = functools.reduce(jnp.add, [_dot(jnp.exp2(sp - m).astype(_BF16), vp) for sp, vp in zip(sc, vs)])
                heads.append(ol[:, 0:HEAD_DIM] / ol[:, HEAD_DIM:HEAD_DIM + 1])
            else:
                p = [jnp.exp2(sp - m) for sp in sc]
                l = functools.reduce(jnp.add, [jnp.sum(pp, axis=-1, keepdims=True) for pp in p])
                o = functools.reduce(jnp.add, [_dot(pp.astype(_BF16), vp) for pp, vp in zip(p, vs)])
                heads.append(o / l)
    ssq = functools.reduce(jnp.add, [jnp.sum(h * h, axis=-1, keepdims=True) for h in heads])
    inv = lax.rsqrt(ssq * (1.0 / D_ATTN) + RMS_EPS)
    for hd, h in enumerate(heads):
        hc = slice(hd * HEAD_DIM, (hd + 1) * HEAD_DIM)
        mixed_scr[:, hc] = (h * inv * ga_ref[:, hc]).astype(_BF16)
    mixed_scr[:, D_ATTN:D_ATTN + D_CONV] = conv_n


def _mix(x, q, kv_parts, bg, cu, mod, conv_w, g_attn_out, g_conv_out, w_out_b, g_norm2, w_router_t,
         latent, seq):
    n_tok = x.shape[0]
    seq_tiles = seq // TOKEN_TILE
    n_tiles = n_tok // TOKEN_TILE
    started = lambda i: jnp.minimum(i, n_tiles - 1)
    finished = lambda i: jnp.maximum(i - 1, 0)
    start_tile = lambda w: pl.BlockSpec((TOKEN_TILE, w), lambda i: (started(i), 0))
    finish_tile = lambda w: pl.BlockSpec((TOKEN_TILE, w), lambda i: (finished(i), 0))
    in_specs = [finish_tile(D_MODEL), start_tile(D_ATTN)]
    args = [x, q]
    for k, v in kv_parts:
        spec = pl.BlockSpec((None, k.shape[1], D_KV), lambda i: (started(i) // seq_tiles, 0, 0))
        in_specs += [spec, spec]
        args += [k, v]
    in_specs += [start_tile(D_CONV), start_tile(D_CONV)]
    args += [bg, cu]
    if latent:
        per_tile = TOKEN_TILE // HALO_ROWS
        last = n_tok // HALO_ROWS - 1
        in_specs += [
            pl.BlockSpec((HALO_ROWS, D_CONV), lambda i: (jnp.maximum(started(i) * per_tile - 1, 0), 0)),
            pl.BlockSpec((HALO_ROWS, D_CONV), lambda i: (jnp.minimum((started(i) + 1) * per_tile, last), 0)),
        ]
        args += [cu, cu]
    in_specs += [
        _resident((3, D_CONV)),
        _resident((1, D_ATTN)),
        _resident((1, D_CONV)),
        _resident((D_ATTN + D_CONV, D_MODEL)),
        _resident((N_MOD, COND_ROWS, D_MODEL)),
        _resident((1, D_MODEL)),
        _resident((N_EXPERTS, D_MODEL)),
    ]
    args += [conv_w, g_attn_out.reshape(1, D_ATTN), g_conv_out.reshape(1, D_CONV), w_out_b, mod,
             g_norm2.reshape(1, D_MODEL), w_router_t]
    return pl.pallas_call(
        functools.partial(_mix_kernel, latent, seq_tiles, n_tiles),
        grid=(n_tiles + 1,),
        in_specs=in_specs,
        out_specs=[finish_tile(D_MODEL), finish_tile(D_MODEL),
                   pl.BlockSpec((N_EXPERTS, TOKEN_TILE), lambda i: (0, finished(i)))],
        out_shape=[
            jax.ShapeDtypeStruct((n_tok, D_MODEL), _F32),
            jax.ShapeDtypeStruct((n_tok, D_MODEL), _BF16),
            jax.ShapeDtypeStruct((N_EXPERTS, n_tok), _F32),
        ],
        scratch_shapes=[pltpu.VMEM((TOKEN_TILE, D_ATTN + D_CONV), _BF16)],
        compiler_params=_params(1),
        name="mix_lat" if latent else "mix_ctx",
    )(*args)


def _route_kernel(seq, cap, e_blk, lg_ref, h2_ref, xs_ref, gw_ref, aff_scr, key_scr):
    j = pl.program_id(1)

    @pl.when(j == 0)
    def _():
        lg = lg_ref[...]
        ex = jnp.exp(lg - jnp.max(lg, axis=0, keepdims=True))
        aff = ex / jnp.sum(ex, axis=0, keepdims=True)
        aff_t = jnp.concatenate([aff, jnp.zeros((HEAD_DIM - N_EXPERTS, seq), _F32)], axis=0).T

        other = lax.broadcasted_iota(jnp.int32, (seq, seq), 0)
        token = lax.broadcasted_iota(jnp.int32, (seq, seq), 1)
        earlier = jnp.where(other < token, 1.0, 0.0)
        ranks = []
        for e in range(N_EXPERTS):
            mine = aff[e:e + 1, :]
            theirs = aff_t[:, e:e + 1]
            beats = jnp.where(theirs > mine, 1.0, jnp.where(theirs == mine, earlier, 0.0))
            ranks.append(jnp.sum(beats, axis=0, keepdims=True))
        rank = jnp.concatenate(ranks, axis=0)
        sel = rank < float(cap)
        slot = _dot(jnp.where(sel, 1.0, 0.0).astype(_BF16), earlier.astype(_BF16))
        aff_scr[...] = aff
        key_scr[...] = jnp.where(sel, slot, -1.0)

    c_idx = lax.broadcasted_iota(jnp.int32, (cap, seq), 0).astype(_F32)
    onehot, gated = [], []
    for i in range(e_blk):
        e = j * e_blk + i
        hit = c_idx == key_scr[pl.ds(e, 1), :]
        onehot.append(jnp.where(hit, 1.0, 0.0))
        gated.append(jnp.where(hit, aff_scr[pl.ds(e, 1), :], 0.0))
    xs = _dot(jnp.concatenate(onehot, axis=0).astype(_BF16), h2_ref[...]).astype(xs_ref.dtype)
    for i in range(e_blk):
        xs_ref[i] = xs[i * cap:(i + 1) * cap, :]
    gw_ref[...] = jnp.concatenate(gated, axis=0).astype(gw_ref.dtype)


def _route(logits_t, h2, seq, e_blk):
    n_tok = h2.shape[0]
    n_sets = n_tok // seq
    cap = EC_CAPACITY_FACTOR * seq // N_EXPERTS
    return pl.pallas_call(
        functools.partial(_route_kernel, seq, cap, e_blk),
        grid=(n_sets, N_EXPERTS // e_blk),
        in_specs=[
            pl.BlockSpec((N_EXPERTS, seq), lambda s, j: (0, s)),
            pl.BlockSpec((seq, D_MODEL), lambda s, j: (s, 0)),
        ],
        out_specs=[
            pl.BlockSpec((e_blk, cap, D_MODEL), lambda s, j: (j, s, 0)),
            pl.BlockSpec((None, e_blk * cap, seq), lambda s, j: (s, j, 0)),
        ],
        out_shape=[
            jax.ShapeDtypeStruct((N_EXPERTS, n_sets * cap, D_MODEL), _BF16),
            jax.ShapeDtypeStruct((n_sets, N_EXPERTS * cap, seq), _BF16),
        ],
        scratch_shapes=[pltpu.VMEM((N_EXPERTS, seq), _F32), pltpu.VMEM((N_EXPERTS, seq), _F32)],
        compiler_params=_params(2),
        name="route_%d" % seq,
    )(logits_t, h2)


def _ffn_kernel(xa_ref, xb_ref, wg_ref, wu_ref, wd_ref, ya_ref, yb_ref, acc_a, acc_b):
    def run(first):
        wg = wg_ref[...].astype(_BF16)
        wu = wu_ref[...].astype(_BF16)
        wd = wd_ref[...].astype(_BF16)
        for x_ref, y_ref, acc in ((xa_ref, ya_ref, acc_a), (xb_ref, yb_ref, acc_b)):
            for c in range(x_ref.shape[0] // FFN_ROW_CHUNK):
                rows = slice(c * FFN_ROW_CHUNK, (c + 1) * FFN_ROW_CHUNK)
                x = x_ref[rows, :]
                a = _dot(x, wg)
                u = _dot(x, wu)
                total = _dot((a * jax.nn.sigmoid(a) * u).astype(_BF16), wd)
                if not first:
                    total = acc[rows, :] + total
                acc[rows, :] = total
                y_ref[rows, :] = total.astype(y_ref.dtype)

    @pl.when(pl.program_id(1) == 0)
    def _():
        run(True)

    @pl.when(pl.program_id(1) > 0)
    def _():
        run(False)


def _ffn(xs_a, xs_b, w_gate, w_up, w_down):
    rows_a, rows_b = xs_a.shape[1], xs_b.shape[1]
    rows = lambda n: pl.BlockSpec((None, n, D_MODEL), lambda e, f: (e, 0, 0))
    return pl.pallas_call(
        _ffn_kernel,
        grid=(N_EXPERTS, D_FF // FF_TILE),
        in_specs=[
            rows(rows_a), rows(rows_b),
            pl.BlockSpec((None, D_MODEL, FF_TILE), lambda e, f: (e, 0, f)),
            pl.BlockSpec((None, D_MODEL, FF_TILE), lambda e, f: (e, 0, f)),
            pl.BlockSpec((None, FF_TILE, D_MODEL), lambda e, f: (e, f, 0)),
        ],
        out_specs=[rows(rows_a), rows(rows_b)],
        out_shape=[jax.ShapeDtypeStruct(xs_a.shape, _BF16), jax.ShapeDtypeStruct(xs_b.shape, _BF16)],
        scratch_shapes=[pltpu.VMEM((rows_a, D_MODEL), _F32), pltpu.VMEM((rows_b, D_MODEL), _F32)],
        compiler_params=_params(2),
        name="ffn",
    )(xs_a, xs_b, w_gate, w_up, w_down)


def _combine_kernel(latent, x_ref, y_ref, gw_ref, mod_ref, o_ref):
    r = 1 + pl.program_id(0) if latent else 0
    n_e, cap, d = y_ref.shape
    y = y_ref[...].reshape(n_e * cap, d)
    moe = lax.dot_general(gw_ref[...], y, (((0,), (0,)), ((), ())), preferred_element_type=_F32)
    o_ref[...] = x_ref[...] + mod_ref[5, pl.ds(r, 1), :] * moe


def _combine(x_new, y, gw, mod, latent, seq):
    n_tok = x_new.shape[0]
    n_sets = n_tok // seq
    cap = EC_CAPACITY_FACTOR * seq // N_EXPERTS
    seq_tiles = seq // TOKEN_TILE
    tile = pl.BlockSpec((TOKEN_TILE, D_MODEL), lambda s, t: (s * seq_tiles + t, 0))
    return pl.pallas_call(
        functools.partial(_combine_kernel, latent),
        grid=(n_sets, seq_tiles),
        in_specs=[
            tile,
            pl.BlockSpec((N_EXPERTS, cap, D_MODEL), lambda s, t: (0, s, 0)),
            pl.BlockSpec((None, N_EXPERTS * cap, TOKEN_TILE), lambda s, t: (s, 0, t)),
            _resident((N_MOD, COND_ROWS, D_MODEL)),
        ],
        out_specs=tile,
        out_shape=jax.ShapeDtypeStruct((n_tok, D_MODEL), _F32),
        compiler_params=_params(2),
        name="combine_lat" if latent else "combine_ctx",
    )(x_new, y, gw, mod)


def kernel(x_prompt, x_sample, cache_k, cache_v, c, c_ctx, w_ada, b_ada, g_norm1, w_in, g_q, g_k, conv_w,
           g_attn_out, g_conv_out, w_out, g_norm2, w_router, w_gate, w_up, w_down):
    batch, seq, _ = x_prompt.shape
    dec_batch, dec_seq, _ = x_sample.shape
    depth = w_ada.shape[0]
    assert 1 + dec_batch <= COND_ROWS

    cond = jnp.concatenate([c_ctx[None, :], c, jnp.zeros((COND_ROWS - 1 - dec_batch, D_MODEL), _F32)], axis=0)
    xp = x_prompt.reshape(batch * seq, D_MODEL)
    xl = x_sample.reshape(dec_batch * dec_seq, D_MODEL)
    past = cache_k.shape[2]

    new_k, new_v = [], []
    for l in range(depth):
        mod = _ada(cond, w_ada[l], b_ada[l])
        w_in_b = w_in[l].astype(_BF16)
        w_out_b = w_out[l].astype(_BF16)
        w_router_t = w_router[l].T

        q_p, k_p, v_p, bg_p, cu_p, kstate, vstate = _inproj(xp, mod, g_norm1[l], w_in_b, g_q[l], g_k[l],
                                                            False, seq)
        q_l, k_l, v_l, bg_l, cu_l = _inproj(xl, mod, g_norm1[l], w_in_b, g_q[l], g_k[l], True, dec_seq)

        shared = (mod, conv_w[l], g_attn_out[l], g_conv_out[l], w_out_b, g_norm2[l], w_router_t)
        kv_p = [(k_p.reshape(batch, seq, D_KV), v_p.reshape(batch, seq, D_KV))]
        kv_l = [(cache_k[:, l].reshape(dec_batch, past, D_KV), cache_v[:, l].reshape(dec_batch, past, D_KV)),
                (k_l.reshape(dec_batch, dec_seq, D_KV), v_l.reshape(dec_batch, dec_seq, D_KV))]
        xp1, h2_p, lg_p = _mix(xp, q_p, kv_p, bg_p, cu_p, *shared, False, seq)
        xl1, h2_l, lg_l = _mix(xl, q_l, kv_l, bg_l, cu_l, *shared, True, dec_seq)

        xs_p, gw_p = _route(lg_p, h2_p, seq, N_EXPERTS)
        xs_l, gw_l = _route(lg_l, h2_l, dec_seq, LATENT_ROUTE_EXPERTS)
        y_p, y_l = _ffn(xs_p, xs_l, w_gate[l], w_up[l], w_down[l])
        xp = _combine(xp1, y_p, gw_p, mod, False, seq)
        xl = _combine(xl1, y_l, gw_l, mod, True, dec_seq)

        new_k.append(kstate.reshape(batch, seq, N_KV_HEADS, HEAD_DIM))
        new_v.append(vstate.reshape(batch, seq, N_KV_HEADS, HEAD_DIM))

    return (xp.reshape(batch, seq, D_MODEL), xl.reshape(dec_batch, dec_seq, D_MODEL),
            jnp.stack(new_k, axis=1), jnp.stack(new_v, axis=1))
```

```python
import functools

import numpy as np
import jax
import jax.numpy as jnp
from jax import lax
from jax.experimental import pallas as pl
from jax.experimental.pallas import tpu as pltpu

D_MODEL = 2048
GRID_W = 64
HEAD_DIM = 128
D_ATTN = 1024
D_CONV = 1024
N_HEADS = 8
N_KV_HEADS = 2
GQA_GROUP = N_HEADS // N_KV_HEADS
D_KV = N_KV_HEADS * HEAD_DIM
N_EXPERTS = 16
EC_CAPACITY_FACTOR = 2
D_FF = 1024
ROPE_THETA = 10000.0
RMS_EPS = 1e-6
LOG2_E = 1.4426950408889634
N_MOD = 6
D_IN_PROJ = D_ATTN + 2 * D_KV + 3 * D_CONV

TOKEN_TILE = 256
ADA_COL_TILE = 1024
FF_TILE = 256
FFN_ROW_CHUNK = 512
COND_ROWS = 8
HALO_ROWS = 8
LATENT_ROUTE_EXPERTS = 4
VMEM_LIMIT = 52 * 1024 * 1024
LANES = 128
COMBINE_TOKENS = 512

_BF16 = jnp.bfloat16
_F32 = jnp.float32


def _params(n_axes):
    return pltpu.CompilerParams(dimension_semantics=("arbitrary",) * n_axes,
                                vmem_limit_bytes=VMEM_LIMIT)


def _resident(shape):
    return pl.BlockSpec(shape, lambda *_: (0,) * len(shape), pipeline_mode=pl.Buffered(1))


def _dot(a, b):
    return jnp.dot(a, b, preferred_element_type=_F32)


def _dot_nt(a, b):
    return lax.dot_general(a, b, (((1,), (1,)), ((), ())), preferred_element_type=_F32)


def _rms(x):
    return x * lax.rsqrt(jnp.mean(x * x, axis=-1, keepdims=True) + RMS_EPS)


def _split_bf16(x):
    hi = x.astype(_BF16)
    lo = (x - hi.astype(_F32)).astype(_BF16)
    return hi, lo


def _ada_kernel(c_ref, w_ref, b_ref, o_ref):
    c = c_ref[...]
    a = c * jax.nn.sigmoid(c)
    a_hi, a_lo = _split_bf16(a)
    w = w_ref[...]
    w_hi, w_lo = _split_bf16(w)
    o_ref[...] = _dot(a_hi, w_hi) + (_dot(a_lo, w_hi) + _dot(a_hi, w_lo)) + b_ref[...]


def _ada(cond, w_ada, b_ada):
    per_mod = D_MODEL // ADA_COL_TILE
    return pl.pallas_call(
        _ada_kernel,
        grid=(N_MOD, per_mod),
        in_specs=[
            pl.BlockSpec((COND_ROWS, D_MODEL), lambda j, h: (0, 0)),
            pl.BlockSpec((D_MODEL, ADA_COL_TILE), lambda j, h: (0, j * per_mod + h)),
            pl.BlockSpec((1, ADA_COL_TILE), lambda j, h: (0, j * per_mod + h)),
        ],
        out_specs=pl.BlockSpec((None, COND_ROWS, ADA_COL_TILE), lambda j, h: (j, 0, h)),
        out_shape=jax.ShapeDtypeStruct((N_MOD, COND_ROWS, D_MODEL), _F32),
        compiler_params=_params(2),
        name="ada",
    )(cond, w_ada, b_ada.reshape(1, N_MOD * D_MODEL))


def _rope_tables(seq):
    t = jnp.arange(seq)
    row = (t // GRID_W).astype(_F32)
    col = (t % GRID_W).astype(_F32)
    half = HEAD_DIM // 2
    inv_freq = ROPE_THETA ** (-jnp.arange(0, half, 2, dtype=_F32) / half)
    ang_r = row[:, None] * inv_freq[None, :]
    ang_c = col[:, None] * inv_freq[None, :]
    cos = jnp.concatenate([jnp.cos(ang_r)] * 2 + [jnp.cos(ang_c)] * 2, axis=-1)
    sin = jnp.concatenate([-jnp.sin(ang_r), jnp.sin(ang_r), -jnp.sin(ang_c), jnp.sin(ang_c)], axis=-1)
    return cos, sin


def _cond_row(latent, seq_tiles):
    if not latent:
        return 0
    return 1 + pl.program_id(0) // seq_tiles


def _inproj_kernel(latent, seq_tiles, x_ref, mod_ref, g1_ref, w_ref, gq_ref, gk_ref, *rest):
    if latent:
        cos_ref, sin_ref, q_ref, k_ref, v_ref, bg_ref, cu_ref = rest
    else:
        q_ref, k_ref, v_ref, bg_ref, cu_ref, kstate_ref, vstate_ref = rest
    r = _cond_row(latent, seq_tiles)
    sh = mod_ref[0, pl.ds(r, 1), :]
    sc = mod_ref[1, pl.ds(r, 1), :]
    h = _rms(x_ref[...]) * (g1_ref[...] * (1.0 + sc)) + sh
    hb = h.astype(_BF16)

    if latent:
        cos = cos_ref[...]
        sin = sin_ref[...]
        lane = lax.broadcasted_iota(jnp.int32, cos.shape, 1)
        first = (lane % (HEAD_DIM // 2)) < (HEAD_DIM // 4)

    def head_norm(xh, g):
        xh = _rms(xh) * g
        if latent:
            partner = jnp.where(first, pltpu.roll(xh, HEAD_DIM - HEAD_DIM // 4, axis=1),
                                pltpu.roll(xh, HEAD_DIM // 4, axis=1))
            xh = xh * cos + partner * sin
        return xh

    gq = gq_ref[...] * (HEAD_DIM ** -0.5 * LOG2_E)
    q = _dot(hb, w_ref[:, 0:D_ATTN])
    for hd in range(N_HEADS):
        cols = slice(hd * HEAD_DIM, (hd + 1) * HEAD_DIM)
        q_ref[:, cols] = head_norm(q[:, cols], gq).astype(q_ref.dtype)
    k = _dot(hb, w_ref[:, D_ATTN:D_ATTN + D_KV])
    o = D_ATTN + D_KV
    v = _dot(hb, w_ref[:, o:o + D_KV])
    for hd in range(N_KV_HEADS):
        cols = slice(hd * HEAD_DIM, (hd + 1) * HEAD_DIM)
        kh = head_norm(k[:, cols], gk_ref[...])
        k_ref[:, cols] = kh
        v_ref[:, cols] = v[:, cols]
        if not latent:
            kstate_ref[:, hd, :] = kh
            vstate_ref[:, hd, :] = v[:, cols]
    o += D_KV
    bg_ref[...] = _dot(hb, w_ref[:, o:o + D_CONV])
    o += D_CONV
    cg = _dot(hb, w_ref[:, o:o + D_CONV])
    o += D_CONV
    cu_ref[...] = cg * _dot(hb, w_ref[:, o:o + D_CONV])


def _inproj(x, mod, g_norm1, w_in_b, g_q, g_k, latent, seq):
    n_tok = x.shape[0]
    seq_tiles = seq // TOKEN_TILE
    tile = lambda w: pl.BlockSpec((TOKEN_TILE, w), lambda i: (i, 0))
    in_specs = [
        tile(D_MODEL),
        _resident((N_MOD, COND_ROWS, D_MODEL)),
        _resident((1, D_MODEL)),
        _resident((D_MODEL, D_IN_PROJ)),
        _resident((1, HEAD_DIM)),
        _resident((1, HEAD_DIM)),
    ]
    args = [x, mod, g_norm1.reshape(1, D_MODEL), w_in_b, g_q.reshape(1, HEAD_DIM), g_k.reshape(1, HEAD_DIM)]
    if latent:
        cos, sin = _rope_tables(seq)
        rope_spec = pl.BlockSpec((TOKEN_TILE, HEAD_DIM), lambda i: (i % seq_tiles, 0))
        in_specs += [rope_spec, rope_spec]
        args += [cos, sin]
    out_specs = [tile(D_ATTN), tile(D_KV), tile(D_KV), tile(D_CONV), tile(D_CONV)]
    out_shape = [
        jax.ShapeDtypeStruct((n_tok, D_ATTN), _BF16),
        jax.ShapeDtypeStruct((n_tok, D_KV), _F32),
        jax.ShapeDtypeStruct((n_tok, D_KV), _F32),
        jax.ShapeDtypeStruct((n_tok, D_CONV), _F32),
        jax.ShapeDtypeStruct((n_tok, D_CONV), _F32),
    ]
    if not latent:
        state_spec = pl.BlockSpec((TOKEN_TILE, N_KV_HEADS, HEAD_DIM), lambda i: (i, 0, 0))
        out_specs += [state_spec, state_spec]
        out_shape += [jax.ShapeDtypeStruct((n_tok, N_KV_HEADS, HEAD_DIM), _F32)] * 2
    return pl.pallas_call(
        functools.partial(_inproj_kernel, latent, seq_tiles),
        grid=(n_tok // TOKEN_TILE,),
        in_specs=in_specs,
        out_specs=out_specs,
        out_shape=out_shape,
        compiler_params=_params(1),
        name="inproj_lat" if latent else "inproj_ctx",
    )(*args)


def _mix_kernel(latent, seq_tiles, n_tiles, x_ref, q_ref, *rest):
    n_kv = 2 if latent else 1
    kv_refs = rest[:2 * n_kv]
    rest = rest[2 * n_kv:]
    bg_ref, cu_ref = rest[:2]
    rest = rest[2:]
    if latent:
        cu_prev_ref, cu_next_ref = rest[:2]
        rest = rest[2:]
    (convw_ref, ga_ref, gc_ref, wout_ref, mod_ref, g2_ref, wr_ref,
     xo_ref, h2_ref, lg_ref, mixed_scr) = rest

    s = pl.program_id(0)

    @pl.when(s == 0)
    def _():
        mixed_scr[...] = jnp.zeros_like(mixed_scr)

    r = 1 + jnp.maximum(s - 1, 0) // seq_tiles if latent else 0
    y = _dot(mixed_scr[...], wout_ref[...])
    x_new = x_ref[...] + mod_ref[2, pl.ds(r, 1), :] * y
    xo_ref[...] = x_new
    h2 = _rms(x_new) * (g2_ref[...] * (1.0 + mod_ref[4, pl.ds(r, 1), :])) + mod_ref[3, pl.ds(r, 1), :]
    h2_hi = h2.astype(_BF16)
    h2_ref[...] = h2_hi
    lg_ref[...] = _dot_nt(wr_ref[...].astype(_BF16), h2_hi)

    cu = cu_ref[...]
    if latent:
        pos = jnp.minimum(s, n_tiles - 1) % seq_tiles
        edge_prev = jnp.where(pos == 0, 0.0, 1.0) * cu_prev_ref[HALO_ROWS - 1:HALO_ROWS, :]
        edge_next = jnp.where(pos == seq_tiles - 1, 0.0, 1.0) * cu_next_ref[0:1, :]
    else:
        edge_prev = jnp.zeros((1, D_CONV), _F32)
        edge_next = edge_prev
    prev = pltpu.roll(cu, 1, axis=0)
    nxt = pltpu.roll(cu, TOKEN_TILE - 1, axis=0)
    sub = lax.broadcasted_iota(jnp.int32, (HALO_ROWS, D_CONV), 0)
    prev = jnp.concatenate([jnp.where(sub == 0, edge_prev, prev[0:HALO_ROWS]), prev[HALO_ROWS:]], axis=0)
    nxt = jnp.concatenate([nxt[:TOKEN_TILE - HALO_ROWS],
                           jnp.where(sub == HALO_ROWS - 1, edge_next, nxt[TOKEN_TILE - HALO_ROWS:])], axis=0)
    conv = bg_ref[...] * (prev * convw_ref[0:1, :] + cu * convw_ref[1:2, :] + nxt * convw_ref[2:3, :])
    conv_n = (_rms(conv) * gc_ref[...]).astype(_BF16)

    row_sum_on_mxu = latent
    heads = []
    for kvh in range(N_KV_HEADS):
        cols = slice(kvh * HEAD_DIM, (kvh + 1) * HEAD_DIM)
        ks = [kv_refs[2 * p][:, cols].astype(_BF16) for p in range(n_kv)]
        vs = [kv_refs[2 * p + 1][:, cols].astype(_BF16) for p in range(n_kv)]
        if row_sum_on_mxu:
            vs = [jnp.concatenate([vp, jnp.ones_like(vp)], axis=1) for vp in vs]
        for g in range(GQA_GROUP):
            hc = slice((kvh * GQA_GROUP + g) * HEAD_DIM, (kvh * GQA_GROUP + g + 1) * HEAD_DIM)
            qh = q_ref[:, hc]
            sc = [_dot_nt(qh, kp) for kp in ks]
            m = functools.reduce(jnp.maximum, [jnp.max(sp, axis=-1, keepdims=True) for sp in sc])
            if row_sum_on_mxu:
                ol = functools.reduce(jnp.add, [_dot(jnp.exp2(sp - m).astype(_BF16), vp) for sp, vp in zip(sc, vs)])
                heads.append(ol[:, 0:HEAD_DIM] / ol[:, HEAD_DIM:HEAD_DIM + 1])
            else:
                p = [jnp.exp2(sp - m) for sp in sc]
                l = functools.reduce(jnp.add, [jnp.sum(pp, axis=-1, keepdims=True) for pp in p])
                o = functools.reduce(jnp.add, [_dot(pp.astype(_BF16), vp) for pp, vp in zip(p, vs)])
                heads.append(o / l)
    ssq = functools.reduce(jnp.add, [jnp.sum(h * h, axis=-1, keepdims=True) for h in heads])
    inv = lax.rsqrt(ssq * (1.0 / D_ATTN) + RMS_EPS)
    for hd, h in enumerate(heads):
        hc = slice(hd * HEAD_DIM, (hd + 1) * HEAD_DIM)
        mixed_scr[:, hc] = (h * inv * ga_ref[:, hc]).astype(_BF16)
    mixed_scr[:, D_ATTN:D_ATTN + D_CONV] = conv_n


def _mix(x, q, kv_parts, bg, cu, mod, conv_w, g_attn_out, g_conv_out, w_out_b, g_norm2, w_router_t,
         latent, seq):
    n_tok = x.shape[0]
    seq_tiles = seq // TOKEN_TILE
    n_tiles = n_tok // TOKEN_TILE
    started = lambda i: jnp.minimum(i, n_tiles - 1)
    finished = lambda i: jnp.maximum(i - 1, 0)
    start_tile = lambda w: pl.BlockSpec((TOKEN_TILE, w), lambda i: (started(i), 0))
    finish_tile = lambda w: pl.BlockSpec((TOKEN_TILE, w), lambda i: (finished(i), 0))
    in_specs = [finish_tile(D_MODEL), start_tile(D_ATTN)]
    args = [x, q]
    for k, v in kv_parts:
        spec = pl.BlockSpec((None, k.shape[1], D_KV), lambda i: (started(i) // seq_tiles, 0, 0))
        in_specs += [spec, spec]
        args += [k, v]
    in_specs += [start_tile(D_CONV), start_tile(D_CONV)]
    args += [bg, cu]
    if latent:
        per_tile = TOKEN_TILE // HALO_ROWS
        last = n_tok // HALO_ROWS - 1
        in_specs += [
            pl.BlockSpec((HALO_ROWS, D_CONV), lambda i: (jnp.maximum(started(i) * per_tile - 1, 0), 0)),
            pl.BlockSpec((HALO_ROWS, D_CONV), lambda i: (jnp.minimum((started(i) + 1) * per_tile, last), 0)),
        ]
        args += [cu, cu]
    in_specs += [
        _resident((3, D_CONV)),
        _resident((1, D_ATTN)),
        _resident((1, D_CONV)),
        _resident((D_ATTN + D_CONV, D_MODEL)),
        _resident((N_MOD, COND_ROWS, D_MODEL)),
        _resident((1, D_MODEL)),
        _resident((N_EXPERTS, D_MODEL)),
    ]
    args += [conv_w, g_attn_out.reshape(1, D_ATTN), g_conv_out.reshape(1, D_CONV), w_out_b, mod,
             g_norm2.reshape(1, D_MODEL), w_router_t]
    return pl.pallas_call(
        functools.partial(_mix_kernel, latent, seq_tiles, n_tiles),
        grid=(n_tiles + 1,),
        in_specs=in_specs,
        out_specs=[finish_tile(D_MODEL), finish_tile(D_MODEL),
                   pl.BlockSpec((N_EXPERTS, TOKEN_TILE), lambda i: (0, finished(i)))],
        out_shape=[
            jax.ShapeDtypeStruct((n_tok, D_MODEL), _F32),
            jax.ShapeDtypeStruct((n_tok, D_MODEL), _BF16),
            jax.ShapeDtypeStruct((N_EXPERTS, n_tok), _F32),
        ],
        scratch_shapes=[pltpu.VMEM((TOKEN_TILE, D_ATTN + D_CONV), _BF16)],
        compiler_params=_params(1),
        name="mix_lat" if latent else "mix_ctx",
    )(*args)


def _route_kernel(seq, cap, e_blk, lg_ref, h2_ref, xs_ref, gw_ref, aff_scr, key_scr):
    j = pl.program_id(1)

    @pl.when(j == 0)
    def _():
        lg = lg_ref[...]
        ex = jnp.exp(lg - jnp.max(lg, axis=0, keepdims=True))
        aff = ex / jnp.sum(ex, axis=0, keepdims=True)
        aff_t = jnp.concatenate([aff, jnp.zeros((HEAD_DIM - N_EXPERTS, seq), _F32)], axis=0).T

        other = lax.broadcasted_iota(jnp.int32, (seq, seq), 0)
        token = lax.broadcasted_iota(jnp.int32, (seq, seq), 1)
        earlier = jnp.where(other < token, 1.0, 0.0)
        tie = earlier[0:LANES, 0:LANES]
        ranks = []
        for e in range(N_EXPERTS):
            theirs = aff_t[:, e:e + 1]
            blocks = []
            for lo in range(0, seq, LANES):
                hi = lo + LANES
                mine = aff[e:e + 1, lo:hi]
                diag = theirs[lo:hi]
                cnt = jnp.sum(jnp.where(diag > mine, 1.0, jnp.where(diag == mine, tie, 0.0)), axis=0, keepdims=True)
                if lo > 0:
                    cnt += jnp.sum(jnp.where(theirs[0:lo] >= mine, 1.0, 0.0), axis=0, keepdims=True)
                if hi < seq:
                    cnt += jnp.sum(jnp.where(theirs[hi:seq] > mine, 1.0, 0.0), axis=0, keepdims=True)
                blocks.append(cnt)
            ranks.append(jnp.concatenate(blocks, axis=1))
        rank = jnp.concatenate(ranks, axis=0)
        sel = rank < float(cap)
        slot = _dot(jnp.where(sel, 1.0, 0.0).astype(_BF16), earlier.astype(_BF16))
        aff_scr[...] = aff
        key_scr[...] = jnp.where(sel, slot, -1.0)

    c_idx = lax.broadcasted_iota(jnp.int32, (cap, seq), 0).astype(_F32)
    onehot, gated = [], []
    for i in range(e_blk):
        e = j * e_blk + i
        hit = c_idx == key_scr[pl.ds(e, 1), :]
        onehot.append(jnp.where(hit, 1.0, 0.0))
        gated.append(jnp.where(hit, aff_scr[pl.ds(e, 1), :], 0.0))
    xs = _dot(jnp.concatenate(onehot, axis=0).astype(_BF16), h2_ref[...]).astype(xs_ref.dtype)
    for i in range(e_blk):
        xs_ref[i] = xs[i * cap:(i + 1) * cap, :]
    gw_ref[...] = jnp.concatenate(gated, axis=0).astype(gw_ref.dtype)


def _route(logits_t, h2, seq, e_blk):
    n_tok = h2.shape[0]
    n_sets = n_tok // seq
    cap = EC_CAPACITY_FACTOR * seq // N_EXPERTS
    return pl.pallas_call(
        functools.partial(_route_kernel, seq, cap, e_blk),
        grid=(n_sets, N_EXPERTS // e_blk),
        in_specs=[
            pl.BlockSpec((N_EXPERTS, seq), lambda s, j: (0, s)),
            pl.BlockSpec((seq, D_MODEL), lambda s, j: (s, 0)),
        ],
        out_specs=[
            pl.BlockSpec((e_blk, cap, D_MODEL), lambda s, j: (j, s, 0)),
            pl.BlockSpec((None, e_blk * cap, seq), lambda s, j: (s, j, 0)),
        ],
        out_shape=[
            jax.ShapeDtypeStruct((N_EXPERTS, n_sets * cap, D_MODEL), _BF16),
            jax.ShapeDtypeStruct((n_sets, N_EXPERTS * cap, seq), _BF16),
        ],
        scratch_shapes=[pltpu.VMEM((N_EXPERTS, seq), _F32), pltpu.VMEM((N_EXPERTS, seq), _F32)],
        compiler_params=_params(2),
        name="route_%d" % seq,
    )(logits_t, h2)


def _ffn_kernel(xa_ref, xb_ref, wg_ref, wu_ref, wd_ref, ya_ref, yb_ref, acc_a, acc_b):
    def run(first):
        wg = wg_ref[...].astype(_BF16)
        wu = wu_ref[...].astype(_BF16)
        wd = wd_ref[...].astype(_BF16)
        for x_ref, y_ref, acc in ((xa_ref, ya_ref, acc_a), (xb_ref, yb_ref, acc_b)):
            chunk = min(FFN_ROW_CHUNK, x_ref.shape[0])
            for c in range(x_ref.shape[0] // chunk):
                rows = slice(c * chunk, (c + 1) * chunk)
                x = x_ref[rows, :]
                a = _dot(x, wg)
                u = _dot(x, wu)
                total = _dot((a * jax.nn.sigmoid(a) * u).astype(_BF16), wd)
                if not first:
                    total = acc[rows, :] + total
                acc[rows, :] = total
                y_ref[rows, :] = total.astype(y_ref.dtype)

    @pl.when(pl.program_id(1) == 0)
    def _():
        run(True)

    @pl.when(pl.program_id(1) > 0)
    def _():
        run(False)


def _ffn(xs_a, xs_b, w_gate, w_up, w_down):
    rows_a, rows_b = xs_a.shape[1], xs_b.shape[1]
    rows = lambda n: pl.BlockSpec((None, n, D_MODEL), lambda e, f: (e, 0, 0))
    return pl.pallas_call(
        _ffn_kernel,
        grid=(N_EXPERTS, D_FF // FF_TILE),
        in_specs=[
            rows(rows_a), rows(rows_b),
            pl.BlockSpec((None, D_MODEL, FF_TILE), lambda e, f: (e, 0, f)),
            pl.BlockSpec((None, D_MODEL, FF_TILE), lambda e, f: (e, 0, f)),
            pl.BlockSpec((None, FF_TILE, D_MODEL), lambda e, f: (e, f, 0)),
        ],
        out_specs=[rows(rows_a), rows(rows_b)],
        out_shape=[jax.ShapeDtypeStruct(xs_a.shape, _BF16), jax.ShapeDtypeStruct(xs_b.shape, _BF16)],
        scratch_shapes=[pltpu.VMEM((rows_a, D_MODEL), _F32), pltpu.VMEM((rows_b, D_MODEL), _F32)],
        compiler_params=_params(2),
        name="ffn",
    )(xs_a, xs_b, w_gate, w_up, w_down)


def _combine_kernel(latent, x_ref, y_ref, gw_ref, mod_ref, o_ref):
    r = 1 + pl.program_id(0) if latent else 0
    sets, rows, tile = gw_ref.shape
    cap = rows // N_EXPERTS
    gate = mod_ref[5, pl.ds(r, 1), :]
    for i in range(sets):
        y = y_ref[:, i * cap:(i + 1) * cap, :].reshape(rows, D_MODEL)
        moe = lax.dot_general(gw_ref[i], y, (((0,), (0,)), ((), ())), preferred_element_type=_F32)
        tok = slice(i * tile, (i + 1) * tile)
        o_ref[tok, :] = x_ref[tok, :] + gate * moe


def _combine(x_new, y, gw, mod, latent, seq):
    n_tok = x_new.shape[0]
    n_sets = n_tok // seq
    cap = EC_CAPACITY_FACTOR * seq // N_EXPERTS
    sets = max(1, COMBINE_TOKENS // seq)
    tile = COMBINE_TOKENS // sets
    per_set = seq // tile
    tokens = pl.BlockSpec((COMBINE_TOKENS, D_MODEL), lambda s, t: (s * per_set + t, 0))
    return pl.pallas_call(
        functools.partial(_combine_kernel, latent),
        grid=(n_sets // sets, per_set),
        in_specs=[
            tokens,
            pl.BlockSpec((N_EXPERTS, sets * cap, D_MODEL), lambda s, t: (0, s, 0)),
            pl.BlockSpec((sets, N_EXPERTS * cap, tile), lambda s, t: (s, 0, t)),
            _resident((N_MOD, COND_ROWS, D_MODEL)),
        ],
        out_specs=tokens,
        out_shape=jax.ShapeDtypeStruct((n_tok, D_MODEL), _F32),
        compiler_params=_params(2),
        name="combine_lat" if latent else "combine_ctx",
    )(x_new, y, gw, mod)


def kernel(x_prompt, x_sample, cache_k, cache_v, c, c_ctx, w_ada, b_ada, g_norm1, w_in, g_q, g_k, conv_w,
           g_attn_out, g_conv_out, w_out, g_norm2, w_router, w_gate, w_up, w_down):
    batch, seq, _ = x_prompt.shape
    dec_batch, dec_seq, _ = x_sample.shape
    depth = w_ada.shape[0]
    assert 1 + dec_batch <= COND_ROWS

    cond = jnp.concatenate([c_ctx[None, :], c, jnp.zeros((COND_ROWS - 1 - dec_batch, D_MODEL), _F32)], axis=0)
    xp = x_prompt.reshape(batch * seq, D_MODEL)
    xl = x_sample.reshape(dec_batch * dec_seq, D_MODEL)
    past = cache_k.shape[2]

    new_k, new_v = [], []
    for l in range(depth):
        mod = _ada(cond, w_ada[l], b_ada[l])
        w_in_b = w_in[l].astype(_BF16)
        w_out_b = w_out[l].astype(_BF16)
        w_router_t = w_router[l].T

        q_p, k_p, v_p, bg_p, cu_p, kstate, vstate = _inproj(xp, mod, g_norm1[l], w_in_b, g_q[l], g_k[l],
                                                            False, seq)
        q_l, k_l, v_l, bg_l, cu_l = _inproj(xl, mod, g_norm1[l], w_in_b, g_q[l], g_k[l], True, dec_seq)

        shared = (mod, conv_w[l], g_attn_out[l], g_conv_out[l], w_out_b, g_norm2[l], w_router_t)
        kv_p = [(k_p.reshape(batch, seq, D_KV), v_p.reshape(batch, seq, D_KV))]
        kv_l = [(cache_k[:, l].reshape(dec_batch, past, D_KV), cache_v[:, l].reshape(dec_batch, past, D_KV)),
                (k_l.reshape(dec_batch, dec_seq, D_KV), v_l.reshape(dec_batch, dec_seq, D_KV))]
        xp1, h2_p, lg_p = _mix(xp, q_p, kv_p, bg_p, cu_p, *shared, False, seq)
        xl1, h2_l, lg_l = _mix(xl, q_l, kv_l, bg_l, cu_l, *shared, True, dec_seq)

        xs_p, gw_p = _route(lg_p, h2_p, seq, N_EXPERTS)
        xs_l, gw_l = _route(lg_l, h2_l, dec_seq, LATENT_ROUTE_EXPERTS)
        y_p, y_l = _ffn(xs_p, xs_l, w_gate[l], w_up[l], w_down[l])
        xp = _combine(xp1, y_p, gw_p, mod, False, seq)
        xl = _combine(xl1, y_l, gw_l, mod, True, dec_seq)

        new_k.append(kstate.reshape(batch, seq, N_KV_HEADS, HEAD_DIM))
        new_v.append(vstate.reshape(batch, seq, N_KV_HEADS, HEAD_DIM))

    return (xp.reshape(batch, seq, D_MODEL), xl.reshape(dec_batch, dec_seq, D_MODEL),
            jnp.stack(new_k, axis=1), jnp.stack(new_v, axis=1))
```

```python
import functools

import numpy as np
import jax
import jax.numpy as jnp
from jax import lax
from jax.experimental import pallas as pl
from jax.experimental.pallas import tpu as pltpu

D_MODEL = 2048
GRID_W = 64
HEAD_DIM = 128
D_ATTN = 1024
D_CONV = 1024
N_HEADS = 8
N_KV_HEADS = 2
GQA_GROUP = N_HEADS // N_KV_HEADS
D_KV = N_KV_HEADS * HEAD_DIM
N_EXPERTS = 16
EC_CAPACITY_FACTOR = 2
D_FF = 1024
ROPE_THETA = 10000.0
RMS_EPS = 1e-6
LOG2_E = 1.4426950408889634
N_MOD = 6
D_IN_PROJ = D_ATTN + 2 * D_KV + 3 * D_CONV

TOKEN_TILE = 256
ADA_COL_TILE = 1024
FF_TILE = 256
FFN_ROW_CHUNK = 512
COND_ROWS = 8
HALO_ROWS = 8
LATENT_ROUTE_EXPERTS = 4
CONTEXT_ROUTE_SETS = 4
VMEM_LIMIT = 52 * 1024 * 1024
LANES = 128
COMBINE_TOKENS = 512

_BF16 = jnp.bfloat16
_F32 = jnp.float32


def _params(n_axes):
    return pltpu.CompilerParams(dimension_semantics=("arbitrary",) * n_axes,
                                vmem_limit_bytes=VMEM_LIMIT)


def _resident(shape):
    return pl.BlockSpec(shape, lambda *_: (0,) * len(shape), pipeline_mode=pl.Buffered(1))


def _dot(a, b):
    return jnp.dot(a, b, preferred_element_type=_F32)


def _dot_nt(a, b):
    return lax.dot_general(a, b, (((1,), (1,)), ((), ())), preferred_element_type=_F32)


def _rms(x):
    return x * lax.rsqrt(jnp.mean(x * x, axis=-1, keepdims=True) + RMS_EPS)


def _split_bf16(x):
    hi = x.astype(_BF16)
    lo = (x - hi.astype(_F32)).astype(_BF16)
    return hi, lo


def _ada_kernel(c_ref, w_ref, b_ref, o_ref):
    c = c_ref[...]
    a = c * jax.nn.sigmoid(c)
    a_hi, a_lo = _split_bf16(a)
    w = w_ref[...]
    w_hi, w_lo = _split_bf16(w)
    o_ref[...] = _dot(a_hi, w_hi) + (_dot(a_lo, w_hi) + _dot(a_hi, w_lo)) + b_ref[...]


def _ada(cond, w_ada, b_ada):
    per_mod = D_MODEL // ADA_COL_TILE
    return pl.pallas_call(
        _ada_kernel,
        grid=(N_MOD, per_mod),
        in_specs=[
            pl.BlockSpec((COND_ROWS, D_MODEL), lambda j, h: (0, 0)),
            pl.BlockSpec((D_MODEL, ADA_COL_TILE), lambda j, h: (0, j * per_mod + h)),
            pl.BlockSpec((1, ADA_COL_TILE), lambda j, h: (0, j * per_mod + h)),
        ],
        out_specs=pl.BlockSpec((None, COND_ROWS, ADA_COL_TILE), lambda j, h: (j, 0, h)),
        out_shape=jax.ShapeDtypeStruct((N_MOD, COND_ROWS, D_MODEL), _F32),
        compiler_params=_params(2),
        name="ada",
    )(cond, w_ada, b_ada.reshape(1, N_MOD * D_MODEL))


def _rope_tables(seq):
    t = jnp.arange(seq)
    row = (t // GRID_W).astype(_F32)
    col = (t % GRID_W).astype(_F32)
    half = HEAD_DIM // 2
    inv_freq = ROPE_THETA ** (-jnp.arange(0, half, 2, dtype=_F32) / half)
    ang_r = row[:, None] * inv_freq[None, :]
    ang_c = col[:, None] * inv_freq[None, :]
    cos = jnp.concatenate([jnp.cos(ang_r)] * 2 + [jnp.cos(ang_c)] * 2, axis=-1)
    sin = jnp.concatenate([-jnp.sin(ang_r), jnp.sin(ang_r), -jnp.sin(ang_c), jnp.sin(ang_c)], axis=-1)
    return cos, sin


def _cond_row(latent, seq_tiles):
    if not latent:
        return 0
    return 1 + pl.program_id(0) // seq_tiles


def _inproj_kernel(latent, seq_tiles, x_ref, mod_ref, g1_ref, w_ref, gq_ref, gk_ref, *rest):
    if latent:
        cos_ref, sin_ref, q_ref, k_ref, v_ref, bg_ref, cu_ref = rest
    else:
        q_ref, k_ref, v_ref, bg_ref, cu_ref, kstate_ref, vstate_ref = rest
    r = _cond_row(latent, seq_tiles)
    sh = mod_ref[0, pl.ds(r, 1), :]
    sc = mod_ref[1, pl.ds(r, 1), :]
    h = _rms(x_ref[...]) * (g1_ref[...] * (1.0 + sc)) + sh
    hb = h.astype(_BF16)

    if latent:
        cos = cos_ref[...]
        sin = sin_ref[...]
        lane = lax.broadcasted_iota(jnp.int32, cos.shape, 1)
        first = (lane % (HEAD_DIM // 2)) < (HEAD_DIM // 4)

    def head_norm(xh, g):
        xh = _rms(xh) * g
        if latent:
            partner = jnp.where(first, pltpu.roll(xh, HEAD_DIM - HEAD_DIM // 4, axis=1),
                                pltpu.roll(xh, HEAD_DIM // 4, axis=1))
            xh = xh * cos + partner * sin
        return xh

    gq = gq_ref[...] * (HEAD_DIM ** -0.5 * LOG2_E)
    q = _dot(hb, w_ref[:, 0:D_ATTN])
    for hd in range(N_HEADS):
        cols = slice(hd * HEAD_DIM, (hd + 1) * HEAD_DIM)
        q_ref[:, cols] = head_norm(q[:, cols], gq).astype(q_ref.dtype)
    k = _dot(hb, w_ref[:, D_ATTN:D_ATTN + D_KV])
    o = D_ATTN + D_KV
    v = _dot(hb, w_ref[:, o:o + D_KV])
    for hd in range(N_KV_HEADS):
        cols = slice(hd * HEAD_DIM, (hd + 1) * HEAD_DIM)
        kh = head_norm(k[:, cols], gk_ref[...])
        k_ref[:, cols] = kh
        v_ref[:, cols] = v[:, cols]
        if not latent:
            kstate_ref[:, hd, :] = kh
            vstate_ref[:, hd, :] = v[:, cols]
    o += D_KV
    bg_ref[...] = _dot(hb, w_ref[:, o:o + D_CONV])
    o += D_CONV
    cg = _dot(hb, w_ref[:, o:o + D_CONV])
    o += D_CONV
    cu_ref[...] = cg * _dot(hb, w_ref[:, o:o + D_CONV])


def _inproj(x, mod, g_norm1, w_in_b, g_q, g_k, latent, seq):
    n_tok = x.shape[0]
    seq_tiles = seq // TOKEN_TILE
    tile = lambda w: pl.BlockSpec((TOKEN_TILE, w), lambda i: (i, 0))
    in_specs = [
        tile(D_MODEL),
        _resident((N_MOD, COND_ROWS, D_MODEL)),
        _resident((1, D_MODEL)),
        _resident((D_MODEL, D_IN_PROJ)),
        _resident((1, HEAD_DIM)),
        _resident((1, HEAD_DIM)),
    ]
    args = [x, mod, g_norm1.reshape(1, D_MODEL), w_in_b, g_q.reshape(1, HEAD_DIM), g_k.reshape(1, HEAD_DIM)]
    if latent:
        cos, sin = _rope_tables(seq)
        rope_spec = pl.BlockSpec((TOKEN_TILE, HEAD_DIM), lambda i: (i % seq_tiles, 0))
        in_specs += [rope_spec, rope_spec]
        args += [cos, sin]
    out_specs = [tile(D_ATTN), tile(D_KV), tile(D_KV), tile(D_CONV), tile(D_CONV)]
    out_shape = [
        jax.ShapeDtypeStruct((n_tok, D_ATTN), _BF16),
        jax.ShapeDtypeStruct((n_tok, D_KV), _F32),
        jax.ShapeDtypeStruct((n_tok, D_KV), _F32),
        jax.ShapeDtypeStruct((n_tok, D_CONV), _F32),
        jax.ShapeDtypeStruct((n_tok, D_CONV), _F32),
    ]
    if not latent:
        state_spec = pl.BlockSpec((TOKEN_TILE, N_KV_HEADS, HEAD_DIM), lambda i: (i, 0, 0))
        out_specs += [state_spec, state_spec]
        out_shape += [jax.ShapeDtypeStruct((n_tok, N_KV_HEADS, HEAD_DIM), _F32)] * 2
    return pl.pallas_call(
        functools.partial(_inproj_kernel, latent, seq_tiles),
        grid=(n_tok // TOKEN_TILE,),
        in_specs=in_specs,
        out_specs=out_specs,
        out_shape=out_shape,
        compiler_params=_params(1),
        name="inproj_lat" if latent else "inproj_ctx",
    )(*args)


def _mix_kernel(latent, seq_tiles, n_tiles, x_ref, q_ref, *rest):
    n_kv = 2 if latent else 1
    kv_refs = rest[:2 * n_kv]
    rest = rest[2 * n_kv:]
    bg_ref, cu_ref = rest[:2]
    rest = rest[2:]
    if latent:
        cu_prev_ref, cu_next_ref = rest[:2]
        rest = rest[2:]
    (convw_ref, ga_ref, gc_ref, wout_ref, mod_ref, g2_ref, wr_ref,
     xo_ref, h2_ref, lg_ref, mixed_scr) = rest

    s = pl.program_id(0)

    @pl.when(s == 0)
    def _():
        mixed_scr[...] = jnp.zeros_like(mixed_scr)

    r = 1 + jnp.maximum(s - 1, 0) // seq_tiles if latent else 0
    y = _dot(mixed_scr[...], wout_ref[...])
    x_new = x_ref[...] + mod_ref[2, pl.ds(r, 1), :] * y
    xo_ref[...] = x_new
    h2 = _rms(x_new) * (g2_ref[...] * (1.0 + mod_ref[4, pl.ds(r, 1), :])) + mod_ref[3, pl.ds(r, 1), :]
    h2_hi = h2.astype(_BF16)
    h2_ref[...] = h2_hi
    lg_ref[...] = _dot_nt(wr_ref[...].astype(_BF16), h2_hi)

    cu = cu_ref[...]
    if latent:
        pos = jnp.minimum(s, n_tiles - 1) % seq_tiles
        edge_prev = jnp.where(pos == 0, 0.0, 1.0) * cu_prev_ref[HALO_ROWS - 1:HALO_ROWS, :]
        edge_next = jnp.where(pos == seq_tiles - 1, 0.0, 1.0) * cu_next_ref[0:1, :]
    else:
        edge_prev = jnp.zeros((1, D_CONV), _F32)
        edge_next = edge_prev
    prev = pltpu.roll(cu, 1, axis=0)
    nxt = pltpu.roll(cu, TOKEN_TILE - 1, axis=0)
    sub = lax.broadcasted_iota(jnp.int32, (HALO_ROWS, D_CONV), 0)
    prev = jnp.concatenate([jnp.where(sub == 0, edge_prev, prev[0:HALO_ROWS]), prev[HALO_ROWS:]], axis=0)
    nxt = jnp.concatenate([nxt[:TOKEN_TILE - HALO_ROWS],
                           jnp.where(sub == HALO_ROWS - 1, edge_next, nxt[TOKEN_TILE - HALO_ROWS:])], axis=0)
    conv = bg_ref[...] * (prev * convw_ref[0:1, :] + cu * convw_ref[1:2, :] + nxt * convw_ref[2:3, :])
    conv_n = (_rms(conv) * gc_ref[...]).astype(_BF16)

    row_sum_on_mxu = latent
    heads = []
    for kvh in range(N_KV_HEADS):
        cols = slice(kvh * HEAD_DIM, (kvh + 1) * HEAD_DIM)
        ks = [kv_refs[2 * p][:, cols].astype(_BF16) for p in range(n_kv)]
        vs = [kv_refs[2 * p + 1][:, cols].astype(_BF16) for p in range(n_kv)]
        if row_sum_on_mxu:
            vs = [jnp.concatenate([vp, jnp.ones_like(vp)], axis=1) for vp in vs]
        for g in range(GQA_GROUP):
            hc = slice((kvh * GQA_GROUP + g) * HEAD_DIM, (kvh * GQA_GROUP + g + 1) * HEAD_DIM)
            qh = q_ref[:, hc]
            sc = [_dot_nt(qh, kp) for kp in ks]
            m = functools.reduce(jnp.maximum, [jnp.max(sp, axis=-1, keepdims=True) for sp in sc])
            if row_sum_on_mxu:
                ol = functools.reduce(jnp.add, [_dot(jnp.exp2(sp - m).astype(_BF16), vp) for sp, vp in zip(sc, vs)])
                heads.append(ol[:, 0:HEAD_DIM] / ol[:, HEAD_DIM:HEAD_DIM + 1])
            else:
                p = [jnp.exp2(sp - m) for sp in sc]
                l = functools.reduce(jnp.add, [jnp.sum(pp, axis=-1, keepdims=True) for pp in p])
                o = functools.reduce(jnp.add, [_dot(pp.astype(_BF16), vp) for pp, vp in zip(p, vs)])
                heads.append(o / l)
    ssq = functools.reduce(jnp.add, [jnp.sum(h * h, axis=-1, keepdims=True) for h in heads])
    inv = lax.rsqrt(ssq * (1.0 / D_ATTN) + RMS_EPS)
    for hd, h in enumerate(heads):
        hc = slice(hd * HEAD_DIM, (hd + 1) * HEAD_DIM)
        mixed_scr[:, hc] = (h * inv * ga_ref[:, hc]).astype(_BF16)
    mixed_scr[:, D_ATTN:D_ATTN + D_CONV] = conv_n


def _mix(x, q, kv_parts, bg, cu, mod, conv_w, g_attn_out, g_conv_out, w_out_b, g_norm2, w_router_t,
         latent, seq):
    n_tok = x.shape[0]
    seq_tiles = seq // TOKEN_TILE
    n_tiles = n_tok // TOKEN_TILE
    started = lambda i: jnp.minimum(i, n_tiles - 1)
    finished = lambda i: jnp.maximum(i - 1, 0)
    start_tile = lambda w: pl.BlockSpec((TOKEN_TILE, w), lambda i: (started(i), 0))
    finish_tile = lambda w: pl.BlockSpec((TOKEN_TILE, w), lambda i: (finished(i), 0))
    in_specs = [finish_tile(D_MODEL), start_tile(D_ATTN)]
    args = [x, q]
    for k, v in kv_parts:
        spec = pl.BlockSpec((None, k.shape[1], D_KV), lambda i: (started(i) // seq_tiles, 0, 0))
        in_specs += [spec, spec]
        args += [k, v]
    in_specs += [start_tile(D_CONV), start_tile(D_CONV)]
    args += [bg, cu]
    if latent:
        per_tile = TOKEN_TILE // HALO_ROWS
        last = n_tok // HALO_ROWS - 1
        in_specs += [
            pl.BlockSpec((HALO_ROWS, D_CONV), lambda i: (jnp.maximum(started(i) * per_tile - 1, 0), 0)),
            pl.BlockSpec((HALO_ROWS, D_CONV), lambda i: (jnp.minimum((started(i) + 1) * per_tile, last), 0)),
        ]
        args += [cu, cu]
    in_specs += [
        _resident((3, D_CONV)),
        _resident((1, D_ATTN)),
        _resident((1, D_CONV)),
        _resident((D_ATTN + D_CONV, D_MODEL)),
        _resident((N_MOD, COND_ROWS, D_MODEL)),
        _resident((1, D_MODEL)),
        _resident((N_EXPERTS, D_MODEL)),
    ]
    args += [conv_w, g_attn_out.reshape(1, D_ATTN), g_conv_out.reshape(1, D_CONV), w_out_b, mod,
             g_norm2.reshape(1, D_MODEL), w_router_t]
    return pl.pallas_call(
        functools.partial(_mix_kernel, latent, seq_tiles, n_tiles),
        grid=(n_tiles + 1,),
        in_specs=in_specs,
        out_specs=[finish_tile(D_MODEL), finish_tile(D_MODEL),
                   pl.BlockSpec((N_EXPERTS, TOKEN_TILE), lambda i: (0, finished(i)))],
        out_shape=[
            jax.ShapeDtypeStruct((n_tok, D_MODEL), _F32),
            jax.ShapeDtypeStruct((n_tok, D_MODEL), _BF16),
            jax.ShapeDtypeStruct((N_EXPERTS, n_tok), _F32),
        ],
        scratch_shapes=[pltpu.VMEM((TOKEN_TILE, D_ATTN + D_CONV), _BF16)],
        compiler_params=_params(1),
        name="mix_lat" if latent else "mix_ctx",
    )(*args)


def _route_kernel(seq, cap, sets, e_blk, lg_ref, h2_ref, xs_ref, gw_ref, aff_scr, key_scr):
    j = pl.program_id(1)

    @pl.when(j == 0)
    def _():
        other = lax.broadcasted_iota(jnp.int32, (seq, seq), 0)
        token = lax.broadcasted_iota(jnp.int32, (seq, seq), 1)
        earlier = jnp.where(other < token, 1.0, 0.0)
        tie = earlier[0:LANES, 0:LANES]
        before = earlier.astype(_BF16)
        for s in range(sets):
            lg = lg_ref[:, s * seq:(s + 1) * seq]
            ex = jnp.exp(lg - jnp.max(lg, axis=0, keepdims=True))
            aff = ex / jnp.sum(ex, axis=0, keepdims=True)
            aff_t = jnp.concatenate([aff, jnp.zeros((LANES - N_EXPERTS, seq), _F32)], axis=0).T
            ranks = []
            for e in range(N_EXPERTS):
                theirs = aff_t[:, e:e + 1]
                blocks = []
                for lo in range(0, seq, LANES):
                    hi = lo + LANES
                    mine = aff[e:e + 1, lo:hi]
                    diag = theirs[lo:hi]
                    cnt = jnp.sum(jnp.where(diag > mine, 1.0, jnp.where(diag == mine, tie, 0.0)),
                                  axis=0, keepdims=True)
                    if lo > 0:
                        cnt += jnp.sum(jnp.where(theirs[0:lo] >= mine, 1.0, 0.0), axis=0, keepdims=True)
                    if hi < seq:
                        cnt += jnp.sum(jnp.where(theirs[hi:seq] > mine, 1.0, 0.0), axis=0, keepdims=True)
                    blocks.append(cnt)
                ranks.append(jnp.concatenate(blocks, axis=1))
            rank = jnp.concatenate(ranks, axis=0)
            sel = rank < float(cap)
            slot = _dot(jnp.where(sel, 1.0, 0.0).astype(_BF16), before)
            rows = slice(s * N_EXPERTS, (s + 1) * N_EXPERTS)
            aff_scr[rows, :] = aff
            key_scr[rows, :] = jnp.where(sel, slot, -1.0)

    c_idx = lax.broadcasted_iota(jnp.int32, (cap, seq), 0).astype(_F32)
    for s in range(sets):
        onehot, gated = [], []
        for i in range(e_blk):
            e = s * N_EXPERTS + j * e_blk + i
            hit = c_idx == key_scr[pl.ds(e, 1), :]
            onehot.append(jnp.where(hit, 1.0, 0.0))
            gated.append(jnp.where(hit, aff_scr[pl.ds(e, 1), :], 0.0))
        xs = _dot(jnp.concatenate(onehot, axis=0).astype(_BF16), h2_ref[s * seq:(s + 1) * seq, :]).astype(xs_ref.dtype)
        for i in range(e_blk):
            xs_ref[i, s * cap:(s + 1) * cap, :] = xs[i * cap:(i + 1) * cap, :]
        gw_ref[s] = jnp.concatenate(gated, axis=0).astype(gw_ref.dtype)


def _route(logits_t, h2, seq, sets, e_blk):
    n_tok = h2.shape[0]
    n_sets = n_tok // seq
    cap = EC_CAPACITY_FACTOR * seq // N_EXPERTS
    return pl.pallas_call(
        functools.partial(_route_kernel, seq, cap, sets, e_blk),
        grid=(n_sets // sets, N_EXPERTS // e_blk),
        in_specs=[
            pl.BlockSpec((N_EXPERTS, sets * seq), lambda s, j: (0, s)),
            pl.BlockSpec((sets * seq, D_MODEL), lambda s, j: (s, 0)),
        ],
        out_specs=[
            pl.BlockSpec((e_blk, sets * cap, D_MODEL), lambda s, j: (j, s, 0)),
            pl.BlockSpec((sets, e_blk * cap, seq), lambda s, j: (s, j, 0)),
        ],
        out_shape=[
            jax.ShapeDtypeStruct((N_EXPERTS, n_sets * cap, D_MODEL), _BF16),
            jax.ShapeDtypeStruct((n_sets, N_EXPERTS * cap, seq), _BF16),
        ],
        scratch_shapes=[pltpu.VMEM((sets * N_EXPERTS, seq), _F32), pltpu.VMEM((sets * N_EXPERTS, seq), _F32)],
        compiler_params=_params(2),
        name="route_%d" % seq,
    )(logits_t, h2)


def _ffn_kernel(xa_ref, xb_ref, wg_ref, wu_ref, wd_ref, ya_ref, yb_ref, acc_a, acc_b):
    def run(first):
        wg = wg_ref[...].astype(_BF16)
        wu = wu_ref[...].astype(_BF16)
        wd = wd_ref[...].astype(_BF16)
        for x_ref, y_ref, acc in ((xa_ref, ya_ref, acc_a), (xb_ref, yb_ref, acc_b)):
            chunk = min(FFN_ROW_CHUNK, x_ref.shape[0])
            for c in range(x_ref.shape[0] // chunk):
                rows = slice(c * chunk, (c + 1) * chunk)
                x = x_ref[rows, :]
                a = _dot(x, wg)
                u = _dot(x, wu)
                total = _dot((a * jax.nn.sigmoid(a) * u).astype(_BF16), wd)
                if not first:
                    total = acc[rows, :] + total
                acc[rows, :] = total
                y_ref[rows, :] = total.astype(y_ref.dtype)

    @pl.when(pl.program_id(1) == 0)
    def _():
        run(True)

    @pl.when(pl.program_id(1) > 0)
    def _():
        run(False)


def _ffn(xs_a, xs_b, w_gate, w_up, w_down):
    rows_a, rows_b = xs_a.shape[1], xs_b.shape[1]
    rows = lambda n: pl.BlockSpec((None, n, D_MODEL), lambda e, f: (e, 0, 0))
    return pl.pallas_call(
        _ffn_kernel,
        grid=(N_EXPERTS, D_FF // FF_TILE),
        in_specs=[
            rows(rows_a), rows(rows_b),
            pl.BlockSpec((None, D_MODEL, FF_TILE), lambda e, f: (e, 0, f)),
            pl.BlockSpec((None, D_MODEL, FF_TILE), lambda e, f: (e, 0, f)),
            pl.BlockSpec((None, FF_TILE, D_MODEL), lambda e, f: (e, f, 0)),
        ],
        out_specs=[rows(rows_a), rows(rows_b)],
        out_shape=[jax.ShapeDtypeStruct(xs_a.shape, _BF16), jax.ShapeDtypeStruct(xs_b.shape, _BF16)],
        scratch_shapes=[pltpu.VMEM((rows_a, D_MODEL), _F32), pltpu.VMEM((rows_b, D_MODEL), _F32)],
        compiler_params=_params(2),
        name="ffn",
    )(xs_a, xs_b, w_gate, w_up, w_down)


def _combine_kernel(latent, x_ref, y_ref, gw_ref, mod_ref, o_ref):
    r = 1 + pl.program_id(0) if latent else 0
    sets, rows, tile = gw_ref.shape
    cap = rows // N_EXPERTS
    gate = mod_ref[5, pl.ds(r, 1), :]
    for i in range(sets):
        y = y_ref[:, i * cap:(i + 1) * cap, :].reshape(rows, D_MODEL)
        moe = lax.dot_general(gw_ref[i], y, (((0,), (0,)), ((), ())), preferred_element_type=_F32)
        tok = slice(i * tile, (i + 1) * tile)
        o_ref[tok, :] = x_ref[tok, :] + gate * moe


def _combine(x_new, y, gw, mod, latent, seq):
    n_tok = x_new.shape[0]
    n_sets = n_tok // seq
    cap = EC_CAPACITY_FACTOR * seq // N_EXPERTS
    sets = max(1, COMBINE_TOKENS // seq)
    tile = COMBINE_TOKENS // sets
    per_set = seq // tile
    tokens = pl.BlockSpec((COMBINE_TOKENS, D_MODEL), lambda s, t: (s * per_set + t, 0))
    return pl.pallas_call(
        functools.partial(_combine_kernel, latent),
        grid=(n_sets // sets, per_set),
        in_specs=[
            tokens,
            pl.BlockSpec((N_EXPERTS, sets * cap, D_MODEL), lambda s, t: (0, s, 0)),
            pl.BlockSpec((sets, N_EXPERTS * cap, tile), lambda s, t: (s, 0, t)),
            _resident((N_MOD, COND_ROWS, D_MODEL)),
        ],
        out_specs=tokens,
        out_shape=jax.ShapeDtypeStruct((n_tok, D_MODEL), _F32),
        compiler_params=_params(2),
        name="combine_lat" if latent else "combine_ctx",
    )(x_new, y, gw, mod)


def kernel(x_prompt, x_sample, cache_k, cache_v, c, c_ctx, w_ada, b_ada, g_norm1, w_in, g_q, g_k, conv_w,
           g_attn_out, g_conv_out, w_out, g_norm2, w_router, w_gate, w_up, w_down):
    batch, seq, _ = x_prompt.shape
    dec_batch, dec_seq, _ = x_sample.shape
    depth = w_ada.shape[0]
    assert 1 + dec_batch <= COND_ROWS

    cond = jnp.concatenate([c_ctx[None, :], c, jnp.zeros((COND_ROWS - 1 - dec_batch, D_MODEL), _F32)], axis=0)
    xp = x_prompt.reshape(batch * seq, D_MODEL)
    xl = x_sample.reshape(dec_batch * dec_seq, D_MODEL)
    past = cache_k.shape[2]

    new_k, new_v = [], []
    for l in range(depth):
        mod = _ada(cond, w_ada[l], b_ada[l])
        w_in_b = w_in[l].astype(_BF16)
        w_out_b = w_out[l].astype(_BF16)
        w_router_t = w_router[l].T

        q_p, k_p, v_p, bg_p, cu_p, kstate, vstate = _inproj(xp, mod, g_norm1[l], w_in_b, g_q[l], g_k[l],
                                                            False, seq)
        q_l, k_l, v_l, bg_l, cu_l = _inproj(xl, mod, g_norm1[l], w_in_b, g_q[l], g_k[l], True, dec_seq)

        shared = (mod, conv_w[l], g_attn_out[l], g_conv_out[l], w_out_b, g_norm2[l], w_router_t)
        kv_p = [(k_p.reshape(batch, seq, D_KV), v_p.reshape(batch, seq, D_KV))]
        kv_l = [(cache_k[:, l].reshape(dec_batch, past, D_KV), cache_v[:, l].reshape(dec_batch, past, D_KV)),
                (k_l.reshape(dec_batch, dec_seq, D_KV), v_l.reshape(dec_batch, dec_seq, D_KV))]
        xp1, h2_p, lg_p = _mix(xp, q_p, kv_p, bg_p, cu_p, *shared, False, seq)
        xl1, h2_l, lg_l = _mix(xl, q_l, kv_l, bg_l, cu_l, *shared, True, dec_seq)

        xs_p, gw_p = _route(lg_p, h2_p, seq, CONTEXT_ROUTE_SETS, N_EXPERTS)
        xs_l, gw_l = _route(lg_l, h2_l, dec_seq, 1, LATENT_ROUTE_EXPERTS)
        y_p, y_l = _ffn(xs_p, xs_l, w_gate[l], w_up[l], w_down[l])
        xp = _combine(xp1, y_p, gw_p, mod, False, seq)
        xl = _combine(xl1, y_l, gw_l, mod, True, dec_seq)

        new_k.append(kstate.reshape(batch, seq, N_KV_HEADS, HEAD_DIM))
        new_v.append(vstate.reshape(batch, seq, N_KV_HEADS, HEAD_DIM))

    return (xp.reshape(batch, seq, D_MODEL), xl.reshape(dec_batch, dec_seq, D_MODEL),
            jnp.stack(new_k, axis=1), jnp.stack(new_v, axis=1))
```

```python
import functools

import numpy as np
import jax
import jax.numpy as jnp
from jax import lax
from jax.experimental import pallas as pl
from jax.experimental.pallas import tpu as pltpu

D_MODEL = 2048
GRID_W = 64
HEAD_DIM = 128
D_ATTN = 1024
D_CONV = 1024
N_HEADS = 8
N_KV_HEADS = 2
GQA_GROUP = N_HEADS // N_KV_HEADS
D_KV = N_KV_HEADS * HEAD_DIM
N_EXPERTS = 16
EC_CAPACITY_FACTOR = 2
D_FF = 1024
ROPE_THETA = 10000.0
RMS_EPS = 1e-6
LOG2_E = 1.4426950408889634
N_MOD = 6
D_IN_PROJ = D_ATTN + 2 * D_KV + 3 * D_CONV

TOKEN_TILE = 256
INPROJ_TILE = 512
ADA_COL_TILE = 1024
FF_TILE = 256
FFN_ROW_CHUNK = 512
COND_ROWS = 8
HALO_ROWS = 8
LATENT_ROUTE_EXPERTS = 4
CONTEXT_ROUTE_SETS = 4
CONTEXT_HEADS_STACKED = 2
VMEM_LIMIT = 52 * 1024 * 1024
LANES = 128
COMBINE_TOKENS = 512

_BF16 = jnp.bfloat16
_F32 = jnp.float32


def _params(n_axes):
    return pltpu.CompilerParams(dimension_semantics=("arbitrary",) * n_axes,
                                vmem_limit_bytes=VMEM_LIMIT)


def _resident(shape):
    return pl.BlockSpec(shape, lambda *_: (0,) * len(shape), pipeline_mode=pl.Buffered(1))


def _dot(a, b):
    return jnp.dot(a, b, preferred_element_type=_F32)


def _dot_nt(a, b):
    return lax.dot_general(a, b, (((1,), (1,)), ((), ())), preferred_element_type=_F32)


def _rms(x):
    return x * lax.rsqrt(jnp.mean(x * x, axis=-1, keepdims=True) + RMS_EPS)


def _split_bf16(x):
    hi = x.astype(_BF16)
    lo = (x - hi.astype(_F32)).astype(_BF16)
    return hi, lo


def _ada_kernel(c_ref, w_ref, b_ref, o_ref):
    c = c_ref[...]
    a = c * jax.nn.sigmoid(c)
    a_hi, a_lo = _split_bf16(a)
    w = w_ref[...]
    w_hi, w_lo = _split_bf16(w)
    o_ref[...] = _dot(a_hi, w_hi) + (_dot(a_lo, w_hi) + _dot(a_hi, w_lo)) + b_ref[...]


def _ada(cond, w_ada, b_ada):
    per_mod = D_MODEL // ADA_COL_TILE
    return pl.pallas_call(
        _ada_kernel,
        grid=(N_MOD, per_mod),
        in_specs=[
            pl.BlockSpec((COND_ROWS, D_MODEL), lambda j, h: (0, 0)),
            pl.BlockSpec((D_MODEL, ADA_COL_TILE), lambda j, h: (0, j * per_mod + h)),
            pl.BlockSpec((1, ADA_COL_TILE), lambda j, h: (0, j * per_mod + h)),
        ],
        out_specs=pl.BlockSpec((None, COND_ROWS, ADA_COL_TILE), lambda j, h: (j, 0, h)),
        out_shape=jax.ShapeDtypeStruct((N_MOD, COND_ROWS, D_MODEL), _F32),
        compiler_params=_params(2),
        name="ada",
    )(cond, w_ada, b_ada.reshape(1, N_MOD * D_MODEL))


def _rope_tables(seq):
    t = jnp.arange(seq)
    row = (t // GRID_W).astype(_F32)
    col = (t % GRID_W).astype(_F32)
    half = HEAD_DIM // 2
    inv_freq = ROPE_THETA ** (-jnp.arange(0, half, 2, dtype=_F32) / half)
    ang_r = row[:, None] * inv_freq[None, :]
    ang_c = col[:, None] * inv_freq[None, :]
    cos = jnp.concatenate([jnp.cos(ang_r)] * 2 + [jnp.cos(ang_c)] * 2, axis=-1)
    sin = jnp.concatenate([-jnp.sin(ang_r), jnp.sin(ang_r), -jnp.sin(ang_c), jnp.sin(ang_c)], axis=-1)
    return cos, sin


def _cond_row(latent, seq_tiles):
    if not latent:
        return 0
    return 1 + pl.program_id(0) // seq_tiles


def _inproj_kernel(latent, seq_tiles, x_ref, mod_ref, g1_ref, w_ref, gq_ref, gk_ref, *rest):
    if latent:
        cos_ref, sin_ref, q_ref, k_ref, v_ref, bg_ref, cu_ref = rest
    else:
        q_ref, k_ref, v_ref, bg_ref, cu_ref, kstate_ref, vstate_ref = rest
    r = _cond_row(latent, seq_tiles)
    sh = mod_ref[0, pl.ds(r, 1), :]
    sc = mod_ref[1, pl.ds(r, 1), :]
    h = _rms(x_ref[...]) * (g1_ref[...] * (1.0 + sc)) + sh
    hb = h.astype(_BF16)

    if latent:
        cos = cos_ref[...]
        sin = sin_ref[...]
        lane = lax.broadcasted_iota(jnp.int32, cos.shape, 1)
        first = (lane % (HEAD_DIM // 2)) < (HEAD_DIM // 4)

    def head_norm(xh, g):
        xh = _rms(xh) * g
        if latent:
            partner = jnp.where(first, pltpu.roll(xh, HEAD_DIM - HEAD_DIM // 4, axis=1),
                                pltpu.roll(xh, HEAD_DIM // 4, axis=1))
            xh = xh * cos + partner * sin
        return xh

    gq = gq_ref[...] * (HEAD_DIM ** -0.5 * LOG2_E)
    q = _dot(hb, w_ref[:, 0:D_ATTN])
    for hd in range(N_HEADS):
        cols = slice(hd * HEAD_DIM, (hd + 1) * HEAD_DIM)
        q_ref[:, cols] = head_norm(q[:, cols], gq).astype(q_ref.dtype)
    k = _dot(hb, w_ref[:, D_ATTN:D_ATTN + D_KV])
    o = D_ATTN + D_KV
    v = _dot(hb, w_ref[:, o:o + D_KV])
    for hd in range(N_KV_HEADS):
        cols = slice(hd * HEAD_DIM, (hd + 1) * HEAD_DIM)
        kh = head_norm(k[:, cols], gk_ref[...])
        k_ref[:, cols] = kh
        v_ref[:, cols] = v[:, cols]
        if not latent:
            kstate_ref[:, hd, :] = kh
            vstate_ref[:, hd, :] = v[:, cols]
    o += D_KV
    bg_ref[...] = _dot(hb, w_ref[:, o:o + D_CONV])
    o += D_CONV
    cg = _dot(hb, w_ref[:, o:o + D_CONV])
    o += D_CONV
    cu_ref[...] = cg * _dot(hb, w_ref[:, o:o + D_CONV])


def _inproj(x, mod, g_norm1, w_in_b, g_q, g_k, latent, seq):
    n_tok = x.shape[0]
    seq_tiles = seq // INPROJ_TILE
    tile = lambda w: pl.BlockSpec((INPROJ_TILE, w), lambda i: (i, 0))
    in_specs = [
        tile(D_MODEL),
        _resident((N_MOD, COND_ROWS, D_MODEL)),
        _resident((1, D_MODEL)),
        _resident((D_MODEL, D_IN_PROJ)),
        _resident((1, HEAD_DIM)),
        _resident((1, HEAD_DIM)),
    ]
    args = [x, mod, g_norm1.reshape(1, D_MODEL), w_in_b, g_q.reshape(1, HEAD_DIM), g_k.reshape(1, HEAD_DIM)]
    if latent:
        cos, sin = _rope_tables(seq)
        rope_spec = pl.BlockSpec((INPROJ_TILE, HEAD_DIM), lambda i: (i % seq_tiles, 0))
        in_specs += [rope_spec, rope_spec]
        args += [cos, sin]
    out_specs = [tile(D_ATTN), tile(D_KV), tile(D_KV), tile(D_CONV), tile(D_CONV)]
    out_shape = [
        jax.ShapeDtypeStruct((n_tok, D_ATTN), _BF16),
        jax.ShapeDtypeStruct((n_tok, D_KV), _F32),
        jax.ShapeDtypeStruct((n_tok, D_KV), _F32),
        jax.ShapeDtypeStruct((n_tok, D_CONV), _F32),
        jax.ShapeDtypeStruct((n_tok, D_CONV), _F32),
    ]
    if not latent:
        state_spec = pl.BlockSpec((INPROJ_TILE, N_KV_HEADS, HEAD_DIM), lambda i: (i, 0, 0))
        out_specs += [state_spec, state_spec]
        out_shape += [jax.ShapeDtypeStruct((n_tok, N_KV_HEADS, HEAD_DIM), _F32)] * 2
    return pl.pallas_call(
        functools.partial(_inproj_kernel, latent, seq_tiles),
        grid=(n_tok // INPROJ_TILE,),
        in_specs=in_specs,
        out_specs=out_specs,
        out_shape=out_shape,
        compiler_params=_params(1),
        name="inproj_lat" if latent else "inproj_ctx",
    )(*args)


def _mix_kernel(latent, seq_tiles, n_tiles, x_ref, q_ref, *rest):
    n_kv = 2 if latent else 1
    kv_refs = rest[:2 * n_kv]
    rest = rest[2 * n_kv:]
    bg_ref, cu_ref = rest[:2]
    rest = rest[2:]
    if latent:
        cu_prev_ref, cu_next_ref = rest[:2]
        rest = rest[2:]
    (convw_ref, ga_ref, gc_ref, wout_ref, mod_ref, g2_ref, wr_ref,
     xo_ref, h2_ref, lg_ref, mixed_scr) = rest

    s = pl.program_id(0)

    @pl.when(s == 0)
    def _():
        mixed_scr[...] = jnp.zeros_like(mixed_scr)

    r = 1 + jnp.maximum(s - 1, 0) // seq_tiles if latent else 0
    y = _dot(mixed_scr[...], wout_ref[...])
    x_new = x_ref[...] + mod_ref[2, pl.ds(r, 1), :] * y
    xo_ref[...] = x_new
    h2 = _rms(x_new) * (g2_ref[...] * (1.0 + mod_ref[4, pl.ds(r, 1), :])) + mod_ref[3, pl.ds(r, 1), :]
    h2_hi = h2.astype(_BF16)
    h2_ref[...] = h2_hi
    lg_ref[...] = _dot_nt(wr_ref[...].astype(_BF16), h2_hi)

    cu = cu_ref[...]
    if latent:
        pos = jnp.minimum(s, n_tiles - 1) % seq_tiles
        edge_prev = jnp.where(pos == 0, 0.0, 1.0) * cu_prev_ref[HALO_ROWS - 1:HALO_ROWS, :]
        edge_next = jnp.where(pos == seq_tiles - 1, 0.0, 1.0) * cu_next_ref[0:1, :]
    else:
        edge_prev = jnp.zeros((1, D_CONV), _F32)
        edge_next = edge_prev
    prev = pltpu.roll(cu, 1, axis=0)
    nxt = pltpu.roll(cu, TOKEN_TILE - 1, axis=0)
    sub = lax.broadcasted_iota(jnp.int32, (HALO_ROWS, D_CONV), 0)
    prev = jnp.concatenate([jnp.where(sub == 0, edge_prev, prev[0:HALO_ROWS]), prev[HALO_ROWS:]], axis=0)
    nxt = jnp.concatenate([nxt[:TOKEN_TILE - HALO_ROWS],
                           jnp.where(sub == HALO_ROWS - 1, edge_next, nxt[TOKEN_TILE - HALO_ROWS:])], axis=0)
    conv = bg_ref[...] * (prev * convw_ref[0:1, :] + cu * convw_ref[1:2, :] + nxt * convw_ref[2:3, :])
    mixed_scr[:, D_ATTN:D_ATTN + D_CONV] = (_rms(conv) * gc_ref[...]).astype(_BF16)

    row_sum_on_mxu = latent
    stacked = 1 if latent else CONTEXT_HEADS_STACKED
    heads = []
    for kvh in range(N_KV_HEADS):
        cols = slice(kvh * HEAD_DIM, (kvh + 1) * HEAD_DIM)
        ks = [kv_refs[2 * p][:, cols].astype(_BF16) for p in range(n_kv)]
        vs = [kv_refs[2 * p + 1][:, cols].astype(_BF16) for p in range(n_kv)]
        if row_sum_on_mxu:
            vs = [jnp.concatenate([vp, jnp.ones_like(vp)], axis=1) for vp in vs]
        for g0 in range(0, GQA_GROUP, stacked):
            first = kvh * GQA_GROUP + g0
            qh = jnp.concatenate([q_ref[:, (first + i) * HEAD_DIM:(first + i + 1) * HEAD_DIM]
                                  for i in range(stacked)], axis=0)
            sc = [_dot_nt(qh, kp) for kp in ks]
            m = functools.reduce(jnp.maximum, [jnp.max(sp, axis=-1, keepdims=True) for sp in sc])
            if row_sum_on_mxu:
                ol = functools.reduce(jnp.add, [_dot(jnp.exp2(sp - m).astype(_BF16), vp) for sp, vp in zip(sc, vs)])
                o = ol[:, 0:HEAD_DIM] / ol[:, HEAD_DIM:HEAD_DIM + 1]
            else:
                p = [jnp.exp2(sp - m) for sp in sc]
                l = functools.reduce(jnp.add, [jnp.sum(pp, axis=-1, keepdims=True) for pp in p])
                o = functools.reduce(jnp.add, [_dot(pp.astype(_BF16), vp) for pp, vp in zip(p, vs)]) / l
            heads += [o[i * TOKEN_TILE:(i + 1) * TOKEN_TILE] for i in range(stacked)]
    ssq = functools.reduce(jnp.add, [jnp.sum(h * h, axis=-1, keepdims=True) for h in heads])
    inv = lax.rsqrt(ssq * (1.0 / D_ATTN) + RMS_EPS)
    for hd, h in enumerate(heads):
        hc = slice(hd * HEAD_DIM, (hd + 1) * HEAD_DIM)
        mixed_scr[:, hc] = (h * inv * ga_ref[:, hc]).astype(_BF16)


def _mix(x, q, kv_parts, bg, cu, mod, conv_w, g_attn_out, g_conv_out, w_out_b, g_norm2, w_router_t,
         latent, seq):
    n_tok = x.shape[0]
    seq_tiles = seq // TOKEN_TILE
    n_tiles = n_tok // TOKEN_TILE
    started = lambda i: jnp.minimum(i, n_tiles - 1)
    finished = lambda i: jnp.maximum(i - 1, 0)
    start_tile = lambda w: pl.BlockSpec((TOKEN_TILE, w), lambda i: (started(i), 0))
    finish_tile = lambda w: pl.BlockSpec((TOKEN_TILE, w), lambda i: (finished(i), 0))
    in_specs = [finish_tile(D_MODEL), start_tile(D_ATTN)]
    args = [x, q]
    for k, v in kv_parts:
        spec = pl.BlockSpec((None, k.shape[1], D_KV), lambda i: (started(i) // seq_tiles, 0, 0))
        in_specs += [spec, spec]
        args += [k, v]
    in_specs += [start_tile(D_CONV), start_tile(D_CONV)]
    args += [bg, cu]
    if latent:
        per_tile = TOKEN_TILE // HALO_ROWS
        last = n_tok // HALO_ROWS - 1
        in_specs += [
            pl.BlockSpec((HALO_ROWS, D_CONV), lambda i: (jnp.maximum(started(i) * per_tile - 1, 0), 0)),
            pl.BlockSpec((HALO_ROWS, D_CONV), lambda i: (jnp.minimum((started(i) + 1) * per_tile, last), 0)),
        ]
        args += [cu, cu]
    in_specs += [
        _resident((3, D_CONV)),
        _resident((1, D_ATTN)),
        _resident((1, D_CONV)),
        _resident((D_ATTN + D_CONV, D_MODEL)),
        _resident((N_MOD, COND_ROWS, D_MODEL)),
        _resident((1, D_MODEL)),
        _resident((N_EXPERTS, D_MODEL)),
    ]
    args += [conv_w, g_attn_out.reshape(1, D_ATTN), g_conv_out.reshape(1, D_CONV), w_out_b, mod,
             g_norm2.reshape(1, D_MODEL), w_router_t]
    return pl.pallas_call(
        functools.partial(_mix_kernel, latent, seq_tiles, n_tiles),
        grid=(n_tiles + 1,),
        in_specs=in_specs,
        out_specs=[finish_tile(D_MODEL), finish_tile(D_MODEL),
                   pl.BlockSpec((N_EXPERTS, TOKEN_TILE), lambda i: (0, finished(i)))],
        out_shape=[
            jax.ShapeDtypeStruct((n_tok, D_MODEL), _F32),
            jax.ShapeDtypeStruct((n_tok, D_MODEL), _BF16),
            jax.ShapeDtypeStruct((N_EXPERTS, n_tok), _F32),
        ],
        scratch_shapes=[pltpu.VMEM((TOKEN_TILE, D_ATTN + D_CONV), _BF16)],
        compiler_params=_params(1),
        name="mix_lat" if latent else "mix_ctx",
    )(*args)


def _route_kernel(seq, cap, sets, e_blk, lg_ref, h2_ref, xs_ref, gw_ref, aff_scr, key_scr):
    j = pl.program_id(1)

    @pl.when(j == 0)
    def _():
        other = lax.broadcasted_iota(jnp.int32, (seq, seq), 0)
        token = lax.broadcasted_iota(jnp.int32, (seq, seq), 1)
        earlier = jnp.where(other < token, 1.0, 0.0)
        tie = earlier[0:LANES, 0:LANES]
        before = earlier.astype(_BF16)
        for s in range(sets):
            lg = lg_ref[:, s * seq:(s + 1) * seq]
            ex = jnp.exp(lg - jnp.max(lg, axis=0, keepdims=True))
            aff = ex / jnp.sum(ex, axis=0, keepdims=True)
            aff_t = jnp.concatenate([aff, jnp.zeros((LANES - N_EXPERTS, seq), _F32)], axis=0).T
            ranks = []
            for e in range(N_EXPERTS):
                theirs = aff_t[:, e:e + 1]
                blocks = []
                for lo in range(0, seq, LANES):
                    hi = lo + LANES
                    mine = aff[e:e + 1, lo:hi]
                    diag = theirs[lo:hi]
                    cnt = jnp.sum(jnp.where(diag > mine, 1.0, jnp.where(diag == mine, tie, 0.0)),
                                  axis=0, keepdims=True)
                    if lo > 0:
                        cnt += jnp.sum(jnp.where(theirs[0:lo] >= mine, 1.0, 0.0), axis=0, keepdims=True)
                    if hi < seq:
                        cnt += jnp.sum(jnp.where(theirs[hi:seq] > mine, 1.0, 0.0), axis=0, keepdims=True)
                    blocks.append(cnt)
                ranks.append(jnp.concatenate(blocks, axis=1))
            rank = jnp.concatenate(ranks, axis=0)
            sel = rank < float(cap)
            slot = _dot(jnp.where(sel, 1.0, 0.0).astype(_BF16), before)
            rows = slice(s * N_EXPERTS, (s + 1) * N_EXPERTS)
            aff_scr[rows, :] = aff
            key_scr[rows, :] = jnp.where(sel, slot, -1.0)

    c_idx = lax.broadcasted_iota(jnp.int32, (cap, seq), 0).astype(_F32)
    for s in range(sets):
        onehot, gated = [], []
        for i in range(e_blk):
            e = s * N_EXPERTS + j * e_blk + i
            hit = c_idx == key_scr[pl.ds(e, 1), :]
            onehot.append(jnp.where(hit, 1.0, 0.0))
            gated.append(jnp.where(hit, aff_scr[pl.ds(e, 1), :], 0.0))
        xs = _dot(jnp.concatenate(onehot, axis=0).astype(_BF16), h2_ref[s * seq:(s + 1) * seq, :]).astype(xs_ref.dtype)
        for i in range(e_blk):
            xs_ref[i, s * cap:(s + 1) * cap, :] = xs[i * cap:(i + 1) * cap, :]
        gw_ref[s] = jnp.concatenate(gated, axis=0).astype(gw_ref.dtype)


def _route(logits_t, h2, seq, sets, e_blk):
    n_tok = h2.shape[0]
    n_sets = n_tok // seq
    cap = EC_CAPACITY_FACTOR * seq // N_EXPERTS
    return pl.pallas_call(
        functools.partial(_route_kernel, seq, cap, sets, e_blk),
        grid=(n_sets // sets, N_EXPERTS // e_blk),
        in_specs=[
            pl.BlockSpec((N_EXPERTS, sets * seq), lambda s, j: (0, s)),
            pl.BlockSpec((sets * seq, D_MODEL), lambda s, j: (s, 0)),
        ],
        out_specs=[
            pl.BlockSpec((e_blk, sets * cap, D_MODEL), lambda s, j: (j, s, 0)),
            pl.BlockSpec((sets, e_blk * cap, seq), lambda s, j: (s, j, 0)),
        ],
        out_shape=[
            jax.ShapeDtypeStruct((N_EXPERTS, n_sets * cap, D_MODEL), _BF16),
            jax.ShapeDtypeStruct((n_sets, N_EXPERTS * cap, seq), _BF16),
        ],
        scratch_shapes=[pltpu.VMEM((sets * N_EXPERTS, seq), _F32), pltpu.VMEM((sets * N_EXPERTS, seq), _F32)],
        compiler_params=_params(2),
        name="route_%d" % seq,
    )(logits_t, h2)


def _ffn_kernel(xa_ref, xb_ref, wg_ref, wu_ref, wd_ref, ya_ref, yb_ref, acc_a, acc_b):
    def run(first):
        wg = wg_ref[...].astype(_BF16)
        wu = wu_ref[...].astype(_BF16)
        wd = wd_ref[...].astype(_BF16)
        for x_ref, y_ref, acc in ((xa_ref, ya_ref, acc_a), (xb_ref, yb_ref, acc_b)):
            chunk = min(FFN_ROW_CHUNK, x_ref.shape[0])
            for c in range(x_ref.shape[0] // chunk):
                rows = slice(c * chunk, (c + 1) * chunk)
                x = x_ref[rows, :]
                a = _dot(x, wg)
                u = _dot(x, wu)
                total = _dot((a * jax.nn.sigmoid(a) * u).astype(_BF16), wd)
                if not first:
                    total = acc[rows, :] + total
                acc[rows, :] = total
                y_ref[rows, :] = total.astype(y_ref.dtype)

    @pl.when(pl.program_id(1) == 0)
    def _():
        run(True)

    @pl.when(pl.program_id(1) > 0)
    def _():
        run(False)


def _ffn(xs_a, xs_b, w_gate, w_up, w_down):
    rows_a, rows_b = xs_a.shape[1], xs_b.shape[1]
    rows = lambda n: pl.BlockSpec((None, n, D_MODEL), lambda e, f: (e, 0, 0))
    return pl.pallas_call(
        _ffn_kernel,
        grid=(N_EXPERTS, D_FF // FF_TILE),
        in_specs=[
            rows(rows_a), rows(rows_b),
            pl.BlockSpec((None, D_MODEL, FF_TILE), lambda e, f: (e, 0, f)),
            pl.BlockSpec((None, D_MODEL, FF_TILE), lambda e, f: (e, 0, f)),
            pl.BlockSpec((None, FF_TILE, D_MODEL), lambda e, f: (e, f, 0)),
        ],
        out_specs=[rows(rows_a), rows(rows_b)],
        out_shape=[jax.ShapeDtypeStruct(xs_a.shape, _BF16), jax.ShapeDtypeStruct(xs_b.shape, _BF16)],
        scratch_shapes=[pltpu.VMEM((rows_a, D_MODEL), _F32), pltpu.VMEM((rows_b, D_MODEL), _F32)],
        compiler_params=_params(2),
        name="ffn",
    )(xs_a, xs_b, w_gate, w_up, w_down)


def _combine_kernel(latent, x_ref, y_ref, gw_ref, mod_ref, o_ref):
    r = 1 + pl.program_id(0) if latent else 0
    sets, rows, tile = gw_ref.shape
    cap = rows // N_EXPERTS
    gate = mod_ref[5, pl.ds(r, 1), :]
    for i in range(sets):
        y = y_ref[:, i * cap:(i + 1) * cap, :].reshape(rows, D_MODEL)
        moe = lax.dot_general(gw_ref[i], y, (((0,), (0,)), ((), ())), preferred_element_type=_F32)
        tok = slice(i * tile, (i + 1) * tile)
        o_ref[tok, :] = x_ref[tok, :] + gate * moe


def _combine(x_new, y, gw, mod, latent, seq):
    n_tok = x_new.shape[0]
    n_sets = n_tok // seq
    cap = EC_CAPACITY_FACTOR * seq // N_EXPERTS
    sets = max(1, COMBINE_TOKENS // seq)
    tile = COMBINE_TOKENS // sets
    per_set = seq // tile
    tokens = pl.BlockSpec((COMBINE_TOKENS, D_MODEL), lambda s, t: (s * per_set + t, 0))
    return pl.pallas_call(
        functools.partial(_combine_kernel, latent),
        grid=(n_sets // sets, per_set),
        in_specs=[
            tokens,
            pl.BlockSpec((N_EXPERTS, sets * cap, D_MODEL), lambda s, t: (0, s, 0)),
            pl.BlockSpec((sets, N_EXPERTS * cap, tile), lambda s, t: (s, 0, t)),
            _resident((N_MOD, COND_ROWS, D_MODEL)),
        ],
        out_specs=tokens,
        out_shape=jax.ShapeDtypeStruct((n_tok, D_MODEL), _F32),
        compiler_params=_params(2),
        name="combine_lat" if latent else "combine_ctx",
    )(x_new, y, gw, mod)


def kernel(x_prompt, x_sample, cache_k, cache_v, c, c_ctx, w_ada, b_ada, g_norm1, w_in, g_q, g_k, conv_w,
           g_attn_out, g_conv_out, w_out, g_norm2, w_router, w_gate, w_up, w_down):
    batch, seq, _ = x_prompt.shape
    dec_batch, dec_seq, _ = x_sample.shape
    depth = w_ada.shape[0]
    assert 1 + dec_batch <= COND_ROWS

    cond = jnp.concatenate([c_ctx[None, :], c, jnp.zeros((COND_ROWS - 1 - dec_batch, D_MODEL), _F32)], axis=0)
    xp = x_prompt.reshape(batch * seq, D_MODEL)
    xl = x_sample.reshape(dec_batch * dec_seq, D_MODEL)
    past = cache_k.shape[2]

    new_k, new_v = [], []
    for l in range(depth):
        mod = _ada(cond, w_ada[l], b_ada[l])
        w_in_b = w_in[l].astype(_BF16)
        w_out_b = w_out[l].astype(_BF16)
        w_router_t = w_router[l].T

        q_p, k_p, v_p, bg_p, cu_p, kstate, vstate = _inproj(xp, mod, g_norm1[l], w_in_b, g_q[l], g_k[l],
                                                            False, seq)
        q_l, k_l, v_l, bg_l, cu_l = _inproj(xl, mod, g_norm1[l], w_in_b, g_q[l], g_k[l], True, dec_seq)

        shared = (mod, conv_w[l], g_attn_out[l], g_conv_out[l], w_out_b, g_norm2[l], w_router_t)
        kv_p = [(k_p.reshape(batch, seq, D_KV), v_p.reshape(batch, seq, D_KV))]
        kv_l = [(cache_k[:, l].reshape(dec_batch, past, D_KV), cache_v[:, l].reshape(dec_batch, past, D_KV)),
                (k_l.reshape(dec_batch, dec_seq, D_KV), v_l.reshape(dec_batch, dec_seq, D_KV))]
        xp1, h2_p, lg_p = _mix(xp, q_p, kv_p, bg_p, cu_p, *shared, False, seq)
        xl1, h2_l, lg_l = _mix(xl, q_l, kv_l, bg_l, cu_l, *shared, True, dec_seq)

        xs_p, gw_p = _route(lg_p, h2_p, seq, CONTEXT_ROUTE_SETS, N_EXPERTS)
        xs_l, gw_l = _route(lg_l, h2_l, dec_seq, 1, LATENT_ROUTE_EXPERTS)
        y_p, y_l = _ffn(xs_p, xs_l, w_gate[l], w_up[l], w_down[l])
        xp = _combine(xp1, y_p, gw_p, mod, False, seq)
        xl = _combine(xl1, y_l, gw_l, mod, True, dec_seq)

        new_k.append(kstate.reshape(batch, seq, N_KV_HEADS, HEAD_DIM))
        new_v.append(vstate.reshape(batch, seq, N_KV_HEADS, HEAD_DIM))

    return (xp.reshape(batch, seq, D_MODEL), xl.reshape(dec_batch, dec_seq, D_MODEL),
            jnp.stack(new_k, axis=1), jnp.stack(new_v, axis=1))
```

```python
import functools

import numpy as np
import jax
import jax.numpy as jnp
from jax import lax
from jax.experimental import pallas as pl
from jax.experimental.pallas import tpu as pltpu

D_MODEL = 2048
GRID_W = 64
HEAD_DIM = 128
D_ATTN = 1024
D_CONV = 1024
N_HEADS = 8
N_KV_HEADS = 2
GQA_GROUP = N_HEADS // N_KV_HEADS
D_KV = N_KV_HEADS * HEAD_DIM
N_EXPERTS = 16
EC_CAPACITY_FACTOR = 2
D_FF = 1024
ROPE_THETA = 10000.0
RMS_EPS = 1e-6
LOG2_E = 1.4426950408889634
N_MOD = 6
D_IN_PROJ = D_ATTN + 2 * D_KV + 3 * D_CONV

TOKEN_TILE = 256
INPROJ_TILE = 512
ADA_COL_TILE = 1024
FF_TILE = 256
FFN_ROW_CHUNK = 512
COND_ROWS = 8
HALO_ROWS = 8
LATENT_ROUTE_EXPERTS = 4
CONTEXT_ROUTE_SETS = 4
CONTEXT_HEADS_STACKED = 2
VMEM_LIMIT = 52 * 1024 * 1024
LANES = 128
STAGE_COLS = 256
CONTEXT_COMBINE_TOKENS = 1024
LATENT_COMBINE_TOKENS = 512

_BF16 = jnp.bfloat16
_F32 = jnp.float32


def _params(n_axes):
    return pltpu.CompilerParams(dimension_semantics=("arbitrary",) * n_axes,
                                vmem_limit_bytes=VMEM_LIMIT)


def _resident(shape):
    return pl.BlockSpec(shape, lambda *_: (0,) * len(shape), pipeline_mode=pl.Buffered(1))


def _dot(a, b):
    return jnp.dot(a, b, preferred_element_type=_F32)


def _dot_nt(a, b):
    return lax.dot_general(a, b, (((1,), (1,)), ((), ())), preferred_element_type=_F32)


def _rms(x):
    return x * lax.rsqrt(jnp.mean(x * x, axis=-1, keepdims=True) + RMS_EPS)


def _split_bf16(x):
    hi = x.astype(_BF16)
    lo = (x - hi.astype(_F32)).astype(_BF16)
    return hi, lo


def _weight_scratch(rows, cols):
    return [pltpu.VMEM((rows, cols), _BF16), pltpu.VMEM((2, rows, STAGE_COLS), _F32),
            pltpu.SemaphoreType.DMA((2,)), pltpu.SemaphoreType.DMA(())]


def _cast_weight_once(w_hbm, w_pub, w_scr, stage, sems, pub_sem, step, n_steps):
    n_chunks = w_scr.shape[1] // STAGE_COLS

    def fetch(c):
        return pltpu.make_async_copy(w_hbm.at[:, pl.ds(c * STAGE_COLS, STAGE_COLS)], stage.at[c % 2], sems.at[c % 2])

    publish = pltpu.make_async_copy(w_scr, w_pub, pub_sem)

    @pl.when(step == 0)
    def _():
        fetch(0).start()
        fetch(1).start()
        for c in range(n_chunks):
            fetch(c).wait()
            w_scr[:, c * STAGE_COLS:(c + 1) * STAGE_COLS] = stage[c % 2].astype(_BF16)
            if c + 2 < n_chunks:
                fetch(c + 2).start()
        publish.start()

    @pl.when(step == n_steps - 1)
    def _():
        publish.wait()


def _ada_kernel(c_ref, w_ref, b_ref, o_ref):
    c = c_ref[...]
    a = c * jax.nn.sigmoid(c)
    a_hi, a_lo = _split_bf16(a)
    w = w_ref[...]
    w_hi, w_lo = _split_bf16(w)
    o_ref[...] = _dot(a_hi, w_hi) + (_dot(a_lo, w_hi) + _dot(a_hi, w_lo)) + b_ref[...]


def _ada(cond, w_ada, b_ada):
    per_mod = D_MODEL // ADA_COL_TILE
    return pl.pallas_call(
        _ada_kernel,
        grid=(N_MOD, per_mod),
        in_specs=[
            pl.BlockSpec((COND_ROWS, D_MODEL), lambda j, h: (0, 0)),
            pl.BlockSpec((D_MODEL, ADA_COL_TILE), lambda j, h: (0, j * per_mod + h)),
            pl.BlockSpec((1, ADA_COL_TILE), lambda j, h: (0, j * per_mod + h)),
        ],
        out_specs=pl.BlockSpec((None, COND_ROWS, ADA_COL_TILE), lambda j, h: (j, 0, h)),
        out_shape=jax.ShapeDtypeStruct((N_MOD, COND_ROWS, D_MODEL), _F32),
        compiler_params=_params(2),
        name="ada",
    )(cond, w_ada, b_ada.reshape(1, N_MOD * D_MODEL))


def _rope_tables(seq):
    t = jnp.arange(seq)
    row = (t // GRID_W).astype(_F32)
    col = (t % GRID_W).astype(_F32)
    half = HEAD_DIM // 2
    inv_freq = ROPE_THETA ** (-jnp.arange(0, half, 2, dtype=_F32) / half)
    ang_r = row[:, None] * inv_freq[None, :]
    ang_c = col[:, None] * inv_freq[None, :]
    cos = jnp.concatenate([jnp.cos(ang_r)] * 2 + [jnp.cos(ang_c)] * 2, axis=-1)
    sin = jnp.concatenate([-jnp.sin(ang_r), jnp.sin(ang_r), -jnp.sin(ang_c), jnp.sin(ang_c)], axis=-1)
    return cos, sin


def _cond_row(latent, seq_tiles):
    if not latent:
        return 0
    return 1 + pl.program_id(0) // seq_tiles


def _inproj_kernel(latent, seq_tiles, x_ref, mod_ref, g1_ref, w_ref, gq_ref, gk_ref, *rest):
    if latent:
        cos_ref, sin_ref, q_ref, k_ref, v_ref, bg_ref, cu_ref = rest
    else:
        q_ref, k_ref, v_ref, bg_ref, cu_ref, kstate_ref, vstate_ref, w_pub, *cast_scratch = rest
        _cast_weight_once(w_ref, w_pub, *cast_scratch, pl.program_id(0), pl.num_programs(0))
        w_ref = cast_scratch[0]
    r = _cond_row(latent, seq_tiles)
    sh = mod_ref[0, pl.ds(r, 1), :]
    sc = mod_ref[1, pl.ds(r, 1), :]
    h = _rms(x_ref[...]) * (g1_ref[...] * (1.0 + sc)) + sh
    hb = h.astype(_BF16)

    if latent:
        cos = cos_ref[...]
        sin = sin_ref[...]
        lane = lax.broadcasted_iota(jnp.int32, cos.shape, 1)
        first = (lane % (HEAD_DIM // 2)) < (HEAD_DIM // 4)

    def head_norm(xh, g):
        xh = _rms(xh) * g
        if latent:
            partner = jnp.where(first, pltpu.roll(xh, HEAD_DIM - HEAD_DIM // 4, axis=1),
                                pltpu.roll(xh, HEAD_DIM // 4, axis=1))
            xh = xh * cos + partner * sin
        return xh

    gq = gq_ref[...] * (HEAD_DIM ** -0.5 * LOG2_E)
    q = _dot(hb, w_ref[:, 0:D_ATTN])
    for hd in range(N_HEADS):
        cols = slice(hd * HEAD_DIM, (hd + 1) * HEAD_DIM)
        q_ref[:, cols] = head_norm(q[:, cols], gq).astype(q_ref.dtype)
    k = _dot(hb, w_ref[:, D_ATTN:D_ATTN + D_KV])
    o = D_ATTN + D_KV
    v = _dot(hb, w_ref[:, o:o + D_KV])
    for hd in range(N_KV_HEADS):
        cols = slice(hd * HEAD_DIM, (hd + 1) * HEAD_DIM)
        kh = head_norm(k[:, cols], gk_ref[...])
        k_ref[:, cols] = kh
        v_ref[:, cols] = v[:, cols]
        if not latent:
            kstate_ref[:, hd, :] = kh
            vstate_ref[:, hd, :] = v[:, cols]
    o += D_KV
    bg_ref[...] = _dot(hb, w_ref[:, o:o + D_CONV])
    o += D_CONV
    cg = _dot(hb, w_ref[:, o:o + D_CONV])
    o += D_CONV
    cu_ref[...] = cg * _dot(hb, w_ref[:, o:o + D_CONV])


def _inproj(x, mod, g_norm1, w_in, g_q, g_k, latent, seq):
    n_tok = x.shape[0]
    seq_tiles = seq // INPROJ_TILE
    tile = lambda w: pl.BlockSpec((INPROJ_TILE, w), lambda i: (i, 0))
    in_specs = [
        tile(D_MODEL),
        _resident((N_MOD, COND_ROWS, D_MODEL)),
        _resident((1, D_MODEL)),
        _resident((D_MODEL, D_IN_PROJ)) if latent else pl.BlockSpec(memory_space=pl.ANY),
        _resident((1, HEAD_DIM)),
        _resident((1, HEAD_DIM)),
    ]
    args = [x, mod, g_norm1.reshape(1, D_MODEL), w_in, g_q.reshape(1, HEAD_DIM), g_k.reshape(1, HEAD_DIM)]
    if latent:
        cos, sin = _rope_tables(seq)
        rope_spec = pl.BlockSpec((INPROJ_TILE, HEAD_DIM), lambda i: (i % seq_tiles, 0))
        in_specs += [rope_spec, rope_spec]
        args += [cos, sin]
    out_specs = [tile(D_ATTN), tile(D_KV), tile(D_KV), tile(D_CONV), tile(D_CONV)]
    out_shape = [
        jax.ShapeDtypeStruct((n_tok, D_ATTN), _BF16),
        jax.ShapeDtypeStruct((n_tok, D_KV), _F32),
        jax.ShapeDtypeStruct((n_tok, D_KV), _F32),
        jax.ShapeDtypeStruct((n_tok, D_CONV), _F32),
        jax.ShapeDtypeStruct((n_tok, D_CONV), _F32),
    ]
    if not latent:
        state_spec = pl.BlockSpec((INPROJ_TILE, N_KV_HEADS, HEAD_DIM), lambda i: (i, 0, 0))
        out_specs += [state_spec, state_spec, pl.BlockSpec(memory_space=pl.ANY)]
        out_shape += [jax.ShapeDtypeStruct((n_tok, N_KV_HEADS, HEAD_DIM), _F32)] * 2
        out_shape += [jax.ShapeDtypeStruct((D_MODEL, D_IN_PROJ), _BF16)]
    return pl.pallas_call(
        functools.partial(_inproj_kernel, latent, seq_tiles),
        grid=(n_tok // INPROJ_TILE,),
        in_specs=in_specs,
        out_specs=out_specs,
        out_shape=out_shape,
        scratch_shapes=[] if latent else _weight_scratch(D_MODEL, D_IN_PROJ),
        compiler_params=_params(1),
        name="inproj_lat" if latent else "inproj_ctx",
    )(*args)


def _mix_kernel(latent, seq_tiles, n_tiles, x_ref, q_ref, *rest):
    n_kv = 2 if latent else 1
    kv_refs = rest[:2 * n_kv]
    rest = rest[2 * n_kv:]
    bg_ref, cu_ref = rest[:2]
    rest = rest[2:]
    if latent:
        cu_prev_ref, cu_next_ref = rest[:2]
        rest = rest[2:]
    (convw_ref, ga_ref, gc_ref, wout_ref, mod_ref, g2_ref, wr_ref, xo_ref, h2_ref, lg_ref, *rest) = rest
    if latent:
        (mixed_scr,) = rest
    else:
        w_pub, mixed_scr, *cast_scratch = rest
        _cast_weight_once(wout_ref, w_pub, *cast_scratch, pl.program_id(0), pl.num_programs(0))
        wout_ref = cast_scratch[0]

    s = pl.program_id(0)

    @pl.when(s == 0)
    def _():
        mixed_scr[...] = jnp.zeros_like(mixed_scr)

    r = 1 + jnp.maximum(s - 1, 0) // seq_tiles if latent else 0
    y = _dot(mixed_scr[...], wout_ref[...])
    x_new = x_ref[...] + mod_ref[2, pl.ds(r, 1), :] * y
    xo_ref[...] = x_new
    h2 = _rms(x_new) * (g2_ref[...] * (1.0 + mod_ref[4, pl.ds(r, 1), :])) + mod_ref[3, pl.ds(r, 1), :]
    h2_hi = h2.astype(_BF16)
    h2_ref[...] = h2_hi
    lg_ref[...] = _dot_nt(wr_ref[...].astype(_BF16), h2_hi)

    cu = cu_ref[...]
    if latent:
        pos = jnp.minimum(s, n_tiles - 1) % seq_tiles
        edge_prev = jnp.where(pos == 0, 0.0, 1.0) * cu_prev_ref[HALO_ROWS - 1:HALO_ROWS, :]
        edge_next = jnp.where(pos == seq_tiles - 1, 0.0, 1.0) * cu_next_ref[0:1, :]
    else:
        edge_prev = jnp.zeros((1, D_CONV), _F32)
        edge_next = edge_prev
    prev = pltpu.roll(cu, 1, axis=0)
    nxt = pltpu.roll(cu, TOKEN_TILE - 1, axis=0)
    sub = lax.broadcasted_iota(jnp.int32, (HALO_ROWS, D_CONV), 0)
    prev = jnp.concatenate([jnp.where(sub == 0, edge_prev, prev[0:HALO_ROWS]), prev[HALO_ROWS:]], axis=0)
    nxt = jnp.concatenate([nxt[:TOKEN_TILE - HALO_ROWS],
                           jnp.where(sub == HALO_ROWS - 1, edge_next, nxt[TOKEN_TILE - HALO_ROWS:])], axis=0)
    conv = bg_ref[...] * (prev * convw_ref[0:1, :] + cu * convw_ref[1:2, :] + nxt * convw_ref[2:3, :])
    mixed_scr[:, D_ATTN:D_ATTN + D_CONV] = (_rms(conv) * gc_ref[...]).astype(_BF16)

    row_sum_on_mxu = latent
    stacked = 1 if latent else CONTEXT_HEADS_STACKED
    heads = []
    for kvh in range(N_KV_HEADS):
        cols = slice(kvh * HEAD_DIM, (kvh + 1) * HEAD_DIM)
        ks = [kv_refs[2 * p][:, cols].astype(_BF16) for p in range(n_kv)]
        vs = [kv_refs[2 * p + 1][:, cols].astype(_BF16) for p in range(n_kv)]
        if row_sum_on_mxu:
            vs = [jnp.concatenate([vp, jnp.ones_like(vp)], axis=1) for vp in vs]
        for g0 in range(0, GQA_GROUP, stacked):
            first = kvh * GQA_GROUP + g0
            qh = jnp.concatenate([q_ref[:, (first + i) * HEAD_DIM:(first + i + 1) * HEAD_DIM]
                                  for i in range(stacked)], axis=0)
            sc = [_dot_nt(qh, kp) for kp in ks]
            m = functools.reduce(jnp.maximum, [jnp.max(sp, axis=-1, keepdims=True) for sp in sc])
            if row_sum_on_mxu:
                ol = functools.reduce(jnp.add, [_dot(jnp.exp2(sp - m).astype(_BF16), vp) for sp, vp in zip(sc, vs)])
                o = ol[:, 0:HEAD_DIM] / ol[:, HEAD_DIM:HEAD_DIM + 1]
            else:
                p = [jnp.exp2(sp - m) for sp in sc]
                l = functools.reduce(jnp.add, [jnp.sum(pp, axis=-1, keepdims=True) for pp in p])
                o = functools.reduce(jnp.add, [_dot(pp.astype(_BF16), vp) for pp, vp in zip(p, vs)]) / l
            heads += [o[i * TOKEN_TILE:(i + 1) * TOKEN_TILE] for i in range(stacked)]
    ssq = functools.reduce(jnp.add, [jnp.sum(h * h, axis=-1, keepdims=True) for h in heads])
    inv = lax.rsqrt(ssq * (1.0 / D_ATTN) + RMS_EPS)
    for hd, h in enumerate(heads):
        hc = slice(hd * HEAD_DIM, (hd + 1) * HEAD_DIM)
        mixed_scr[:, hc] = (h * inv * ga_ref[:, hc]).astype(_BF16)


def _mix(x, q, kv_parts, bg, cu, mod, conv_w, g_attn_out, g_conv_out, w_out, g_norm2, w_router_t,
         latent, seq):
    n_tok = x.shape[0]
    seq_tiles = seq // TOKEN_TILE
    n_tiles = n_tok // TOKEN_TILE
    started = lambda i: jnp.minimum(i, n_tiles - 1)
    finished = lambda i: jnp.maximum(i - 1, 0)
    start_tile = lambda w: pl.BlockSpec((TOKEN_TILE, w), lambda i: (started(i), 0))
    finish_tile = lambda w: pl.BlockSpec((TOKEN_TILE, w), lambda i: (finished(i), 0))
    in_specs = [finish_tile(D_MODEL), start_tile(D_ATTN)]
    args = [x, q]
    for k, v in kv_parts:
        spec = pl.BlockSpec((None, k.shape[1], D_KV), lambda i: (started(i) // seq_tiles, 0, 0))
        in_specs += [spec, spec]
        args += [k, v]
    in_specs += [start_tile(D_CONV), start_tile(D_CONV)]
    args += [bg, cu]
    if latent:
        per_tile = TOKEN_TILE // HALO_ROWS
        last = n_tok // HALO_ROWS - 1
        in_specs += [
            pl.BlockSpec((HALO_ROWS, D_CONV), lambda i: (jnp.maximum(started(i) * per_tile - 1, 0), 0)),
            pl.BlockSpec((HALO_ROWS, D_CONV), lambda i: (jnp.minimum((started(i) + 1) * per_tile, last), 0)),
        ]
        args += [cu, cu]
    in_specs += [
        _resident((3, D_CONV)),
        _resident((1, D_ATTN)),
        _resident((1, D_CONV)),
        _resident((D_ATTN + D_CONV, D_MODEL)) if latent else pl.BlockSpec(memory_space=pl.ANY),
        _resident((N_MOD, COND_ROWS, D_MODEL)),
        _resident((1, D_MODEL)),
        _resident((N_EXPERTS, D_MODEL)),
    ]
    args += [conv_w, g_attn_out.reshape(1, D_ATTN), g_conv_out.reshape(1, D_CONV), w_out, mod,
             g_norm2.reshape(1, D_MODEL), w_router_t]
    out_specs = [finish_tile(D_MODEL), finish_tile(D_MODEL),
                 pl.BlockSpec((N_EXPERTS, TOKEN_TILE), lambda i: (0, finished(i)))]
    out_shape = [
        jax.ShapeDtypeStruct((n_tok, D_MODEL), _F32),
        jax.ShapeDtypeStruct((n_tok, D_MODEL), _BF16),
        jax.ShapeDtypeStruct((N_EXPERTS, n_tok), _F32),
    ]
    scratch_shapes = [pltpu.VMEM((TOKEN_TILE, D_ATTN + D_CONV), _BF16)]
    if not latent:
        out_specs += [pl.BlockSpec(memory_space=pl.ANY)]
        out_shape += [jax.ShapeDtypeStruct((D_ATTN + D_CONV, D_MODEL), _BF16)]
        scratch_shapes += _weight_scratch(D_ATTN + D_CONV, D_MODEL)
    return pl.pallas_call(
        functools.partial(_mix_kernel, latent, seq_tiles, n_tiles),
        grid=(n_tiles + 1,),
        in_specs=in_specs,
        out_specs=out_specs,
        out_shape=out_shape,
        scratch_shapes=scratch_shapes,
        compiler_params=_params(1),
        name="mix_lat" if latent else "mix_ctx",
    )(*args)


def _route_kernel(seq, cap, sets, e_blk, lg_ref, h2_ref, xs_ref, gw_ref, aff_scr, key_scr):
    j = pl.program_id(1)

    @pl.when(j == 0)
    def _():
        other = lax.broadcasted_iota(jnp.int32, (seq, seq), 0)
        token = lax.broadcasted_iota(jnp.int32, (seq, seq), 1)
        earlier = jnp.where(other < token, 1.0, 0.0)
        tie = earlier[0:LANES, 0:LANES]
        before = earlier.astype(_BF16)
        for s in range(sets):
            lg = lg_ref[:, s * seq:(s + 1) * seq]
            ex = jnp.exp(lg - jnp.max(lg, axis=0, keepdims=True))
            aff = ex / jnp.sum(ex, axis=0, keepdims=True)
            aff_t = jnp.concatenate([aff, jnp.zeros((LANES - N_EXPERTS, seq), _F32)], axis=0).T
            ranks = []
            for e in range(N_EXPERTS):
                theirs = aff_t[:, e:e + 1]
                blocks = []
                for lo in range(0, seq, LANES):
                    hi = lo + LANES
                    mine = aff[e:e + 1, lo:hi]
                    diag = theirs[lo:hi]
                    cnt = jnp.sum(jnp.where(diag > mine, 1.0, jnp.where(diag == mine, tie, 0.0)),
                                  axis=0, keepdims=True)
                    if lo > 0:
                        cnt += jnp.sum(jnp.where(theirs[0:lo] >= mine, 1.0, 0.0), axis=0, keepdims=True)
                    if hi < seq:
                        cnt += jnp.sum(jnp.where(theirs[hi:seq] > mine, 1.0, 0.0), axis=0, keepdims=True)
                    blocks.append(cnt)
                ranks.append(jnp.concatenate(blocks, axis=1))
            rank = jnp.concatenate(ranks, axis=0)
            sel = rank < float(cap)
            slot = _dot(jnp.where(sel, 1.0, 0.0).astype(_BF16), before)
            rows = slice(s * N_EXPERTS, (s + 1) * N_EXPERTS)
            aff_scr[rows, :] = aff
            key_scr[rows, :] = jnp.where(sel, slot, -1.0)

    c_idx = lax.broadcasted_iota(jnp.int32, (cap, seq), 0).astype(_F32)
    for s in range(sets):
        onehot, gated = [], []
        for i in range(e_blk):
            e = s * N_EXPERTS + j * e_blk + i
            hit = c_idx == key_scr[pl.ds(e, 1), :]
            onehot.append(jnp.where(hit, 1.0, 0.0))
            gated.append(jnp.where(hit, aff_scr[pl.ds(e, 1), :], 0.0))
        xs = _dot(jnp.concatenate(onehot, axis=0).astype(_BF16), h2_ref[s * seq:(s + 1) * seq, :]).astype(xs_ref.dtype)
        for i in range(e_blk):
            xs_ref[i, s * cap:(s + 1) * cap, :] = xs[i * cap:(i + 1) * cap, :]
        gw_ref[s] = jnp.concatenate(gated, axis=0).astype(gw_ref.dtype)


def _route(logits_t, h2, seq, sets, e_blk):
    n_tok = h2.shape[0]
    n_sets = n_tok // seq
    cap = EC_CAPACITY_FACTOR * seq // N_EXPERTS
    return pl.pallas_call(
        functools.partial(_route_kernel, seq, cap, sets, e_blk),
        grid=(n_sets // sets, N_EXPERTS // e_blk),
        in_specs=[
            pl.BlockSpec((N_EXPERTS, sets * seq), lambda s, j: (0, s)),
            pl.BlockSpec((sets * seq, D_MODEL), lambda s, j: (s, 0)),
        ],
        out_specs=[
            pl.BlockSpec((e_blk, sets * cap, D_MODEL), lambda s, j: (j, s, 0)),
            pl.BlockSpec((sets, e_blk * cap, seq), lambda s, j: (s, j, 0)),
        ],
        out_shape=[
            jax.ShapeDtypeStruct((N_EXPERTS, n_sets * cap, D_MODEL), _BF16),
            jax.ShapeDtypeStruct((n_sets, N_EXPERTS * cap, seq), _BF16),
        ],
        scratch_shapes=[pltpu.VMEM((sets * N_EXPERTS, seq), _F32), pltpu.VMEM((sets * N_EXPERTS, seq), _F32)],
        compiler_params=_params(2),
        name="route_%d" % seq,
    )(logits_t, h2)


def _ffn_kernel(xa_ref, xb_ref, wg_ref, wu_ref, wd_ref, ya_ref, yb_ref, acc_a, acc_b):
    def run(first):
        wg = wg_ref[...].astype(_BF16)
        wu = wu_ref[...].astype(_BF16)
        wd = wd_ref[...].astype(_BF16)
        for x_ref, y_ref, acc in ((xa_ref, ya_ref, acc_a), (xb_ref, yb_ref, acc_b)):
            chunk = min(FFN_ROW_CHUNK, x_ref.shape[0])
            for c in range(x_ref.shape[0] // chunk):
                rows = slice(c * chunk, (c + 1) * chunk)
                x = x_ref[rows, :]
                a = _dot(x, wg)
                u = _dot(x, wu)
                total = _dot((a * jax.nn.sigmoid(a) * u).astype(_BF16), wd)
                if not first:
                    total = acc[rows, :] + total
                acc[rows, :] = total
                y_ref[rows, :] = total.astype(y_ref.dtype)

    @pl.when(pl.program_id(1) == 0)
    def _():
        run(True)

    @pl.when(pl.program_id(1) > 0)
    def _():
        run(False)


def _ffn(xs_a, xs_b, w_gate, w_up, w_down):
    rows_a, rows_b = xs_a.shape[1], xs_b.shape[1]
    rows = lambda n: pl.BlockSpec((None, n, D_MODEL), lambda e, f: (e, 0, 0))
    return pl.pallas_call(
        _ffn_kernel,
        grid=(N_EXPERTS, D_FF // FF_TILE),
        in_specs=[
            rows(rows_a), rows(rows_b),
            pl.BlockSpec((None, D_MODEL, FF_TILE), lambda e, f: (e, 0, f)),
            pl.BlockSpec((None, D_MODEL, FF_TILE), lambda e, f: (e, 0, f)),
            pl.BlockSpec((None, FF_TILE, D_MODEL), lambda e, f: (e, f, 0)),
        ],
        out_specs=[rows(rows_a), rows(rows_b)],
        out_shape=[jax.ShapeDtypeStruct(xs_a.shape, _BF16), jax.ShapeDtypeStruct(xs_b.shape, _BF16)],
        scratch_shapes=[pltpu.VMEM((rows_a, D_MODEL), _F32), pltpu.VMEM((rows_b, D_MODEL), _F32)],
        compiler_params=_params(2),
        name="ffn",
    )(xs_a, xs_b, w_gate, w_up, w_down)


def _combine_kernel(latent, x_ref, y_ref, gw_ref, mod_ref, o_ref):
    r = 1 + pl.program_id(0) if latent else 0
    sets, rows, tile = gw_ref.shape
    cap = rows // N_EXPERTS
    gate = mod_ref[5, pl.ds(r, 1), :]
    for i in range(sets):
        y = y_ref[:, i * cap:(i + 1) * cap, :].reshape(rows, D_MODEL)
        moe = lax.dot_general(gw_ref[i], y, (((0,), (0,)), ((), ())), preferred_element_type=_F32)
        tok = slice(i * tile, (i + 1) * tile)
        o_ref[tok, :] = x_ref[tok, :] + gate * moe


def _combine(x_new, y, gw, mod, latent, seq, step_tokens):
    n_tok = x_new.shape[0]
    n_sets = n_tok // seq
    cap = EC_CAPACITY_FACTOR * seq // N_EXPERTS
    sets = max(1, step_tokens // seq)
    tile = step_tokens // sets
    per_set = seq // tile
    tokens = pl.BlockSpec((step_tokens, D_MODEL), lambda s, t: (s * per_set + t, 0))
    return pl.pallas_call(
        functools.partial(_combine_kernel, latent),
        grid=(n_sets // sets, per_set),
        in_specs=[
            tokens,
            pl.BlockSpec((N_EXPERTS, sets * cap, D_MODEL), lambda s, t: (0, s, 0)),
            pl.BlockSpec((sets, N_EXPERTS * cap, tile), lambda s, t: (s, 0, t)),
            _resident((N_MOD, COND_ROWS, D_MODEL)),
        ],
        out_specs=tokens,
        out_shape=jax.ShapeDtypeStruct((n_tok, D_MODEL), _F32),
        compiler_params=_params(2),
        name="combine_lat" if latent else "combine_ctx",
    )(x_new, y, gw, mod)


def kernel(x_prompt, x_sample, cache_k, cache_v, c, c_ctx, w_ada, b_ada, g_norm1, w_in, g_q, g_k, conv_w,
           g_attn_out, g_conv_out, w_out, g_norm2, w_router, w_gate, w_up, w_down):
    batch, seq, _ = x_prompt.shape
    dec_batch, dec_seq, _ = x_sample.shape
    depth = w_ada.shape[0]
    assert 1 + dec_batch <= COND_ROWS

    cond = jnp.concatenate([c_ctx[None, :], c, jnp.zeros((COND_ROWS - 1 - dec_batch, D_MODEL), _F32)], axis=0)
    xp = x_prompt.reshape(batch * seq, D_MODEL)
    xl = x_sample.reshape(dec_batch * dec_seq, D_MODEL)
    past = cache_k.shape[2]

    new_k, new_v = [], []
    for l in range(depth):
        mod = _ada(cond, w_ada[l], b_ada[l])
        w_router_t = w_router[l].T

        q_p, k_p, v_p, bg_p, cu_p, kstate, vstate, w_in_b = _inproj(xp, mod, g_norm1[l], w_in[l], g_q[l], g_k[l],
                                                                    False, seq)
        q_l, k_l, v_l, bg_l, cu_l = _inproj(xl, mod, g_norm1[l], w_in_b, g_q[l], g_k[l], True, dec_seq)

        shared = lambda w: (mod, conv_w[l], g_attn_out[l], g_conv_out[l], w, g_norm2[l], w_router_t)
        kv_p = [(k_p.reshape(batch, seq, D_KV), v_p.reshape(batch, seq, D_KV))]
        kv_l = [(cache_k[:, l].reshape(dec_batch, past, D_KV), cache_v[:, l].reshape(dec_batch, past, D_KV)),
                (k_l.reshape(dec_batch, dec_seq, D_KV), v_l.reshape(dec_batch, dec_seq, D_KV))]
        xp1, h2_p, lg_p, w_out_b = _mix(xp, q_p, kv_p, bg_p, cu_p, *shared(w_out[l]), False, seq)
        xl1, h2_l, lg_l = _mix(xl, q_l, kv_l, bg_l, cu_l, *shared(w_out_b), True, dec_seq)

        xs_p, gw_p = _route(lg_p, h2_p, seq, CONTEXT_ROUTE_SETS, N_EXPERTS)
        xs_l, gw_l = _route(lg_l, h2_l, dec_seq, 1, LATENT_ROUTE_EXPERTS)
        y_p, y_l = _ffn(xs_p, xs_l, w_gate[l], w_up[l], w_down[l])
        xp = _combine(xp1, y_p, gw_p, mod, False, seq, CONTEXT_COMBINE_TOKENS)
        xl = _combine(xl1, y_l, gw_l, mod, True, dec_seq, LATENT_COMBINE_TOKENS)

        new_k.append(kstate.reshape(batch, seq, N_KV_HEADS, HEAD_DIM))
        new_v.append(vstate.reshape(batch, seq, N_KV_HEADS, HEAD_DIM))

    return (xp.reshape(batch, seq, D_MODEL), xl.reshape(dec_batch, dec_seq, D_MODEL),
            jnp.stack(new_k, axis=1), jnp.stack(new_v, axis=1))
```

```python
import functools

import numpy as np
import jax
import jax.numpy as jnp
from jax import lax
from jax.experimental import pallas as pl
from jax.experimental.pallas import tpu as pltpu

D_MODEL = 2048
GRID_W = 64
HEAD_DIM = 128
D_ATTN = 1024
D_CONV = 1024
N_HEADS = 8
N_KV_HEADS = 2
GQA_GROUP = N_HEADS // N_KV_HEADS
D_KV = N_KV_HEADS * HEAD_DIM
N_EXPERTS = 16
EC_CAPACITY_FACTOR = 2
D_FF = 1024
ROPE_THETA = 10000.0
RMS_EPS = 1e-6
LOG2_E = 1.4426950408889634
N_MOD = 6
D_IN_PROJ = D_ATTN + 2 * D_KV + 3 * D_CONV

TOKEN_TILE = 256
INPROJ_TILE = 512
ADA_COL_TILE = 1024
FF_TILE = 256
FFN_ROW_CHUNK = 512
COND_ROWS = 8
HALO_ROWS = 8
LATENT_ROUTE_EXPERTS = 4
CONTEXT_ROUTE_SETS = 4
CONTEXT_HEADS_STACKED = 2
VMEM_LIMIT = 52 * 1024 * 1024
LANES = 128
STAGE_ROWS = 128
CONTEXT_COMBINE_TOKENS = 1024
LATENT_COMBINE_TOKENS = 512

_BF16 = jnp.bfloat16
_F32 = jnp.float32


def _params(n_axes):
    return pltpu.CompilerParams(dimension_semantics=("arbitrary",) * n_axes,
                                vmem_limit_bytes=VMEM_LIMIT)


def _resident(shape):
    return pl.BlockSpec(shape, lambda *_: (0,) * len(shape), pipeline_mode=pl.Buffered(1))


def _dot(a, b):
    return jnp.dot(a, b, preferred_element_type=_F32)


def _dot_nt(a, b):
    return lax.dot_general(a, b, (((1,), (1,)), ((), ())), preferred_element_type=_F32)


def _rms(x):
    return x * lax.rsqrt(jnp.mean(x * x, axis=-1, keepdims=True) + RMS_EPS)


def _split_bf16(x):
    hi = x.astype(_BF16)
    lo = (x - hi.astype(_F32)).astype(_BF16)
    return hi, lo


def _weight_scratch(rows, cols):
    return [pltpu.VMEM((rows, cols), _BF16), pltpu.VMEM((2, STAGE_ROWS, cols), _F32),
            pltpu.SemaphoreType.DMA((2,)), pltpu.SemaphoreType.DMA(())]


def _cast_weight_once(w_hbm, w_pub, w_scr, stage, sems, pub_sem, step, n_steps):
    n_chunks = w_scr.shape[0] // STAGE_ROWS

    def fetch(c):
        return pltpu.make_async_copy(w_hbm.at[pl.ds(c * STAGE_ROWS, STAGE_ROWS), :], stage.at[c % 2], sems.at[c % 2])

    publish = pltpu.make_async_copy(w_scr, w_pub, pub_sem)

    @pl.when(step == 0)
    def _():
        fetch(0).start()
        fetch(1).start()
        for c in range(n_chunks):
            fetch(c).wait()
            w_scr[c * STAGE_ROWS:(c + 1) * STAGE_ROWS, :] = stage[c % 2].astype(_BF16)
            if c + 2 < n_chunks:
                fetch(c + 2).start()
        publish.start()

    @pl.when(step == n_steps - 1)
    def _():
        publish.wait()


def _ada_kernel(c_ref, w_ref, b_ref, o_ref):
    c = c_ref[...]
    a = c * jax.nn.sigmoid(c)
    a_hi, a_lo = _split_bf16(a)
    w = w_ref[...]
    w_hi, w_lo = _split_bf16(w)
    o_ref[...] = _dot(a_hi, w_hi) + (_dot(a_lo, w_hi) + _dot(a_hi, w_lo)) + b_ref[...]


def _ada(cond, w_ada, b_ada):
    per_mod = D_MODEL // ADA_COL_TILE
    return pl.pallas_call(
        _ada_kernel,
        grid=(N_MOD, per_mod),
        in_specs=[
            pl.BlockSpec((COND_ROWS, D_MODEL), lambda j, h: (0, 0)),
            pl.BlockSpec((D_MODEL, ADA_COL_TILE), lambda j, h: (0, j * per_mod + h)),
            pl.BlockSpec((1, ADA_COL_TILE), lambda j, h: (0, j * per_mod + h)),
        ],
        out_specs=pl.BlockSpec((None, COND_ROWS, ADA_COL_TILE), lambda j, h: (j, 0, h)),
        out_shape=jax.ShapeDtypeStruct((N_MOD, COND_ROWS, D_MODEL), _F32),
        compiler_params=_params(2),
        name="ada",
    )(cond, w_ada, b_ada.reshape(1, N_MOD * D_MODEL))


def _rope_tables(seq):
    t = jnp.arange(seq)
    row = (t // GRID_W).astype(_F32)
    col = (t % GRID_W).astype(_F32)
    half = HEAD_DIM // 2
    inv_freq = ROPE_THETA ** (-jnp.arange(0, half, 2, dtype=_F32) / half)
    ang_r = row[:, None] * inv_freq[None, :]
    ang_c = col[:, None] * inv_freq[None, :]
    cos = jnp.concatenate([jnp.cos(ang_r)] * 2 + [jnp.cos(ang_c)] * 2, axis=-1)
    sin = jnp.concatenate([-jnp.sin(ang_r), jnp.sin(ang_r), -jnp.sin(ang_c), jnp.sin(ang_c)], axis=-1)
    return cos, sin


def _cond_row(latent, seq_tiles):
    if not latent:
        return 0
    return 1 + pl.program_id(0) // seq_tiles


def _inproj_kernel(latent, seq_tiles, x_ref, mod_ref, g1_ref, w_ref, gq_ref, gk_ref, *rest):
    if latent:
        cos_ref, sin_ref, q_ref, k_ref, v_ref, bg_ref, cu_ref = rest
    else:
        q_ref, k_ref, v_ref, bg_ref, cu_ref, kstate_ref, vstate_ref, w_pub, *cast_scratch = rest
        _cast_weight_once(w_ref, w_pub, *cast_scratch, pl.program_id(0), pl.num_programs(0))
        w_ref = cast_scratch[0]
    r = _cond_row(latent, seq_tiles)
    sh = mod_ref[0, pl.ds(r, 1), :]
    sc = mod_ref[1, pl.ds(r, 1), :]
    h = _rms(x_ref[...]) * (g1_ref[...] * (1.0 + sc)) + sh
    hb = h.astype(_BF16)

    if latent:
        cos = cos_ref[...]
        sin = sin_ref[...]
        lane = lax.broadcasted_iota(jnp.int32, cos.shape, 1)
        first = (lane % (HEAD_DIM // 2)) < (HEAD_DIM // 4)

    def head_norm(xh, g):
        xh = _rms(xh) * g
        if latent:
            partner = jnp.where(first, pltpu.roll(xh, HEAD_DIM - HEAD_DIM // 4, axis=1),
                                pltpu.roll(xh, HEAD_DIM // 4, axis=1))
            xh = xh * cos + partner * sin
        return xh

    gq = gq_ref[...] * (HEAD_DIM ** -0.5 * LOG2_E)
    q = _dot(hb, w_ref[:, 0:D_ATTN])
    for hd in range(N_HEADS):
        cols = slice(hd * HEAD_DIM, (hd + 1) * HEAD_DIM)
        q_ref[:, cols] = head_norm(q[:, cols], gq).astype(q_ref.dtype)
    k = _dot(hb, w_ref[:, D_ATTN:D_ATTN + D_KV])
    o = D_ATTN + D_KV
    v = _dot(hb, w_ref[:, o:o + D_KV])
    for hd in range(N_KV_HEADS):
        cols = slice(hd * HEAD_DIM, (hd + 1) * HEAD_DIM)
        kh = head_norm(k[:, cols], gk_ref[...])
        k_ref[:, cols] = kh
        v_ref[:, cols] = v[:, cols]
        if not latent:
            kstate_ref[:, hd, :] = kh
            vstate_ref[:, hd, :] = v[:, cols]
    o += D_KV
    bg_ref[...] = _dot(hb, w_ref[:, o:o + D_CONV])
    o += D_CONV
    cg = _dot(hb, w_ref[:, o:o + D_CONV])
    o += D_CONV
    cu_ref[...] = cg * _dot(hb, w_ref[:, o:o + D_CONV])


def _inproj(x, mod, g_norm1, w_in, g_q, g_k, latent, seq):
    n_tok = x.shape[0]
    seq_tiles = seq // INPROJ_TILE
    tile = lambda w: pl.BlockSpec((INPROJ_TILE, w), lambda i: (i, 0))
    in_specs = [
        tile(D_MODEL),
        _resident((N_MOD, COND_ROWS, D_MODEL)),
        _resident((1, D_MODEL)),
        _resident((D_MODEL, D_IN_PROJ)) if latent else pl.BlockSpec(memory_space=pl.ANY),
        _resident((1, HEAD_DIM)),
        _resident((1, HEAD_DIM)),
    ]
    args = [x, mod, g_norm1.reshape(1, D_MODEL), w_in, g_q.reshape(1, HEAD_DIM), g_k.reshape(1, HEAD_DIM)]
    if latent:
        cos, sin = _rope_tables(seq)
        rope_spec = pl.BlockSpec((INPROJ_TILE, HEAD_DIM), lambda i: (i % seq_tiles, 0))
        in_specs += [rope_spec, rope_spec]
        args += [cos, sin]
    out_specs = [tile(D_ATTN), tile(D_KV), tile(D_KV), tile(D_CONV), tile(D_CONV)]
    out_shape = [
        jax.ShapeDtypeStruct((n_tok, D_ATTN), _BF16),
        jax.ShapeDtypeStruct((n_tok, D_KV), _F32),
        jax.ShapeDtypeStruct((n_tok, D_KV), _F32),
        jax.ShapeDtypeStruct((n_tok, D_CONV), _F32),
        jax.ShapeDtypeStruct((n_tok, D_CONV), _F32),
    ]
    if not latent:
        state_spec = pl.BlockSpec((INPROJ_TILE, N_KV_HEADS, HEAD_DIM), lambda i: (i, 0, 0))
        out_specs += [state_spec, state_spec, pl.BlockSpec(memory_space=pl.ANY)]
        out_shape += [jax.ShapeDtypeStruct((n_tok, N_KV_HEADS, HEAD_DIM), _F32)] * 2
        out_shape += [jax.ShapeDtypeStruct((D_MODEL, D_IN_PROJ), _BF16)]
    return pl.pallas_call(
        functools.partial(_inproj_kernel, latent, seq_tiles),
        grid=(n_tok // INPROJ_TILE,),
        in_specs=in_specs,
        out_specs=out_specs,
        out_shape=out_shape,
        scratch_shapes=[] if latent else _weight_scratch(D_MODEL, D_IN_PROJ),
        compiler_params=_params(1),
        name="inproj_lat" if latent else "inproj_ctx",
    )(*args)


def _mix_kernel(latent, seq_tiles, n_tiles, x_ref, q_ref, *rest):
    n_kv = 2 if latent else 1
    kv_refs = rest[:2 * n_kv]
    rest = rest[2 * n_kv:]
    bg_ref, cu_ref = rest[:2]
    rest = rest[2:]
    if latent:
        cu_prev_ref, cu_next_ref = rest[:2]
        rest = rest[2:]
    (convw_ref, ga_ref, gc_ref, wout_ref, mod_ref, g2_ref, wr_ref, xo_ref, h2_ref, lg_ref, *rest) = rest
    if latent:
        (mixed_scr,) = rest
    else:
        w_pub, mixed_scr, *cast_scratch = rest
        _cast_weight_once(wout_ref, w_pub, *cast_scratch, pl.program_id(0), pl.num_programs(0))
        wout_ref = cast_scratch[0]

    s = pl.program_id(0)

    @pl.when(s == 0)
    def _():
        mixed_scr[...] = jnp.zeros_like(mixed_scr)

    r = 1 + jnp.maximum(s - 1, 0) // seq_tiles if latent else 0
    y = _dot(mixed_scr[...], wout_ref[...])
    x_new = x_ref[...] + mod_ref[2, pl.ds(r, 1), :] * y
    xo_ref[...] = x_new
    h2 = _rms(x_new) * (g2_ref[...] * (1.0 + mod_ref[4, pl.ds(r, 1), :])) + mod_ref[3, pl.ds(r, 1), :]
    h2_hi = h2.astype(_BF16)
    h2_ref[...] = h2_hi
    lg_ref[...] = _dot_nt(wr_ref[...].astype(_BF16), h2_hi)

    cu = cu_ref[...]
    if latent:
        pos = jnp.minimum(s, n_tiles - 1) % seq_tiles
        edge_prev = jnp.where(pos == 0, 0.0, 1.0) * cu_prev_ref[HALO_ROWS - 1:HALO_ROWS, :]
        edge_next = jnp.where(pos == seq_tiles - 1, 0.0, 1.0) * cu_next_ref[0:1, :]
    else:
        edge_prev = jnp.zeros((1, D_CONV), _F32)
        edge_next = edge_prev
    prev = pltpu.roll(cu, 1, axis=0)
    nxt = pltpu.roll(cu, TOKEN_TILE - 1, axis=0)
    sub = lax.broadcasted_iota(jnp.int32, (HALO_ROWS, D_CONV), 0)
    prev = jnp.concatenate([jnp.where(sub == 0, edge_prev, prev[0:HALO_ROWS]), prev[HALO_ROWS:]], axis=0)
    nxt = jnp.concatenate([nxt[:TOKEN_TILE - HALO_ROWS],
                           jnp.where(sub == HALO_ROWS - 1, edge_next, nxt[TOKEN_TILE - HALO_ROWS:])], axis=0)
    conv = bg_ref[...] * (prev * convw_ref[0:1, :] + cu * convw_ref[1:2, :] + nxt * convw_ref[2:3, :])
    mixed_scr[:, D_ATTN:D_ATTN + D_CONV] = (_rms(conv) * gc_ref[...]).astype(_BF16)

    row_sum_on_mxu = latent
    stacked = 1 if latent else CONTEXT_HEADS_STACKED
    heads = []
    for kvh in range(N_KV_HEADS):
        cols = slice(kvh * HEAD_DIM, (kvh + 1) * HEAD_DIM)
        ks = [kv_refs[2 * p][:, cols].astype(_BF16) for p in range(n_kv)]
        vs = [kv_refs[2 * p + 1][:, cols].astype(_BF16) for p in range(n_kv)]
        if row_sum_on_mxu:
            vs = [jnp.concatenate([vp, jnp.ones_like(vp)], axis=1) for vp in vs]
        for g0 in range(0, GQA_GROUP, stacked):
            first = kvh * GQA_GROUP + g0
            qh = jnp.concatenate([q_ref[:, (first + i) * HEAD_DIM:(first + i + 1) * HEAD_DIM]
                                  for i in range(stacked)], axis=0)
            sc = [_dot_nt(qh, kp) for kp in ks]
            m = functools.reduce(jnp.maximum, [jnp.max(sp, axis=-1, keepdims=True) for sp in sc])
            if row_sum_on_mxu:
                ol = functools.reduce(jnp.add, [_dot(jnp.exp2(sp - m).astype(_BF16), vp) for sp, vp in zip(sc, vs)])
                o = ol[:, 0:HEAD_DIM] / ol[:, HEAD_DIM:HEAD_DIM + 1]
            else:
                p = [jnp.exp2(sp - m) for sp in sc]
                l = functools.reduce(jnp.add, [jnp.sum(pp, axis=-1, keepdims=True) for pp in p])
                o = functools.reduce(jnp.add, [_dot(pp.astype(_BF16), vp) for pp, vp in zip(p, vs)]) / l
            heads += [o[i * TOKEN_TILE:(i + 1) * TOKEN_TILE] for i in range(stacked)]
    ssq = functools.reduce(jnp.add, [jnp.sum(h * h, axis=-1, keepdims=True) for h in heads])
    inv = lax.rsqrt(ssq * (1.0 / D_ATTN) + RMS_EPS)
    for hd, h in enumerate(heads):
        hc = slice(hd * HEAD_DIM, (hd + 1) * HEAD_DIM)
        mixed_scr[:, hc] = (h * inv * ga_ref[:, hc]).astype(_BF16)


def _mix(x, q, kv_parts, bg, cu, mod, conv_w, g_attn_out, g_conv_out, w_out, g_norm2, w_router_t,
         latent, seq):
    n_tok = x.shape[0]
    seq_tiles = seq // TOKEN_TILE
    n_tiles = n_tok // TOKEN_TILE
    started = lambda i: jnp.minimum(i, n_tiles - 1)
    finished = lambda i: jnp.maximum(i - 1, 0)
    start_tile = lambda w: pl.BlockSpec((TOKEN_TILE, w), lambda i: (started(i), 0))
    finish_tile = lambda w: pl.BlockSpec((TOKEN_TILE, w), lambda i: (finished(i), 0))
    in_specs = [finish_tile(D_MODEL), start_tile(D_ATTN)]
    args = [x, q]
    for k, v in kv_parts:
        spec = pl.BlockSpec((None, k.shape[1], D_KV), lambda i: (started(i) // seq_tiles, 0, 0))
        in_specs += [spec, spec]
        args += [k, v]
    in_specs += [start_tile(D_CONV), start_tile(D_CONV)]
    args += [bg, cu]
    if latent:
        per_tile = TOKEN_TILE // HALO_ROWS
        last = n_tok // HALO_ROWS - 1
        in_specs += [
            pl.BlockSpec((HALO_ROWS, D_CONV), lambda i: (jnp.maximum(started(i) * per_tile - 1, 0), 0)),
            pl.BlockSpec((HALO_ROWS, D_CONV), lambda i: (jnp.minimum((started(i) + 1) * per_tile, last), 0)),
        ]
        args += [cu, cu]
    in_specs += [
        _resident((3, D_CONV)),
        _resident((1, D_ATTN)),
        _resident((1, D_CONV)),
        _resident((D_ATTN + D_CONV, D_MODEL)) if latent else pl.BlockSpec(memory_space=pl.ANY),
        _resident((N_MOD, COND_ROWS, D_MODEL)),
        _resident((1, D_MODEL)),
        _resident((N_EXPERTS, D_MODEL)),
    ]
    args += [conv_w, g_attn_out.reshape(1, D_ATTN), g_conv_out.reshape(1, D_CONV), w_out, mod,
             g_norm2.reshape(1, D_MODEL), w_router_t]
    out_specs = [finish_tile(D_MODEL), finish_tile(D_MODEL),
                 pl.BlockSpec((N_EXPERTS, TOKEN_TILE), lambda i: (0, finished(i)))]
    out_shape = [
        jax.ShapeDtypeStruct((n_tok, D_MODEL), _F32),
        jax.ShapeDtypeStruct((n_tok, D_MODEL), _BF16),
        jax.ShapeDtypeStruct((N_EXPERTS, n_tok), _F32),
    ]
    scratch_shapes = [pltpu.VMEM((TOKEN_TILE, D_ATTN + D_CONV), _BF16)]
    if not latent:
        out_specs += [pl.BlockSpec(memory_space=pl.ANY)]
        out_shape += [jax.ShapeDtypeStruct((D_ATTN + D_CONV, D_MODEL), _BF16)]
        scratch_shapes += _weight_scratch(D_ATTN + D_CONV, D_MODEL)
    return pl.pallas_call(
        functools.partial(_mix_kernel, latent, seq_tiles, n_tiles),
        grid=(n_tiles + 1,),
        in_specs=in_specs,
        out_specs=out_specs,
        out_shape=out_shape,
        scratch_shapes=scratch_shapes,
        compiler_params=_params(1),
        name="mix_lat" if latent else "mix_ctx",
    )(*args)


def _route_kernel(seq, cap, sets, e_blk, lg_ref, h2_ref, xs_ref, gw_ref, aff_scr, key_scr):
    j = pl.program_id(1)

    @pl.when(j == 0)
    def _():
        other = lax.broadcasted_iota(jnp.int32, (seq, seq), 0)
        token = lax.broadcasted_iota(jnp.int32, (seq, seq), 1)
        earlier = jnp.where(other < token, 1.0, 0.0)
        tie = earlier[0:LANES, 0:LANES]
        before = earlier.astype(_BF16)
        for s in range(sets):
            lg = lg_ref[:, s * seq:(s + 1) * seq]
            ex = jnp.exp(lg - jnp.max(lg, axis=0, keepdims=True))
            aff = ex / jnp.sum(ex, axis=0, keepdims=True)
            aff_t = jnp.concatenate([aff, jnp.zeros((LANES - N_EXPERTS, seq), _F32)], axis=0).T
            ranks = []
            for e in range(N_EXPERTS):
                theirs = aff_t[:, e:e + 1]
                blocks = []
                for lo in range(0, seq, LANES):
                    hi = lo + LANES
                    mine = aff[e:e + 1, lo:hi]
                    diag = theirs[lo:hi]
                    cnt = jnp.sum(jnp.where(diag > mine, 1.0, jnp.where(diag == mine, tie, 0.0)),
                                  axis=0, keepdims=True)
                    if lo > 0:
                        cnt += jnp.sum(jnp.where(theirs[0:lo] >= mine, 1.0, 0.0), axis=0, keepdims=True)
                    if hi < seq:
                        cnt += jnp.sum(jnp.where(theirs[hi:seq] > mine, 1.0, 0.0), axis=0, keepdims=True)
                    blocks.append(cnt)
                ranks.append(jnp.concatenate(blocks, axis=1))
            rank = jnp.concatenate(ranks, axis=0)
            sel = rank < float(cap)
            slot = _dot(jnp.where(sel, 1.0, 0.0).astype(_BF16), before)
            rows = slice(s * N_EXPERTS, (s + 1) * N_EXPERTS)
            aff_scr[rows, :] = aff
            key_scr[rows, :] = jnp.where(sel, slot, -1.0)

    c_idx = lax.broadcasted_iota(jnp.int32, (cap, seq), 0).astype(_F32)
    for s in range(sets):
        onehot, gated = [], []
        for i in range(e_blk):
            e = s * N_EXPERTS + j * e_blk + i
            hit = c_idx == key_scr[pl.ds(e, 1), :]
            onehot.append(jnp.where(hit, 1.0, 0.0))
            gated.append(jnp.where(hit, aff_scr[pl.ds(e, 1), :], 0.0))
        xs = _dot(jnp.concatenate(onehot, axis=0).astype(_BF16), h2_ref[s * seq:(s + 1) * seq, :]).astype(xs_ref.dtype)
        for i in range(e_blk):
            xs_ref[i, s * cap:(s + 1) * cap, :] = xs[i * cap:(i + 1) * cap, :]
        gw_ref[s] = jnp.concatenate(gated, axis=0).astype(gw_ref.dtype)


def _route(logits_t, h2, seq, sets, e_blk):
    n_tok = h2.shape[0]
    n_sets = n_tok // seq
    cap = EC_CAPACITY_FACTOR * seq // N_EXPERTS
    return pl.pallas_call(
        functools.partial(_route_kernel, seq, cap, sets, e_blk),
        grid=(n_sets // sets, N_EXPERTS // e_blk),
        in_specs=[
            pl.BlockSpec((N_EXPERTS, sets * seq), lambda s, j: (0, s)),
            pl.BlockSpec((sets * seq, D_MODEL), lambda s, j: (s, 0)),
        ],
        out_specs=[
            pl.BlockSpec((e_blk, sets * cap, D_MODEL), lambda s, j: (j, s, 0)),
            pl.BlockSpec((sets, e_blk * cap, seq), lambda s, j: (s, j, 0)),
        ],
        out_shape=[
            jax.ShapeDtypeStruct((N_EXPERTS, n_sets * cap, D_MODEL), _BF16),
            jax.ShapeDtypeStruct((n_sets, N_EXPERTS * cap, seq), _BF16),
        ],
        scratch_shapes=[pltpu.VMEM((sets * N_EXPERTS, seq), _F32), pltpu.VMEM((sets * N_EXPERTS, seq), _F32)],
        compiler_params=_params(2),
        name="route_%d" % seq,
    )(logits_t, h2)


def _ffn_kernel(xa_ref, xb_ref, wg_ref, wu_ref, wd_ref, ya_ref, yb_ref, acc_a, acc_b):
    def run(first):
        wg = wg_ref[...].astype(_BF16)
        wu = wu_ref[...].astype(_BF16)
        wd = wd_ref[...].astype(_BF16)
        for x_ref, y_ref, acc in ((xa_ref, ya_ref, acc_a), (xb_ref, yb_ref, acc_b)):
            chunk = min(FFN_ROW_CHUNK, x_ref.shape[0])
            for c in range(x_ref.shape[0] // chunk):
                rows = slice(c * chunk, (c + 1) * chunk)
                x = x_ref[rows, :]
                a = _dot(x, wg)
                u = _dot(x, wu)
                total = _dot((a * jax.nn.sigmoid(a) * u).astype(_BF16), wd)
                if not first:
                    total = acc[rows, :] + total
                acc[rows, :] = total
                y_ref[rows, :] = total.astype(y_ref.dtype)

    @pl.when(pl.program_id(1) == 0)
    def _():
        run(True)

    @pl.when(pl.program_id(1) > 0)
    def _():
        run(False)


def _ffn(xs_a, xs_b, w_gate, w_up, w_down):
    rows_a, rows_b = xs_a.shape[1], xs_b.shape[1]
    rows = lambda n: pl.BlockSpec((None, n, D_MODEL), lambda e, f: (e, 0, 0))
    return pl.pallas_call(
        _ffn_kernel,
        grid=(N_EXPERTS, D_FF // FF_TILE),
        in_specs=[
            rows(rows_a), rows(rows_b),
            pl.BlockSpec((None, D_MODEL, FF_TILE), lambda e, f: (e, 0, f)),
            pl.BlockSpec((None, D_MODEL, FF_TILE), lambda e, f: (e, 0, f)),
            pl.BlockSpec((None, FF_TILE, D_MODEL), lambda e, f: (e, f, 0)),
        ],
        out_specs=[rows(rows_a), rows(rows_b)],
        out_shape=[jax.ShapeDtypeStruct(xs_a.shape, _BF16), jax.ShapeDtypeStruct(xs_b.shape, _BF16)],
        scratch_shapes=[pltpu.VMEM((rows_a, D_MODEL), _F32), pltpu.VMEM((rows_b, D_MODEL), _F32)],
        compiler_params=_params(2),
        name="ffn",
    )(xs_a, xs_b, w_gate, w_up, w_down)


def _combine_kernel(latent, x_ref, y_ref, gw_ref, mod_ref, o_ref):
    r = 1 + pl.program_id(0) if latent else 0
    sets, rows, tile = gw_ref.shape
    cap = rows // N_EXPERTS
    gate = mod_ref[5, pl.ds(r, 1), :]
    for i in range(sets):
        y = y_ref[:, i * cap:(i + 1) * cap, :].reshape(rows, D_MODEL)
        moe = lax.dot_general(gw_ref[i], y, (((0,), (0,)), ((), ())), preferred_element_type=_F32)
        tok = slice(i * tile, (i + 1) * tile)
        o_ref[tok, :] = x_ref[tok, :] + gate * moe


def _combine(x_new, y, gw, mod, latent, seq, step_tokens):
    n_tok = x_new.shape[0]
    n_sets = n_tok // seq
    cap = EC_CAPACITY_FACTOR * seq // N_EXPERTS
    sets = max(1, step_tokens // seq)
    tile = step_tokens // sets
    per_set = seq // tile
    tokens = pl.BlockSpec((step_tokens, D_MODEL), lambda s, t: (s * per_set + t, 0))
    return pl.pallas_call(
        functools.partial(_combine_kernel, latent),
        grid=(n_sets // sets, per_set),
        in_specs=[
            tokens,
            pl.BlockSpec((N_EXPERTS, sets * cap, D_MODEL), lambda s, t: (0, s, 0)),
            pl.BlockSpec((sets, N_EXPERTS * cap, tile), lambda s, t: (s, 0, t)),
            _resident((N_MOD, COND_ROWS, D_MODEL)),
        ],
        out_specs=tokens,
        out_shape=jax.ShapeDtypeStruct((n_tok, D_MODEL), _F32),
        compiler_params=_params(2),
        name="combine_lat" if latent else "combine_ctx",
    )(x_new, y, gw, mod)


def kernel(x_prompt, x_sample, cache_k, cache_v, c, c_ctx, w_ada, b_ada, g_norm1, w_in, g_q, g_k, conv_w,
           g_attn_out, g_conv_out, w_out, g_norm2, w_router, w_gate, w_up, w_down):
    batch, seq, _ = x_prompt.shape
    dec_batch, dec_seq, _ = x_sample.shape
    depth = w_ada.shape[0]
    assert 1 + dec_batch <= COND_ROWS

    cond = jnp.concatenate([c_ctx[None, :], c, jnp.zeros((COND_ROWS - 1 - dec_batch, D_MODEL), _F32)], axis=0)
    xp = x_prompt.reshape(batch * seq, D_MODEL)
    xl = x_sample.reshape(dec_batch * dec_seq, D_MODEL)
    past = cache_k.shape[2]

    new_k, new_v = [], []
    for l in range(depth):
        mod = _ada(cond, w_ada[l], b_ada[l])
        w_router_t = w_router[l].T

        q_p, k_p, v_p, bg_p, cu_p, kstate, vstate, w_in_b = _inproj(xp, mod, g_norm1[l], w_in[l], g_q[l], g_k[l],
                                                                    False, seq)
        q_l, k_l, v_l, bg_l, cu_l = _inproj(xl, mod, g_norm1[l], w_in_b, g_q[l], g_k[l], True, dec_seq)

        shared = lambda w: (mod, conv_w[l], g_attn_out[l], g_conv_out[l], w, g_norm2[l], w_router_t)
        kv_p = [(k_p.reshape(batch, seq, D_KV), v_p.reshape(batch, seq, D_KV))]
        kv_l = [(cache_k[:, l].reshape(dec_batch, past, D_KV), cache_v[:, l].reshape(dec_batch, past, D_KV)),
                (k_l.reshape(dec_batch, dec_seq, D_KV), v_l.reshape(dec_batch, dec_seq, D_KV))]
        xp1, h2_p, lg_p, w_out_b = _mix(xp, q_p, kv_p, bg_p, cu_p, *shared(w_out[l]), False, seq)
        xl1, h2_l, lg_l = _mix(xl, q_l, kv_l, bg_l, cu_l, *shared(w_out_b), True, dec_seq)

        xs_p, gw_p = _route(lg_p, h2_p, seq, CONTEXT_ROUTE_SETS, N_EXPERTS)
        xs_l, gw_l = _route(lg_l, h2_l, dec_seq, 1, LATENT_ROUTE_EXPERTS)
        y_p, y_l = _ffn(xs_p, xs_l, w_gate[l], w_up[l], w_down[l])
        xp = _combine(xp1, y_p, gw_p, mod, False, seq, CONTEXT_COMBINE_TOKENS)
        xl = _combine(xl1, y_l, gw_l, mod, True, dec_seq, LATENT_COMBINE_TOKENS)

        new_k.append(kstate.reshape(batch, seq, N_KV_HEADS, HEAD_DIM))
        new_v.append(vstate.reshape(batch, seq, N_KV_HEADS, HEAD_DIM))

    return (xp.reshape(batch, seq, D_MODEL), xl.reshape(dec_batch, dec_seq, D_MODEL),
            jnp.stack(new_k, axis=1), jnp.stack(new_v, axis=1))
```

```python
import functools

import jax
import jax.numpy as jnp
from jax import lax
from jax.experimental import pallas as pl
from jax.experimental.pallas import tpu as pltpu

D_MODEL = 2048
GRID_W = 64
HEAD_DIM = 128
D_ATTN = 1024
D_CONV = 1024
N_HEADS = 8
N_KV_HEADS = 2
GQA_GROUP = N_HEADS // N_KV_HEADS
D_KV = N_KV_HEADS * HEAD_DIM
N_EXPERTS = 16
EC_CAPACITY_FACTOR = 2
D_FF = 1024
ROPE_THETA = 10000.0
RMS_EPS = 1e-6
LOG2_E = 1.4426950408889634
N_MOD = 6
D_IN_PROJ = D_ATTN + 2 * D_KV + 3 * D_CONV

TOKEN_TILE = 256
INPROJ_TILE = 512
ADA_COL_TILE = 1024
FF_TILE = 512
FFN_ROW_CHUNK = 512
COND_ROWS = 8
HALO_ROWS = 8
LATENT_ROUTE_EXPERTS = 8
CONTEXT_ROUTE_SETS = 4
CONTEXT_HEADS_STACKED = 2
VMEM_LIMIT = 56 * 1024 * 1024
LANES = 128
STAGE_ROWS = 32
STAGE_SLOTS = 8
CONTEXT_COMBINE_TOKENS = 1024
LATENT_COMBINE_TOKENS = 512

_BF16 = jnp.bfloat16
_F32 = jnp.float32


def _params(n_axes):
    return pltpu.CompilerParams(dimension_semantics=("arbitrary",) * n_axes,
                                vmem_limit_bytes=VMEM_LIMIT)


def _resident(shape):
    return pl.BlockSpec(shape, lambda *_: (0,) * len(shape), pipeline_mode=pl.Buffered(1))


def _dot(a, b):
    return jnp.dot(a, b, preferred_element_type=_F32)


def _dot_nt(a, b):
    return lax.dot_general(a, b, (((1,), (1,)), ((), ())), preferred_element_type=_F32)


def _rms(x):
    return x * lax.rsqrt(jnp.mean(x * x, axis=-1, keepdims=True) + RMS_EPS)


def _split_bf16(x):
    hi = x.astype(_BF16)
    lo = (x - hi.astype(_F32)).astype(_BF16)
    return hi, lo


def _weight_scratch(rows, cols):
    return [pltpu.VMEM((rows, cols), _BF16), pltpu.VMEM((STAGE_SLOTS, STAGE_ROWS, cols), _F32),
            pltpu.SemaphoreType.DMA((STAGE_SLOTS,)), pltpu.SemaphoreType.DMA(())]


def _cast_weight_once(w_hbm, w_pub, w_scr, stage, sems, pub_sem, step, n_steps):
    n_chunks = w_scr.shape[0] // STAGE_ROWS

    def fetch(c):
        slot = c % STAGE_SLOTS
        return pltpu.make_async_copy(w_hbm.at[pl.ds(c * STAGE_ROWS, STAGE_ROWS), :], stage.at[slot], sems.at[slot])

    publish = pltpu.make_async_copy(w_scr, w_pub, pub_sem)

    @pl.when(step == 0)
    def _():
        for c in range(STAGE_SLOTS):
            fetch(c).start()
        for c in range(n_chunks):
            fetch(c).wait()
            w_scr[c * STAGE_ROWS:(c + 1) * STAGE_ROWS, :] = stage[c % STAGE_SLOTS].astype(_BF16)
            if c + STAGE_SLOTS < n_chunks:
                fetch(c + STAGE_SLOTS).start()
        publish.start()

    @pl.when(step == n_steps - 1)
    def _():
        publish.wait()


def _ada_kernel(c_ref, w_ref, b_ref, o_ref):
    c = c_ref[...]
    a = c * jax.nn.sigmoid(c)
    a_hi, a_lo = _split_bf16(a)
    w = w_ref[...]
    w_hi, w_lo = _split_bf16(w)
    o_ref[...] = _dot(a_hi, w_hi) + (_dot(a_lo, w_hi) + _dot(a_hi, w_lo)) + b_ref[...]


def _ada(cond, w_ada, b_ada):
    per_mod = D_MODEL // ADA_COL_TILE
    return pl.pallas_call(
        _ada_kernel,
        grid=(N_MOD, per_mod),
        in_specs=[
            pl.BlockSpec((COND_ROWS, D_MODEL), lambda j, h: (0, 0)),
            pl.BlockSpec((D_MODEL, ADA_COL_TILE), lambda j, h: (0, j * per_mod + h)),
            pl.BlockSpec((1, ADA_COL_TILE), lambda j, h: (0, j * per_mod + h)),
        ],
        out_specs=pl.BlockSpec((None, COND_ROWS, ADA_COL_TILE), lambda j, h: (j, 0, h)),
        out_shape=jax.ShapeDtypeStruct((N_MOD, COND_ROWS, D_MODEL), _F32),
        compiler_params=_params(2),
        name="ada",
    )(cond, w_ada, b_ada.reshape(1, N_MOD * D_MODEL))


def _rope_tables(seq):
    t = jnp.arange(seq)
    row = (t // GRID_W).astype(_F32)
    col = (t % GRID_W).astype(_F32)
    half = HEAD_DIM // 2
    inv_freq = ROPE_THETA ** (-jnp.arange(0, half, 2, dtype=_F32) / half)
    ang_r = row[:, None] * inv_freq[None, :]
    ang_c = col[:, None] * inv_freq[None, :]
    cos = jnp.concatenate([jnp.cos(ang_r)] * 2 + [jnp.cos(ang_c)] * 2, axis=-1)
    sin = jnp.concatenate([-jnp.sin(ang_r), jnp.sin(ang_r), -jnp.sin(ang_c), jnp.sin(ang_c)], axis=-1)
    return cos, sin


def _cond_row(latent, seq_tiles):
    if not latent:
        return 0
    return 1 + pl.program_id(0) // seq_tiles


def _inproj_kernel(latent, seq_tiles, x_ref, mod_ref, g1_ref, w_ref, gq_ref, gk_ref, *rest):
    if latent:
        cos_ref, sin_ref, q_ref, k_ref, v_ref, bg_ref, cu_ref = rest
    else:
        q_ref, k_ref, v_ref, bg_ref, cu_ref, kstate_ref, vstate_ref, w_pub, *cast_scratch = rest
        _cast_weight_once(w_ref, w_pub, *cast_scratch, pl.program_id(0), pl.num_programs(0))
        w_ref = cast_scratch[0]
    r = _cond_row(latent, seq_tiles)
    sh = mod_ref[0, pl.ds(r, 1), :]
    sc = mod_ref[1, pl.ds(r, 1), :]
    h = _rms(x_ref[...]) * (g1_ref[...] * (1.0 + sc)) + sh
    hb = h.astype(_BF16)

    if latent:
        cos = cos_ref[...]
        sin = sin_ref[...]
        lane = lax.broadcasted_iota(jnp.int32, cos.shape, 1)
        first = (lane % (HEAD_DIM // 2)) < (HEAD_DIM // 4)

    def head_norm(xh, g):
        xh = _rms(xh) * g
        if latent:
            partner = jnp.where(first, pltpu.roll(xh, HEAD_DIM - HEAD_DIM // 4, axis=1),
                                pltpu.roll(xh, HEAD_DIM // 4, axis=1))
            xh = xh * cos + partner * sin
        return xh

    gq = gq_ref[...] * (HEAD_DIM ** -0.5 * LOG2_E)
    q = _dot(hb, w_ref[:, 0:D_ATTN])
    for hd in range(N_HEADS):
        cols = slice(hd * HEAD_DIM, (hd + 1) * HEAD_DIM)
        q_ref[:, cols] = head_norm(q[:, cols], gq).astype(q_ref.dtype)
    k = _dot(hb, w_ref[:, D_ATTN:D_ATTN + D_KV])
    o = D_ATTN + D_KV
    v = _dot(hb, w_ref[:, o:o + D_KV])
    for hd in range(N_KV_HEADS):
        cols = slice(hd * HEAD_DIM, (hd + 1) * HEAD_DIM)
        kh = head_norm(k[:, cols], gk_ref[...])
        k_ref[:, cols] = kh
        v_ref[:, cols] = v[:, cols]
        if not latent:
            kstate_ref[:, hd, :] = kh
            vstate_ref[:, hd, :] = v[:, cols]
    o += D_KV
    bg_ref[...] = _dot(hb, w_ref[:, o:o + D_CONV])
    o += D_CONV
    cg = _dot(hb, w_ref[:, o:o + D_CONV])
    o += D_CONV
    cu_ref[...] = cg * _dot(hb, w_ref[:, o:o + D_CONV])


def _inproj(x, mod, g_norm1, w_in, g_q, g_k, latent, seq):
    n_tok = x.shape[0]
    seq_tiles = seq // INPROJ_TILE
    tile = lambda w: pl.BlockSpec((INPROJ_TILE, w), lambda i: (i, 0))
    in_specs = [
        tile(D_MODEL),
        _resident((N_MOD, COND_ROWS, D_MODEL)),
        _resident((1, D_MODEL)),
        _resident((D_MODEL, D_IN_PROJ)) if latent else pl.BlockSpec(memory_space=pl.ANY),
        _resident((1, HEAD_DIM)),
        _resident((1, HEAD_DIM)),
    ]
    args = [x, mod, g_norm1.reshape(1, D_MODEL), w_in, g_q.reshape(1, HEAD_DIM), g_k.reshape(1, HEAD_DIM)]
    if latent:
        cos, sin = _rope_tables(seq)
        rope_spec = pl.BlockSpec((INPROJ_TILE, HEAD_DIM), lambda i: (i % seq_tiles, 0))
        in_specs += [rope_spec, rope_spec]
        args += [cos, sin]
    out_specs = [tile(D_ATTN), tile(D_KV), tile(D_KV), tile(D_CONV), tile(D_CONV)]
    out_shape = [
        jax.ShapeDtypeStruct((n_tok, D_ATTN), _BF16),
        jax.ShapeDtypeStruct((n_tok, D_KV), _F32),
        jax.ShapeDtypeStruct((n_tok, D_KV), _F32),
        jax.ShapeDtypeStruct((n_tok, D_CONV), _F32),
        jax.ShapeDtypeStruct((n_tok, D_CONV), _F32),
    ]
    if not latent:
        state_spec = pl.BlockSpec((INPROJ_TILE, N_KV_HEADS, HEAD_DIM), lambda i: (i, 0, 0))
        out_specs += [state_spec, state_spec, pl.BlockSpec(memory_space=pl.ANY)]
        out_shape += [jax.ShapeDtypeStruct((n_tok, N_KV_HEADS, HEAD_DIM), _F32)] * 2
        out_shape += [jax.ShapeDtypeStruct((D_MODEL, D_IN_PROJ), _BF16)]
    return pl.pallas_call(
        functools.partial(_inproj_kernel, latent, seq_tiles),
        grid=(n_tok // INPROJ_TILE,),
        in_specs=in_specs,
        out_specs=out_specs,
        out_shape=out_shape,
        scratch_shapes=[] if latent else _weight_scratch(D_MODEL, D_IN_PROJ),
        compiler_params=_params(1),
        name="inproj_lat" if latent else "inproj_ctx",
    )(*args)


def _mix_kernel(latent, seq_tiles, n_tiles, x_ref, q_ref, *rest):
    n_kv = 2 if latent else 1
    kv_refs = rest[:2 * n_kv]
    rest = rest[2 * n_kv:]
    bg_ref, cu_ref = rest[:2]
    rest = rest[2:]
    if latent:
        cu_prev_ref, cu_next_ref = rest[:2]
        rest = rest[2:]
    (convw_ref, ga_ref, gc_ref, wout_ref, mod_ref, g2_ref, wr_ref, xo_ref, h2_ref, lg_ref, *rest) = rest
    if latent:
        (mixed_scr,) = rest
    else:
        w_pub, mixed_scr, *cast_scratch = rest
        _cast_weight_once(wout_ref, w_pub, *cast_scratch, pl.program_id(0), pl.num_programs(0))
        wout_ref = cast_scratch[0]

    s = pl.program_id(0)

    @pl.when(s == 0)
    def _():
        mixed_scr[...] = jnp.zeros_like(mixed_scr)

    r = 1 + jnp.maximum(s - 1, 0) // seq_tiles if latent else 0
    y = _dot(mixed_scr[...], wout_ref[...])
    x_new = x_ref[...] + mod_ref[2, pl.ds(r, 1), :] * y
    xo_ref[...] = x_new
    h2 = _rms(x_new) * (g2_ref[...] * (1.0 + mod_ref[4, pl.ds(r, 1), :])) + mod_ref[3, pl.ds(r, 1), :]
    h2_hi = h2.astype(_BF16)
    h2_ref[...] = h2_hi
    lg_ref[...] = _dot_nt(wr_ref[...].astype(_BF16), h2_hi)

    cu = cu_ref[...]
    if latent:
        pos = jnp.minimum(s, n_tiles - 1) % seq_tiles
        edge_prev = jnp.where(pos == 0, 0.0, 1.0) * cu_prev_ref[HALO_ROWS - 1:HALO_ROWS, :]
        edge_next = jnp.where(pos == seq_tiles - 1, 0.0, 1.0) * cu_next_ref[0:1, :]
    else:
        edge_prev = jnp.zeros((1, D_CONV), _F32)
        edge_next = edge_prev
    prev = pltpu.roll(cu, 1, axis=0)
    nxt = pltpu.roll(cu, TOKEN_TILE - 1, axis=0)
    sub = lax.broadcasted_iota(jnp.int32, (HALO_ROWS, D_CONV), 0)
    prev = jnp.concatenate([jnp.where(sub == 0, edge_prev, prev[0:HALO_ROWS]), prev[HALO_ROWS:]], axis=0)
    nxt = jnp.concatenate([nxt[:TOKEN_TILE - HALO_ROWS],
                           jnp.where(sub == HALO_ROWS - 1, edge_next, nxt[TOKEN_TILE - HALO_ROWS:])], axis=0)
    conv = bg_ref[...] * (prev * convw_ref[0:1, :] + cu * convw_ref[1:2, :] + nxt * convw_ref[2:3, :])
    mixed_scr[:, D_ATTN:D_ATTN + D_CONV] = (_rms(conv) * gc_ref[...]).astype(_BF16)

    row_sum_on_mxu = latent
    stacked = 1 if latent else CONTEXT_HEADS_STACKED
    heads = []
    for kvh in range(N_KV_HEADS):
        cols = slice(kvh * HEAD_DIM, (kvh + 1) * HEAD_DIM)
        ks = [kv_refs[2 * p][:, cols].astype(_BF16) for p in range(n_kv)]
        vs = [kv_refs[2 * p + 1][:, cols].astype(_BF16) for p in range(n_kv)]
        if row_sum_on_mxu:
            vs = [jnp.concatenate([vp, jnp.ones_like(vp)], axis=1) for vp in vs]
        for g0 in range(0, GQA_GROUP, stacked):
            first = kvh * GQA_GROUP + g0
            qh = jnp.concatenate([q_ref[:, (first + i) * HEAD_DIM:(first + i + 1) * HEAD_DIM]
                                  for i in range(stacked)], axis=0)
            sc = [_dot_nt(qh, kp) for kp in ks]
            m = functools.reduce(jnp.maximum, [jnp.max(sp, axis=-1, keepdims=True) for sp in sc])
            if row_sum_on_mxu:
                ol = functools.reduce(jnp.add, [_dot(jnp.exp2(sp - m).astype(_BF16), vp) for sp, vp in zip(sc, vs)])
                o = ol[:, 0:HEAD_DIM] / ol[:, HEAD_DIM:HEAD_DIM + 1]
            else:
                p = [jnp.exp2(sp - m) for sp in sc]
                l = functools.reduce(jnp.add, [jnp.sum(pp, axis=-1, keepdims=True) for pp in p])
                o = functools.reduce(jnp.add, [_dot(pp.astype(_BF16), vp) for pp, vp in zip(p, vs)]) / l
            heads += [o[i * TOKEN_TILE:(i + 1) * TOKEN_TILE] for i in range(stacked)]
    ssq = functools.reduce(jnp.add, [jnp.sum(h * h, axis=-1, keepdims=True) for h in heads])
    inv = lax.rsqrt(ssq * (1.0 / D_ATTN) + RMS_EPS)
    for hd, h in enumerate(heads):
        hc = slice(hd * HEAD_DIM, (hd + 1) * HEAD_DIM)
        mixed_scr[:, hc] = (h * inv * ga_ref[:, hc]).astype(_BF16)


def _mix(x, q, kv_parts, bg, cu, mod, conv_w, g_attn_out, g_conv_out, w_out, g_norm2, w_router_t,
         latent, seq):
    n_tok = x.shape[0]
    seq_tiles = seq // TOKEN_TILE
    n_tiles = n_tok // TOKEN_TILE
    started = lambda i: jnp.minimum(i, n_tiles - 1)
    finished = lambda i: jnp.maximum(i - 1, 0)
    start_tile = lambda w: pl.BlockSpec((TOKEN_TILE, w), lambda i: (started(i), 0))
    finish_tile = lambda w: pl.BlockSpec((TOKEN_TILE, w), lambda i: (finished(i), 0))
    in_specs = [finish_tile(D_MODEL), start_tile(D_ATTN)]
    args = [x, q]
    for k, v in kv_parts:
        spec = pl.BlockSpec((None, k.shape[1], D_KV), lambda i: (started(i) // seq_tiles, 0, 0))
        in_specs += [spec, spec]
        args += [k, v]
    in_specs += [start_tile(D_CONV), start_tile(D_CONV)]
    args += [bg, cu]
    if latent:
        per_tile = TOKEN_TILE // HALO_ROWS
        last = n_tok // HALO_ROWS - 1
        in_specs += [
            pl.BlockSpec((HALO_ROWS, D_CONV), lambda i: (jnp.maximum(started(i) * per_tile - 1, 0), 0)),
            pl.BlockSpec((HALO_ROWS, D_CONV), lambda i: (jnp.minimum((started(i) + 1) * per_tile, last), 0)),
        ]
        args += [cu, cu]
    in_specs += [
        _resident((3, D_CONV)),
        _resident((1, D_ATTN)),
        _resident((1, D_CONV)),
        _resident((D_ATTN + D_CONV, D_MODEL)) if latent else pl.BlockSpec(memory_space=pl.ANY),
        _resident((N_MOD, COND_ROWS, D_MODEL)),
        _resident((1, D_MODEL)),
        _resident((N_EXPERTS, D_MODEL)),
    ]
    args += [conv_w, g_attn_out.reshape(1, D_ATTN), g_conv_out.reshape(1, D_CONV), w_out, mod,
             g_norm2.reshape(1, D_MODEL), w_router_t]
    out_specs = [finish_tile(D_MODEL), finish_tile(D_MODEL),
                 pl.BlockSpec((N_EXPERTS, TOKEN_TILE), lambda i: (0, finished(i)))]
    out_shape = [
        jax.ShapeDtypeStruct((n_tok, D_MODEL), _F32),
        jax.ShapeDtypeStruct((n_tok, D_MODEL), _BF16),
        jax.ShapeDtypeStruct((N_EXPERTS, n_tok), _F32),
    ]
    scratch_shapes = [pltpu.VMEM((TOKEN_TILE, D_ATTN + D_CONV), _BF16)]
    if not latent:
        out_specs += [pl.BlockSpec(memory_space=pl.ANY)]
        out_shape += [jax.ShapeDtypeStruct((D_ATTN + D_CONV, D_MODEL), _BF16)]
        scratch_shapes += _weight_scratch(D_ATTN + D_CONV, D_MODEL)
    return pl.pallas_call(
        functools.partial(_mix_kernel, latent, seq_tiles, n_tiles),
        grid=(n_tiles + 1,),
        in_specs=in_specs,
        out_specs=out_specs,
        out_shape=out_shape,
        scratch_shapes=scratch_shapes,
        compiler_params=_params(1),
        name="mix_lat" if latent else "mix_ctx",
    )(*args)


def _route_kernel(seq, cap, sets, e_blk, lg_ref, h2_ref, xs_ref, gw_ref, aff_scr, key_scr):
    j = pl.program_id(1)

    @pl.when(j == 0)
    def _():
        other = lax.broadcasted_iota(jnp.int32, (seq, seq), 0)
        token = lax.broadcasted_iota(jnp.int32, (seq, seq), 1)
        earlier = jnp.where(other < token, 1.0, 0.0)
        tie = earlier[0:LANES, 0:LANES]
        before = earlier.astype(_BF16)
        for s in range(sets):
            lg = lg_ref[:, s * seq:(s + 1) * seq]
            ex = jnp.exp(lg - jnp.max(lg, axis=0, keepdims=True))
            aff = ex / jnp.sum(ex, axis=0, keepdims=True)
            aff_t = jnp.concatenate([aff, jnp.zeros((LANES - N_EXPERTS, seq), _F32)], axis=0).T
            ranks = []
            for e in range(N_EXPERTS):
                theirs = aff_t[:, e:e + 1]
                blocks = []
                for lo in range(0, seq, LANES):
                    hi = lo + LANES
                    mine = aff[e:e + 1, lo:hi]
                    diag = theirs[lo:hi]
                    cnt = jnp.sum(jnp.where(diag > mine, 1.0, jnp.where(diag == mine, tie, 0.0)),
                                  axis=0, keepdims=True)
                    if lo > 0:
                        cnt += jnp.sum(jnp.where(theirs[0:lo] >= mine, 1.0, 0.0), axis=0, keepdims=True)
                    if hi < seq:
                        cnt += jnp.sum(jnp.where(theirs[hi:seq] > mine, 1.0, 0.0), axis=0, keepdims=True)
                    blocks.append(cnt)
                ranks.append(jnp.concatenate(blocks, axis=1))
            rank = jnp.concatenate(ranks, axis=0)
            sel = rank < float(cap)
            slot = _dot(jnp.where(sel, 1.0, 0.0).astype(_BF16), before)
            rows = slice(s * N_EXPERTS, (s + 1) * N_EXPERTS)
            aff_scr[rows, :] = aff
            key_scr[rows, :] = jnp.where(sel, slot, -1.0)

    c_idx = lax.broadcasted_iota(jnp.int32, (cap, seq), 0).astype(_F32)
    for s in range(sets):
        onehot, gated = [], []
        for i in range(e_blk):
            e = s * N_EXPERTS + j * e_blk + i
            hit = c_idx == key_scr[pl.ds(e, 1), :]
            onehot.append(jnp.where(hit, 1.0, 0.0))
            gated.append(jnp.where(hit, aff_scr[pl.ds(e, 1), :], 0.0))
        xs = _dot(jnp.concatenate(onehot, axis=0).astype(_BF16), h2_ref[s * seq:(s + 1) * seq, :]).astype(xs_ref.dtype)
        for i in range(e_blk):
            xs_ref[i, s * cap:(s + 1) * cap, :] = xs[i * cap:(i + 1) * cap, :]
        gw_ref[s] = jnp.concatenate(gated, axis=0).astype(gw_ref.dtype)


def _route(logits_t, h2, seq, sets, e_blk):
    n_tok = h2.shape[0]
    n_sets = n_tok // seq
    cap = EC_CAPACITY_FACTOR * seq // N_EXPERTS
    return pl.pallas_call(
        functools.partial(_route_kernel, seq, cap, sets, e_blk),
        grid=(n_sets // sets, N_EXPERTS // e_blk),
        in_specs=[
            pl.BlockSpec((N_EXPERTS, sets * seq), lambda s, j: (0, s)),
            pl.BlockSpec((sets * seq, D_MODEL), lambda s, j: (s, 0)),
        ],
        out_specs=[
            pl.BlockSpec((e_blk, sets * cap, D_MODEL), lambda s, j: (j, s, 0)),
            pl.BlockSpec((sets, e_blk * cap, seq), lambda s, j: (s, j, 0)),
        ],
        out_shape=[
            jax.ShapeDtypeStruct((N_EXPERTS, n_sets * cap, D_MODEL), _BF16),
            jax.ShapeDtypeStruct((n_sets, N_EXPERTS * cap, seq), _BF16),
        ],
        scratch_shapes=[pltpu.VMEM((sets * N_EXPERTS, seq), _F32), pltpu.VMEM((sets * N_EXPERTS, seq), _F32)],
        compiler_params=_params(2),
        name="route_%d" % seq,
    )(logits_t, h2)


def _ffn_kernel(xa_ref, xb_ref, wg_ref, wu_ref, wd_ref, ya_ref, yb_ref):
    def run(first):
        wg = wg_ref[...].astype(_BF16)
        wu = wu_ref[...].astype(_BF16)
        wd = wd_ref[...].astype(_BF16)
        for x_ref, y_ref in ((xa_ref, ya_ref), (xb_ref, yb_ref)):
            chunk = min(FFN_ROW_CHUNK, x_ref.shape[0])
            for c in range(x_ref.shape[0] // chunk):
                rows = slice(c * chunk, (c + 1) * chunk)
                x = x_ref[rows, :]
                a = _dot(x, wg)
                u = _dot(x, wu)
                total = _dot((a * jax.nn.sigmoid(a) * u).astype(_BF16), wd)
                if not first:
                    total = y_ref[rows, :].astype(_F32) + total
                y_ref[rows, :] = total.astype(y_ref.dtype)

    @pl.when(pl.program_id(1) == 0)
    def _():
        run(True)

    @pl.when(pl.program_id(1) > 0)
    def _():
        run(False)


def _ffn(xs_a, xs_b, w_gate, w_up, w_down):
    rows_a, rows_b = xs_a.shape[1], xs_b.shape[1]
    rows = lambda n: pl.BlockSpec((None, n, D_MODEL), lambda e, f: (e, 0, 0))
    return pl.pallas_call(
        _ffn_kernel,
        grid=(N_EXPERTS, D_FF // FF_TILE),
        in_specs=[
            rows(rows_a), rows(rows_b),
            pl.BlockSpec((None, D_MODEL, FF_TILE), lambda e, f: (e, 0, f)),
            pl.BlockSpec((None, D_MODEL, FF_TILE), lambda e, f: (e, 0, f)),
            pl.BlockSpec((None, FF_TILE, D_MODEL), lambda e, f: (e, f, 0)),
        ],
        out_specs=[rows(rows_a), rows(rows_b)],
        out_shape=[jax.ShapeDtypeStruct(xs_a.shape, _BF16), jax.ShapeDtypeStruct(xs_b.shape, _BF16)],
        compiler_params=_params(2),
        name="ffn",
    )(xs_a, xs_b, w_gate, w_up, w_down)


def _combine_kernel(latent, x_ref, y_ref, gw_ref, mod_ref, o_ref):
    r = 1 + pl.program_id(0) if latent else 0
    sets, rows, tile = gw_ref.shape
    cap = rows // N_EXPERTS
    gate = mod_ref[5, pl.ds(r, 1), :]
    for i in range(sets):
        y = y_ref[:, i * cap:(i + 1) * cap, :].reshape(rows, D_MODEL)
        moe = lax.dot_general(gw_ref[i], y, (((0,), (0,)), ((), ())), preferred_element_type=_F32)
        tok = slice(i * tile, (i + 1) * tile)
        o_ref[tok, :] = x_ref[tok, :] + gate * moe


def _combine(x_new, y, gw, mod, latent, seq, step_tokens):
    n_tok = x_new.shape[0]
    n_sets = n_tok // seq
    cap = EC_CAPACITY_FACTOR * seq // N_EXPERTS
    sets = max(1, step_tokens // seq)
    tile = step_tokens // sets
    per_set = seq // tile
    tokens = pl.BlockSpec((step_tokens, D_MODEL), lambda s, t: (s * per_set + t, 0))
    return pl.pallas_call(
        functools.partial(_combine_kernel, latent),
        grid=(n_sets // sets, per_set),
        in_specs=[
            tokens,
            pl.BlockSpec((N_EXPERTS, sets * cap, D_MODEL), lambda s, t: (0, s, 0)),
            pl.BlockSpec((sets, N_EXPERTS * cap, tile), lambda s, t: (s, 0, t)),
            _resident((N_MOD, COND_ROWS, D_MODEL)),
        ],
        out_specs=tokens,
        out_shape=jax.ShapeDtypeStruct((n_tok, D_MODEL), _F32),
        compiler_params=_params(2),
        name="combine_lat" if latent else "combine_ctx",
    )(x_new, y, gw, mod)


def kernel(x_prompt, x_sample, cache_k, cache_v, c, c_ctx, w_ada, b_ada, g_norm1, w_in, g_q, g_k, conv_w,
           g_attn_out, g_conv_out, w_out, g_norm2, w_router, w_gate, w_up, w_down):
    batch, seq, _ = x_prompt.shape
    dec_batch, dec_seq, _ = x_sample.shape
    depth = w_ada.shape[0]
    assert 1 + dec_batch <= COND_ROWS

    cond = jnp.concatenate([c_ctx[None, :], c, jnp.zeros((COND_ROWS - 1 - dec_batch, D_MODEL), _F32)], axis=0)
    xp = x_prompt.reshape(batch * seq, D_MODEL)
    xl = x_sample.reshape(dec_batch * dec_seq, D_MODEL)
    past = cache_k.shape[2]

    new_k, new_v = [], []
    for l in range(depth):
        mod = _ada(cond, w_ada[l], b_ada[l])
        w_router_t = w_router[l].T

        q_p, k_p, v_p, bg_p, cu_p, kstate, vstate, w_in_b = _inproj(xp, mod, g_norm1[l], w_in[l], g_q[l], g_k[l],
                                                                    False, seq)
        q_l, k_l, v_l, bg_l, cu_l = _inproj(xl, mod, g_norm1[l], w_in_b, g_q[l], g_k[l], True, dec_seq)

        shared = lambda w: (mod, conv_w[l], g_attn_out[l], g_conv_out[l], w, g_norm2[l], w_router_t)
        kv_p = [(k_p.reshape(batch, seq, D_KV), v_p.reshape(batch, seq, D_KV))]
        kv_l = [(cache_k[:, l].reshape(dec_batch, past, D_KV), cache_v[:, l].reshape(dec_batch, past, D_KV)),
                (k_l.reshape(dec_batch, dec_seq, D_KV), v_l.reshape(dec_batch, dec_seq, D_KV))]
        xp1, h2_p, lg_p, w_out_b = _mix(xp, q_p, kv_p, bg_p, cu_p, *shared(w_out[l]), False, seq)
        xl1, h2_l, lg_l = _mix(xl, q_l, kv_l, bg_l, cu_l, *shared(w_out_b), True, dec_seq)

        xs_p, gw_p = _route(lg_p, h2_p, seq, CONTEXT_ROUTE_SETS, N_EXPERTS)
        xs_l, gw_l = _route(lg_l, h2_l, dec_seq, 1, LATENT_ROUTE_EXPERTS)
        y_p, y_l = _ffn(xs_p, xs_l, w_gate[l], w_up[l], w_down[l])
        xp = _combine(xp1, y_p, gw_p, mod, False, seq, CONTEXT_COMBINE_TOKENS)
        xl = _combine(xl1, y_l, gw_l, mod, True, dec_seq, LATENT_COMBINE_TOKENS)

        new_k.append(kstate.reshape(batch, seq, N_KV_HEADS, HEAD_DIM))
        new_v.append(vstate.reshape(batch, seq, N_KV_HEADS, HEAD_DIM))

    return (xp.reshape(batch, seq, D_MODEL), xl.reshape(dec_batch, dec_seq, D_MODEL),
            jnp.stack(new_k, axis=1), jnp.stack(new_v, axis=1))
```

```python
import functools

import jax
import jax.numpy as jnp
from jax import lax
from jax.experimental import pallas as pl
from jax.experimental.pallas import tpu as pltpu

D_MODEL = 2048
GRID_W = 64
HEAD_DIM = 128
D_ATTN = 1024
D_CONV = 1024
N_HEADS = 8
N_KV_HEADS = 2
GQA_GROUP = N_HEADS // N_KV_HEADS
D_KV = N_KV_HEADS * HEAD_DIM
N_EXPERTS = 16
EC_CAPACITY_FACTOR = 2
D_FF = 1024
ROPE_THETA = 10000.0
RMS_EPS = 1e-6
LOG2_E = 1.4426950408889634
N_MOD = 6
D_IN_PROJ = D_ATTN + 2 * D_KV + 3 * D_CONV

CONTEXT_MIX_TOKENS = 512
LATENT_MIX_TOKENS = 256
INPROJ_TILE = 512
ADA_COL_TILE = 1024
FF_TILE = 512
FFN_ROW_CHUNK = 512
COND_ROWS = 8
HALO_ROWS = 8
LATENT_ROUTE_EXPERTS = 8
CONTEXT_ROUTE_SETS = 4
CONTEXT_HEADS_STACKED = 2
VMEM_LIMIT = 56 * 1024 * 1024
LANES = 128
STAGE_ROWS = 32
STAGE_SLOTS = 8
CONTEXT_COMBINE_TOKENS = 1024
LATENT_COMBINE_TOKENS = 512

_BF16 = jnp.bfloat16
_F32 = jnp.float32


def _params(n_axes):
    return pltpu.CompilerParams(dimension_semantics=("arbitrary",) * n_axes,
                                vmem_limit_bytes=VMEM_LIMIT)


def _resident(shape):
    return pl.BlockSpec(shape, lambda *_: (0,) * len(shape), pipeline_mode=pl.Buffered(1))


def _dot(a, b):
    return jnp.dot(a, b, preferred_element_type=_F32)


def _dot_nt(a, b):
    return lax.dot_general(a, b, (((1,), (1,)), ((), ())), preferred_element_type=_F32)


def _rms(x):
    return x * lax.rsqrt(jnp.mean(x * x, axis=-1, keepdims=True) + RMS_EPS)


def _split_bf16(x):
    hi = x.astype(_BF16)
    lo = (x - hi.astype(_F32)).astype(_BF16)
    return hi, lo


def _weight_scratch(rows, cols):
    return [pltpu.VMEM((rows, cols), _BF16), pltpu.VMEM((STAGE_SLOTS, STAGE_ROWS, cols), _F32),
            pltpu.SemaphoreType.DMA((STAGE_SLOTS,)), pltpu.SemaphoreType.DMA(())]


def _cast_weight_once(w_hbm, w_pub, w_scr, stage, sems, pub_sem, step, n_steps):
    n_chunks = w_scr.shape[0] // STAGE_ROWS

    def fetch(c):
        slot = c % STAGE_SLOTS
        return pltpu.make_async_copy(w_hbm.at[pl.ds(c * STAGE_ROWS, STAGE_ROWS), :], stage.at[slot], sems.at[slot])

    publish = pltpu.make_async_copy(w_scr, w_pub, pub_sem)

    @pl.when(step == 0)
    def _():
        for c in range(STAGE_SLOTS):
            fetch(c).start()
        for c in range(n_chunks):
            fetch(c).wait()
            w_scr[c * STAGE_ROWS:(c + 1) * STAGE_ROWS, :] = stage[c % STAGE_SLOTS].astype(_BF16)
            if c + STAGE_SLOTS < n_chunks:
                fetch(c + STAGE_SLOTS).start()
        publish.start()

    @pl.when(step == n_steps - 1)
    def _():
        publish.wait()


def _ada_kernel(c_ref, w_ref, b_ref, o_ref):
    c = c_ref[...]
    a = c * jax.nn.sigmoid(c)
    a_hi, a_lo = _split_bf16(a)
    w = w_ref[...]
    w_hi, w_lo = _split_bf16(w)
    o_ref[...] = _dot(a_hi, w_hi) + (_dot(a_lo, w_hi) + _dot(a_hi, w_lo)) + b_ref[...]


def _ada(cond, w_ada, b_ada):
    per_mod = D_MODEL // ADA_COL_TILE
    return pl.pallas_call(
        _ada_kernel,
        grid=(N_MOD, per_mod),
        in_specs=[
            pl.BlockSpec((COND_ROWS, D_MODEL), lambda j, h: (0, 0)),
            pl.BlockSpec((D_MODEL, ADA_COL_TILE), lambda j, h: (0, j * per_mod + h)),
            pl.BlockSpec((1, ADA_COL_TILE), lambda j, h: (0, j * per_mod + h)),
        ],
        out_specs=pl.BlockSpec((None, COND_ROWS, ADA_COL_TILE), lambda j, h: (j, 0, h)),
        out_shape=jax.ShapeDtypeStruct((N_MOD, COND_ROWS, D_MODEL), _F32),
        compiler_params=_params(2),
        name="ada",
    )(cond, w_ada, b_ada.reshape(1, N_MOD * D_MODEL))


def _rope_tables(seq):
    t = jnp.arange(seq)
    row = (t // GRID_W).astype(_F32)
    col = (t % GRID_W).astype(_F32)
    half = HEAD_DIM // 2
    inv_freq = ROPE_THETA ** (-jnp.arange(0, half, 2, dtype=_F32) / half)
    ang_r = row[:, None] * inv_freq[None, :]
    ang_c = col[:, None] * inv_freq[None, :]
    cos = jnp.concatenate([jnp.cos(ang_r)] * 2 + [jnp.cos(ang_c)] * 2, axis=-1)
    sin = jnp.concatenate([-jnp.sin(ang_r), jnp.sin(ang_r), -jnp.sin(ang_c), jnp.sin(ang_c)], axis=-1)
    return cos, sin


def _cond_row(latent, seq_tiles):
    if not latent:
        return 0
    return 1 + pl.program_id(0) // seq_tiles


def _inproj_kernel(latent, seq_tiles, x_ref, mod_ref, g1_ref, w_ref, gq_ref, gk_ref, *rest):
    if latent:
        cos_ref, sin_ref, q_ref, k_ref, v_ref, bg_ref, cu_ref = rest
    else:
        q_ref, k_ref, v_ref, bg_ref, cu_ref, kstate_ref, vstate_ref, w_pub, *cast_scratch = rest
        _cast_weight_once(w_ref, w_pub, *cast_scratch, pl.program_id(0), pl.num_programs(0))
        w_ref = cast_scratch[0]
    r = _cond_row(latent, seq_tiles)
    sh = mod_ref[0, pl.ds(r, 1), :]
    sc = mod_ref[1, pl.ds(r, 1), :]
    h = _rms(x_ref[...]) * (g1_ref[...] * (1.0 + sc)) + sh
    hb = h.astype(_BF16)

    if latent:
        cos = cos_ref[...]
        sin = sin_ref[...]
        lane = lax.broadcasted_iota(jnp.int32, cos.shape, 1)
        first = (lane % (HEAD_DIM // 2)) < (HEAD_DIM // 4)

    def head_norm(xh, g):
        xh = _rms(xh) * g
        if latent:
            partner = jnp.where(first, pltpu.roll(xh, HEAD_DIM - HEAD_DIM // 4, axis=1),
                                pltpu.roll(xh, HEAD_DIM // 4, axis=1))
            xh = xh * cos + partner * sin
        return xh

    gq = gq_ref[...] * (HEAD_DIM ** -0.5 * LOG2_E)
    q = _dot(hb, w_ref[:, 0:D_ATTN])
    for hd in range(N_HEADS):
        cols = slice(hd * HEAD_DIM, (hd + 1) * HEAD_DIM)
        q_ref[:, cols] = head_norm(q[:, cols], gq).astype(q_ref.dtype)
    k = _dot(hb, w_ref[:, D_ATTN:D_ATTN + D_KV])
    o = D_ATTN + D_KV
    v = _dot(hb, w_ref[:, o:o + D_KV])
    for hd in range(N_KV_HEADS):
        cols = slice(hd * HEAD_DIM, (hd + 1) * HEAD_DIM)
        kh = head_norm(k[:, cols], gk_ref[...])
        k_ref[:, cols] = kh
        v_ref[:, cols] = v[:, cols]
        if not latent:
            kstate_ref[:, hd, :] = kh
            vstate_ref[:, hd, :] = v[:, cols]
    o += D_KV
    bg_ref[...] = _dot(hb, w_ref[:, o:o + D_CONV])
    o += D_CONV
    cg = _dot(hb, w_ref[:, o:o + D_CONV])
    o += D_CONV
    cu_ref[...] = cg * _dot(hb, w_ref[:, o:o + D_CONV])


def _inproj(x, mod, g_norm1, w_in, g_q, g_k, latent, seq):
    n_tok = x.shape[0]
    seq_tiles = seq // INPROJ_TILE
    tile = lambda w: pl.BlockSpec((INPROJ_TILE, w), lambda i: (i, 0))
    in_specs = [
        tile(D_MODEL),
        _resident((N_MOD, COND_ROWS, D_MODEL)),
        _resident((1, D_MODEL)),
        _resident((D_MODEL, D_IN_PROJ)) if latent else pl.BlockSpec(memory_space=pl.ANY),
        _resident((1, HEAD_DIM)),
        _resident((1, HEAD_DIM)),
    ]
    args = [x, mod, g_norm1.reshape(1, D_MODEL), w_in, g_q.reshape(1, HEAD_DIM), g_k.reshape(1, HEAD_DIM)]
    if latent:
        cos, sin = _rope_tables(seq)
        rope_spec = pl.BlockSpec((INPROJ_TILE, HEAD_DIM), lambda i: (i % seq_tiles, 0))
        in_specs += [rope_spec, rope_spec]
        args += [cos, sin]
    out_specs = [tile(D_ATTN), tile(D_KV), tile(D_KV), tile(D_CONV), tile(D_CONV)]
    out_shape = [
        jax.ShapeDtypeStruct((n_tok, D_ATTN), _BF16),
        jax.ShapeDtypeStruct((n_tok, D_KV), _F32),
        jax.ShapeDtypeStruct((n_tok, D_KV), _F32),
        jax.ShapeDtypeStruct((n_tok, D_CONV), _F32),
        jax.ShapeDtypeStruct((n_tok, D_CONV), _F32),
    ]
    if not latent:
        state_spec = pl.BlockSpec((INPROJ_TILE, N_KV_HEADS, HEAD_DIM), lambda i: (i, 0, 0))
        out_specs += [state_spec, state_spec, pl.BlockSpec(memory_space=pl.ANY)]
        out_shape += [jax.ShapeDtypeStruct((n_tok, N_KV_HEADS, HEAD_DIM), _F32)] * 2
        out_shape += [jax.ShapeDtypeStruct((D_MODEL, D_IN_PROJ), _BF16)]
    return pl.pallas_call(
        functools.partial(_inproj_kernel, latent, seq_tiles),
        grid=(n_tok // INPROJ_TILE,),
        in_specs=in_specs,
        out_specs=out_specs,
        out_shape=out_shape,
        scratch_shapes=[] if latent else _weight_scratch(D_MODEL, D_IN_PROJ),
        compiler_params=_params(1),
        name="inproj_lat" if latent else "inproj_ctx",
    )(*args)


def _mix_kernel(latent, seqs, seq_tiles, n_tiles, x_ref, q_ref, *rest):
    n_kv = 2 if latent else 1
    kv_refs = rest[:2 * n_kv]
    rest = rest[2 * n_kv:]
    bg_ref, cu_ref = rest[:2]
    rest = rest[2:]
    if latent:
        cu_prev_ref, cu_next_ref = rest[:2]
        rest = rest[2:]
    (convw_ref, ga_ref, gc_ref, wout_ref, mod_ref, g2_ref, wr_ref, xo_ref, h2_ref, lg_ref, *rest) = rest
    if latent:
        (mixed_scr,) = rest
    else:
        w_pub, mixed_scr, *cast_scratch = rest
        _cast_weight_once(wout_ref, w_pub, *cast_scratch, pl.program_id(0), pl.num_programs(0))
        wout_ref = cast_scratch[0]

    s = pl.program_id(0)

    @pl.when(s == 0)
    def _():
        mixed_scr[...] = jnp.zeros_like(mixed_scr)

    r = 1 + jnp.maximum(s - 1, 0) // seq_tiles if latent else 0
    y = _dot(mixed_scr[...], wout_ref[...])
    x_new = x_ref[...] + mod_ref[2, pl.ds(r, 1), :] * y
    xo_ref[...] = x_new
    h2 = _rms(x_new) * (g2_ref[...] * (1.0 + mod_ref[4, pl.ds(r, 1), :])) + mod_ref[3, pl.ds(r, 1), :]
    h2_hi = h2.astype(_BF16)
    h2_ref[...] = h2_hi
    lg_ref[...] = _dot_nt(wr_ref[...].astype(_BF16), h2_hi)

    rows_per_seq = x_ref.shape[0] // seqs
    row_sum_on_mxu = latent
    stacked = 1 if latent else CONTEXT_HEADS_STACKED
    sub = lax.broadcasted_iota(jnp.int32, (HALO_ROWS, D_CONV), 0)
    for i in range(seqs):
        rs = slice(i * rows_per_seq, (i + 1) * rows_per_seq)

        cu = cu_ref[rs, :]
        if latent:
            pos = jnp.minimum(s, n_tiles - 1) % seq_tiles
            edge_prev = jnp.where(pos == 0, 0.0, 1.0) * cu_prev_ref[HALO_ROWS - 1:HALO_ROWS, :]
            edge_next = jnp.where(pos == seq_tiles - 1, 0.0, 1.0) * cu_next_ref[0:1, :]
        else:
            edge_prev = jnp.zeros((1, D_CONV), _F32)
            edge_next = edge_prev
        prev = pltpu.roll(cu, 1, axis=0)
        nxt = pltpu.roll(cu, rows_per_seq - 1, axis=0)
        prev = jnp.concatenate([jnp.where(sub == 0, edge_prev, prev[0:HALO_ROWS]), prev[HALO_ROWS:]], axis=0)
        nxt = jnp.concatenate([nxt[:rows_per_seq - HALO_ROWS],
                               jnp.where(sub == HALO_ROWS - 1, edge_next, nxt[rows_per_seq - HALO_ROWS:])], axis=0)
        conv = bg_ref[rs, :] * (prev * convw_ref[0:1, :] + cu * convw_ref[1:2, :] + nxt * convw_ref[2:3, :])
        mixed_scr[rs, D_ATTN:D_ATTN + D_CONV] = (_rms(conv) * gc_ref[...]).astype(_BF16)

        heads = []
        for kvh in range(N_KV_HEADS):
            cols = slice(kvh * HEAD_DIM, (kvh + 1) * HEAD_DIM)
            if latent:
                ks = [kv_refs[2 * p][:, cols].astype(_BF16) for p in range(n_kv)]
                vs = [kv_refs[2 * p + 1][:, cols].astype(_BF16) for p in range(n_kv)]
            else:
                ks = [kv_refs[0][i, :, cols].astype(_BF16)]
                vs = [kv_refs[1][i, :, cols].astype(_BF16)]
            if row_sum_on_mxu:
                vs = [jnp.concatenate([vp, jnp.ones_like(vp)], axis=1) for vp in vs]
            for g0 in range(0, GQA_GROUP, stacked):
                first = kvh * GQA_GROUP + g0
                qh = jnp.concatenate([q_ref[rs, (first + j) * HEAD_DIM:(first + j + 1) * HEAD_DIM]
                                      for j in range(stacked)], axis=0)
                sc = [_dot_nt(qh, kp) for kp in ks]
                m = functools.reduce(jnp.maximum, [jnp.max(sp, axis=-1, keepdims=True) for sp in sc])
                if row_sum_on_mxu:
                    ol = functools.reduce(jnp.add,
                                          [_dot(jnp.exp2(sp - m).astype(_BF16), vp) for sp, vp in zip(sc, vs)])
                    o = ol[:, 0:HEAD_DIM] / ol[:, HEAD_DIM:HEAD_DIM + 1]
                else:
                    p = [jnp.exp2(sp - m) for sp in sc]
                    l = functools.reduce(jnp.add, [jnp.sum(pp, axis=-1, keepdims=True) for pp in p])
                    o = functools.reduce(jnp.add, [_dot(pp.astype(_BF16), vp) for pp, vp in zip(p, vs)]) / l
                heads += [o[j * rows_per_seq:(j + 1) * rows_per_seq] for j in range(stacked)]
        ssq = functools.reduce(jnp.add, [jnp.sum(h * h, axis=-1, keepdims=True) for h in heads])
        inv = lax.rsqrt(ssq * (1.0 / D_ATTN) + RMS_EPS)
        for hd, h in enumerate(heads):
            hc = slice(hd * HEAD_DIM, (hd + 1) * HEAD_DIM)
            mixed_scr[rs, hc] = (h * inv * ga_ref[:, hc]).astype(_BF16)


def _mix(x, q, kv_parts, bg, cu, mod, conv_w, g_attn_out, g_conv_out, w_out, g_norm2, w_router_t,
         latent, seq):
    n_tok = x.shape[0]
    tile = LATENT_MIX_TOKENS if latent else CONTEXT_MIX_TOKENS
    seqs = max(1, tile // seq)
    seq_tiles = max(1, seq // tile)
    n_tiles = n_tok // tile
    started = lambda i: jnp.minimum(i, n_tiles - 1)
    finished = lambda i: jnp.maximum(i - 1, 0)
    start_tile = lambda w: pl.BlockSpec((tile, w), lambda i: (started(i), 0))
    finish_tile = lambda w: pl.BlockSpec((tile, w), lambda i: (finished(i), 0))
    in_specs = [finish_tile(D_MODEL), start_tile(D_ATTN)]
    args = [x, q]
    for k, v in kv_parts:
        if latent:
            spec = pl.BlockSpec((None, k.shape[1], D_KV), lambda i: (started(i) // seq_tiles, 0, 0))
        else:
            spec = pl.BlockSpec((seqs, k.shape[1], D_KV), lambda i: (started(i), 0, 0))
        in_specs += [spec, spec]
        args += [k, v]
    in_specs += [start_tile(D_CONV), start_tile(D_CONV)]
    args += [bg, cu]
    if latent:
        per_tile = tile // HALO_ROWS
        last = n_tok // HALO_ROWS - 1
        in_specs += [
            pl.BlockSpec((HALO_ROWS, D_CONV), lambda i: (jnp.maximum(started(i) * per_tile - 1, 0), 0)),
            pl.BlockSpec((HALO_ROWS, D_CONV), lambda i: (jnp.minimum((started(i) + 1) * per_tile, last), 0)),
        ]
        args += [cu, cu]
    in_specs += [
        _resident((3, D_CONV)),
        _resident((1, D_ATTN)),
        _resident((1, D_CONV)),
        _resident((D_ATTN + D_CONV, D_MODEL)) if latent else pl.BlockSpec(memory_space=pl.ANY),
        _resident((N_MOD, COND_ROWS, D_MODEL)),
        _resident((1, D_MODEL)),
        _resident((N_EXPERTS, D_MODEL)),
    ]
    args += [conv_w, g_attn_out.reshape(1, D_ATTN), g_conv_out.reshape(1, D_CONV), w_out, mod,
             g_norm2.reshape(1, D_MODEL), w_router_t]
    out_specs = [finish_tile(D_MODEL), finish_tile(D_MODEL),
                 pl.BlockSpec((N_EXPERTS, tile), lambda i: (0, finished(i)))]
    out_shape = [
        jax.ShapeDtypeStruct((n_tok, D_MODEL), _F32),
        jax.ShapeDtypeStruct((n_tok, D_MODEL), _BF16),
        jax.ShapeDtypeStruct((N_EXPERTS, n_tok), _F32),
    ]
    scratch_shapes = [pltpu.VMEM((tile, D_ATTN + D_CONV), _BF16)]
    if not latent:
        out_specs += [pl.BlockSpec(memory_space=pl.ANY)]
        out_shape += [jax.ShapeDtypeStruct((D_ATTN + D_CONV, D_MODEL), _BF16)]
        scratch_shapes += _weight_scratch(D_ATTN + D_CONV, D_MODEL)
    return pl.pallas_call(
        functools.partial(_mix_kernel, latent, seqs, seq_tiles, n_tiles),
        grid=(n_tiles + 1,),
        in_specs=in_specs,
        out_specs=out_specs,
        out_shape=out_shape,
        scratch_shapes=scratch_shapes,
        compiler_params=_params(1),
        name="mix_lat" if latent else "mix_ctx",
    )(*args)


def _route_kernel(seq, cap, sets, e_blk, lg_ref, h2_ref, xs_ref, gw_ref, aff_scr, key_scr):
    j = pl.program_id(1)

    @pl.when(j == 0)
    def _():
        other = lax.broadcasted_iota(jnp.int32, (seq, seq), 0)
        token = lax.broadcasted_iota(jnp.int32, (seq, seq), 1)
        earlier = jnp.where(other < token, 1.0, 0.0)
        tie = earlier[0:LANES, 0:LANES]
        before = earlier.astype(_BF16)
        for s in range(sets):
            lg = lg_ref[:, s * seq:(s + 1) * seq]
            ex = jnp.exp(lg - jnp.max(lg, axis=0, keepdims=True))
            aff = ex / jnp.sum(ex, axis=0, keepdims=True)
            aff_t = jnp.concatenate([aff, jnp.zeros((LANES - N_EXPERTS, seq), _F32)], axis=0).T
            ranks = []
            for e in range(N_EXPERTS):
                theirs = aff_t[:, e:e + 1]
                blocks = []
                for lo in range(0, seq, LANES):
                    hi = lo + LANES
                    mine = aff[e:e + 1, lo:hi]
                    diag = theirs[lo:hi]
                    cnt = jnp.sum(jnp.where(diag > mine, 1.0, jnp.where(diag == mine, tie, 0.0)),
                                  axis=0, keepdims=True)
                    if lo > 0:
                        cnt += jnp.sum(jnp.where(theirs[0:lo] >= mine, 1.0, 0.0), axis=0, keepdims=True)
                    if hi < seq:
                        cnt += jnp.sum(jnp.where(theirs[hi:seq] > mine, 1.0, 0.0), axis=0, keepdims=True)
                    blocks.append(cnt)
                ranks.append(jnp.concatenate(blocks, axis=1))
            rank = jnp.concatenate(ranks, axis=0)
            sel = rank < float(cap)
            slot = _dot(jnp.where(sel, 1.0, 0.0).astype(_BF16), before)
            rows = slice(s * N_EXPERTS, (s + 1) * N_EXPERTS)
            aff_scr[rows, :] = aff
            key_scr[rows, :] = jnp.where(sel, slot, -1.0)

    c_idx = lax.broadcasted_iota(jnp.int32, (cap, seq), 0).astype(_F32)
    for s in range(sets):
        onehot, gated = [], []
        for i in range(e_blk):
            e = s * N_EXPERTS + j * e_blk + i
            hit = c_idx == key_scr[pl.ds(e, 1), :]
            onehot.append(jnp.where(hit, 1.0, 0.0))
            gated.append(jnp.where(hit, aff_scr[pl.ds(e, 1), :], 0.0))
        xs = _dot(jnp.concatenate(onehot, axis=0).astype(_BF16), h2_ref[s * seq:(s + 1) * seq, :]).astype(xs_ref.dtype)
        for i in range(e_blk):
            xs_ref[i, s * cap:(s + 1) * cap, :] = xs[i * cap:(i + 1) * cap, :]
        gw_ref[s] = jnp.concatenate(gated, axis=0).astype(gw_ref.dtype)


def _route(logits_t, h2, seq, sets, e_blk):
    n_tok = h2.shape[0]
    n_sets = n_tok // seq
    cap = EC_CAPACITY_FACTOR * seq // N_EXPERTS
    return pl.pallas_call(
        functools.partial(_route_kernel, seq, cap, sets, e_blk),
        grid=(n_sets // sets, N_EXPERTS // e_blk),
        in_specs=[
            pl.BlockSpec((N_EXPERTS, sets * seq), lambda s, j: (0, s)),
            pl.BlockSpec((sets * seq, D_MODEL), lambda s, j: (s, 0)),
        ],
        out_specs=[
            pl.BlockSpec((e_blk, sets * cap, D_MODEL), lambda s, j: (j, s, 0)),
            pl.BlockSpec((sets, e_blk * cap, seq), lambda s, j: (s, j, 0)),
        ],
        out_shape=[
            jax.ShapeDtypeStruct((N_EXPERTS, n_sets * cap, D_MODEL), _BF16),
            jax.ShapeDtypeStruct((n_sets, N_EXPERTS * cap, seq), _BF16),
        ],
        scratch_shapes=[pltpu.VMEM((sets * N_EXPERTS, seq), _F32), pltpu.VMEM((sets * N_EXPERTS, seq), _F32)],
        compiler_params=_params(2),
        name="route_%d" % seq,
    )(logits_t, h2)


def _ffn_kernel(xa_ref, xb_ref, wg_ref, wu_ref, wd_ref, ya_ref, yb_ref):
    def run(first):
        wg = wg_ref[...].astype(_BF16)
        wu = wu_ref[...].astype(_BF16)
        wd = wd_ref[...].astype(_BF16)
        for x_ref, y_ref in ((xa_ref, ya_ref), (xb_ref, yb_ref)):
            chunk = min(FFN_ROW_CHUNK, x_ref.shape[0])
            for c in range(x_ref.shape[0] // chunk):
                rows = slice(c * chunk, (c + 1) * chunk)
                x = x_ref[rows, :]
                a = _dot(x, wg)
                u = _dot(x, wu)
                total = _dot((a * jax.nn.sigmoid(a) * u).astype(_BF16), wd)
                if not first:
                    total = y_ref[rows, :].astype(_F32) + total
                y_ref[rows, :] = total.astype(y_ref.dtype)

    @pl.when(pl.program_id(1) == 0)
    def _():
        run(True)

    @pl.when(pl.program_id(1) > 0)
    def _():
        run(False)


def _ffn(xs_a, xs_b, w_gate, w_up, w_down):
    rows_a, rows_b = xs_a.shape[1], xs_b.shape[1]
    rows = lambda n: pl.BlockSpec((None, n, D_MODEL), lambda e, f: (e, 0, 0))
    return pl.pallas_call(
        _ffn_kernel,
        grid=(N_EXPERTS, D_FF // FF_TILE),
        in_specs=[
            rows(rows_a), rows(rows_b),
            pl.BlockSpec((None, D_MODEL, FF_TILE), lambda e, f: (e, 0, f)),
            pl.BlockSpec((None, D_MODEL, FF_TILE), lambda e, f: (e, 0, f)),
            pl.BlockSpec((None, FF_TILE, D_MODEL), lambda e, f: (e, f, 0)),
        ],
        out_specs=[rows(rows_a), rows(rows_b)],
        out_shape=[jax.ShapeDtypeStruct(xs_a.shape, _BF16), jax.ShapeDtypeStruct(xs_b.shape, _BF16)],
        compiler_params=_params(2),
        name="ffn",
    )(xs_a, xs_b, w_gate, w_up, w_down)


def _combine_kernel(latent, x_ref, y_ref, gw_ref, mod_ref, o_ref):
    r = 1 + pl.program_id(0) if latent else 0
    sets, rows, tile = gw_ref.shape
    cap = rows // N_EXPERTS
    gate = mod_ref[5, pl.ds(r, 1), :]
    for i in range(sets):
        y = y_ref[:, i * cap:(i + 1) * cap, :].reshape(rows, D_MODEL)
        moe = lax.dot_general(gw_ref[i], y, (((0,), (0,)), ((), ())), preferred_element_type=_F32)
        tok = slice(i * tile, (i + 1) * tile)
        o_ref[tok, :] = x_ref[tok, :] + gate * moe


def _combine(x_new, y, gw, mod, latent, seq, step_tokens):
    n_tok = x_new.shape[0]
    n_sets = n_tok // seq
    cap = EC_CAPACITY_FACTOR * seq // N_EXPERTS
    sets = max(1, step_tokens // seq)
    tile = step_tokens // sets
    per_set = seq // tile
    tokens = pl.BlockSpec((step_tokens, D_MODEL), lambda s, t: (s * per_set + t, 0))
    return pl.pallas_call(
        functools.partial(_combine_kernel, latent),
        grid=(n_sets // sets, per_set),
        in_specs=[
            tokens,
            pl.BlockSpec((N_EXPERTS, sets * cap, D_MODEL), lambda s, t: (0, s, 0)),
            pl.BlockSpec((sets, N_EXPERTS * cap, tile), lambda s, t: (s, 0, t)),
            _resident((N_MOD, COND_ROWS, D_MODEL)),
        ],
        out_specs=tokens,
        out_shape=jax.ShapeDtypeStruct((n_tok, D_MODEL), _F32),
        compiler_params=_params(2),
        name="combine_lat" if latent else "combine_ctx",
    )(x_new, y, gw, mod)


def kernel(x_prompt, x_sample, cache_k, cache_v, c, c_ctx, w_ada, b_ada, g_norm1, w_in, g_q, g_k, conv_w,
           g_attn_out, g_conv_out, w_out, g_norm2, w_router, w_gate, w_up, w_down):
    batch, seq, _ = x_prompt.shape
    dec_batch, dec_seq, _ = x_sample.shape
    depth = w_ada.shape[0]
    assert 1 + dec_batch <= COND_ROWS

    cond = jnp.concatenate([c_ctx[None, :], c, jnp.zeros((COND_ROWS - 1 - dec_batch, D_MODEL), _F32)], axis=0)
    xp = x_prompt.reshape(batch * seq, D_MODEL)
    xl = x_sample.reshape(dec_batch * dec_seq, D_MODEL)
    past = cache_k.shape[2]

    new_k, new_v = [], []
    for l in range(depth):
        mod = _ada(cond, w_ada[l], b_ada[l])
        w_router_t = w_router[l].T

        q_p, k_p, v_p, bg_p, cu_p, kstate, vstate, w_in_b = _inproj(xp, mod, g_norm1[l], w_in[l], g_q[l], g_k[l],
                                                                    False, seq)
        q_l, k_l, v_l, bg_l, cu_l = _inproj(xl, mod, g_norm1[l], w_in_b, g_q[l], g_k[l], True, dec_seq)

        shared = lambda w: (mod, conv_w[l], g_attn_out[l], g_conv_out[l], w, g_norm2[l], w_router_t)
        kv_p = [(k_p.reshape(batch, seq, D_KV), v_p.reshape(batch, seq, D_KV))]
        kv_l = [(cache_k[:, l].reshape(dec_batch, past, D_KV), cache_v[:, l].reshape(dec_batch, past, D_KV)),
                (k_l.reshape(dec_batch, dec_seq, D_KV), v_l.reshape(dec_batch, dec_seq, D_KV))]
        xp1, h2_p, lg_p, w_out_b = _mix(xp, q_p, kv_p, bg_p, cu_p, *shared(w_out[l]), False, seq)
        xl1, h2_l, lg_l = _mix(xl, q_l, kv_l, bg_l, cu_l, *shared(w_out_b), True, dec_seq)

        xs_p, gw_p = _route(lg_p, h2_p, seq, CONTEXT_ROUTE_SETS, N_EXPERTS)
        xs_l, gw_l = _route(lg_l, h2_l, dec_seq, 1, LATENT_ROUTE_EXPERTS)
        y_p, y_l = _ffn(xs_p, xs_l, w_gate[l], w_up[l], w_down[l])
        xp = _combine(xp1, y_p, gw_p, mod, False, seq, CONTEXT_COMBINE_TOKENS)
        xl = _combine(xl1, y_l, gw_l, mod, True, dec_seq, LATENT_COMBINE_TOKENS)

        new_k.append(kstate.reshape(batch, seq, N_KV_HEADS, HEAD_DIM))
        new_v.append(vstate.reshape(batch, seq, N_KV_HEADS, HEAD_DIM))

    return (xp.reshape(batch, seq, D_MODEL), xl.reshape(dec_batch, dec_seq, D_MODEL),
            jnp.stack(new_k, axis=1), jnp.stack(new_v, axis=1))
```

```python
import functools

import jax
import jax.numpy as jnp
from jax import lax
from jax.experimental import pallas as pl
from jax.experimental.pallas import tpu as pltpu

D_MODEL = 2048
GRID_W = 64
HEAD_DIM = 128
D_ATTN = 1024
D_CONV = 1024
N_HEADS = 8
N_KV_HEADS = 2
GQA_GROUP = N_HEADS // N_KV_HEADS
D_KV = N_KV_HEADS * HEAD_DIM
N_EXPERTS = 16
EC_CAPACITY_FACTOR = 2
D_FF = 1024
ROPE_THETA = 10000.0
RMS_EPS = 1e-6
LOG2_E = 1.4426950408889634
N_MOD = 6
D_IN_PROJ = D_ATTN + 2 * D_KV + 3 * D_CONV

CONTEXT_MIX_TOKENS = 512
LATENT_MIX_TOKENS = 256
INPROJ_TILE = 512
ADA_COL_TILE = 1024
FF_TILE = 512
FFN_ROW_CHUNK = 512
COND_ROWS = 8
HALO_ROWS = 8
LATENT_ROUTE_EXPERTS = 8
CONTEXT_ROUTE_SETS = 4
CONTEXT_HEADS_STACKED = 2
VMEM_LIMIT = 56 * 1024 * 1024
LANES = 128
STAGE_ROWS = 32
STAGE_SLOTS = 8
CONTEXT_COMBINE_TOKENS = 1024
LATENT_COMBINE_TOKENS = 512

_BF16 = jnp.bfloat16
_F32 = jnp.float32


def _params(n_axes):
    return pltpu.CompilerParams(dimension_semantics=("arbitrary",) * n_axes,
                                vmem_limit_bytes=VMEM_LIMIT)


def _resident(shape):
    return pl.BlockSpec(shape, lambda *_: (0,) * len(shape), pipeline_mode=pl.Buffered(1))


def _dot(a, b):
    return jnp.dot(a, b, preferred_element_type=_F32)


def _dot_nt(a, b):
    return lax.dot_general(a, b, (((1,), (1,)), ((), ())), preferred_element_type=_F32)


def _rms(x):
    return x * lax.rsqrt(jnp.mean(x * x, axis=-1, keepdims=True) + RMS_EPS)


def _split_bf16(x):
    hi = x.astype(_BF16)
    lo = (x - hi.astype(_F32)).astype(_BF16)
    return hi, lo


def _weight_scratch(rows, cols):
    return [pltpu.VMEM((rows, cols), _BF16), pltpu.VMEM((STAGE_SLOTS, STAGE_ROWS, cols), _F32),
            pltpu.SemaphoreType.DMA((STAGE_SLOTS,)), pltpu.SemaphoreType.DMA(())]


def _cast_weight_once(w_hbm, w_pub, w_scr, stage, sems, pub_sem, step, n_steps):
    n_chunks = w_scr.shape[0] // STAGE_ROWS

    def fetch(c):
        slot = c % STAGE_SLOTS
        return pltpu.make_async_copy(w_hbm.at[pl.ds(c * STAGE_ROWS, STAGE_ROWS), :], stage.at[slot], sems.at[slot])

    publish = pltpu.make_async_copy(w_scr, w_pub, pub_sem)

    @pl.when(step == 0)
    def _():
        for c in range(STAGE_SLOTS):
            fetch(c).start()
        for c in range(n_chunks):
            fetch(c).wait()
            w_scr[c * STAGE_ROWS:(c + 1) * STAGE_ROWS, :] = stage[c % STAGE_SLOTS].astype(_BF16)
            if c + STAGE_SLOTS < n_chunks:
                fetch(c + STAGE_SLOTS).start()
        publish.start()

    @pl.when(step == n_steps - 1)
    def _():
        publish.wait()


def _stream_scratch(rows, cols, n_steps):
    chunk = rows // n_steps
    return [pltpu.VMEM((2, chunk, cols), _F32), pltpu.VMEM((2, chunk, cols), _BF16),
            pltpu.SemaphoreType.DMA((2,)), pltpu.SemaphoreType.DMA((2,))]


def _cast_weight_alongside(w_hbm, w_pub, stage, stage_b, in_sems, out_sems, step, n_steps):
    chunk = stage.shape[1]
    slot = step % 2

    def fetch(k, sl):
        return pltpu.make_async_copy(w_hbm.at[pl.ds(k * chunk, chunk), :], stage.at[sl], in_sems.at[sl])

    def push(k, sl):
        return pltpu.make_async_copy(stage_b.at[sl], w_pub.at[pl.ds(k * chunk, chunk), :], out_sems.at[sl])

    @pl.when(step == 0)
    def _():
        fetch(0, 0).start()

    @pl.when(step + 1 < n_steps)
    def _():
        fetch(step + 1, 1 - slot).start()

    fetch(step, slot).wait()

    @pl.when(step >= 2)
    def _():
        push(step - 2, slot).wait()

    stage_b[slot] = stage[slot].astype(_BF16)
    push(step, slot).start()

    @pl.when(step == n_steps - 1)
    def _():
        push(step, slot).wait()
        push(step - 1, 1 - slot).wait()


def _ada_kernel(c_ref, w_ref, b_ref, o_ref):
    c = c_ref[...]
    a = c * jax.nn.sigmoid(c)
    a_hi, a_lo = _split_bf16(a)
    w = w_ref[...]
    w_hi, w_lo = _split_bf16(w)
    o_ref[...] = _dot(a_hi, w_hi) + (_dot(a_lo, w_hi) + _dot(a_hi, w_lo)) + b_ref[...]


def _ada(cond, w_ada, b_ada):
    per_mod = D_MODEL // ADA_COL_TILE
    return pl.pallas_call(
        _ada_kernel,
        grid=(N_MOD, per_mod),
        in_specs=[
            pl.BlockSpec((COND_ROWS, D_MODEL), lambda j, h: (0, 0)),
            pl.BlockSpec((D_MODEL, ADA_COL_TILE), lambda j, h: (0, j * per_mod + h)),
            pl.BlockSpec((1, ADA_COL_TILE), lambda j, h: (0, j * per_mod + h)),
        ],
        out_specs=pl.BlockSpec((None, COND_ROWS, ADA_COL_TILE), lambda j, h: (j, 0, h)),
        out_shape=jax.ShapeDtypeStruct((N_MOD, COND_ROWS, D_MODEL), _F32),
        compiler_params=_params(2),
        name="ada",
    )(cond, w_ada, b_ada.reshape(1, N_MOD * D_MODEL))


def _rope_tables(seq):
    t = jnp.arange(seq)
    row = (t // GRID_W).astype(_F32)
    col = (t % GRID_W).astype(_F32)
    half = HEAD_DIM // 2
    inv_freq = ROPE_THETA ** (-jnp.arange(0, half, 2, dtype=_F32) / half)
    ang_r = row[:, None] * inv_freq[None, :]
    ang_c = col[:, None] * inv_freq[None, :]
    cos = jnp.concatenate([jnp.cos(ang_r)] * 2 + [jnp.cos(ang_c)] * 2, axis=-1)
    sin = jnp.concatenate([-jnp.sin(ang_r), jnp.sin(ang_r), -jnp.sin(ang_c), jnp.sin(ang_c)], axis=-1)
    return cos, sin


def _cond_row(latent, seq_tiles):
    if not latent:
        return 0
    return 1 + pl.program_id(0) // seq_tiles


def _inproj_kernel(latent, seq_tiles, x_ref, mod_ref, g1_ref, w_ref, gq_ref, gk_ref, *rest):
    if latent:
        cos_ref, sin_ref, q_ref, k_ref, v_ref, bg_ref, cu_ref = rest
    else:
        (wo_hbm, q_ref, k_ref, v_ref, bg_ref, cu_ref, kstate_ref, vstate_ref, w_pub, wo_pub,
         *scratch) = rest
        cast_scratch, stream_scratch = scratch[:4], scratch[4:]
        _cast_weight_once(w_ref, w_pub, *cast_scratch, pl.program_id(0), pl.num_programs(0))
        w_ref = cast_scratch[0]
        _cast_weight_alongside(wo_hbm, wo_pub, *stream_scratch, pl.program_id(0), pl.num_programs(0))
    r = _cond_row(latent, seq_tiles)
    sh = mod_ref[0, pl.ds(r, 1), :]
    sc = mod_ref[1, pl.ds(r, 1), :]
    h = _rms(x_ref[...]) * (g1_ref[...] * (1.0 + sc)) + sh
    hb = h.astype(_BF16)

    if latent:
        cos = cos_ref[...]
        sin = sin_ref[...]
        lane = lax.broadcasted_iota(jnp.int32, cos.shape, 1)
        first = (lane % (HEAD_DIM // 2)) < (HEAD_DIM // 4)

    def head_norm(xh, g):
        xh = _rms(xh) * g
        if latent:
            partner = jnp.where(first, pltpu.roll(xh, HEAD_DIM - HEAD_DIM // 4, axis=1),
                                pltpu.roll(xh, HEAD_DIM // 4, axis=1))
            xh = xh * cos + partner * sin
        return xh

    gq = gq_ref[...] * (HEAD_DIM ** -0.5 * LOG2_E)
    q = _dot(hb, w_ref[:, 0:D_ATTN])
    for hd in range(N_HEADS):
        cols = slice(hd * HEAD_DIM, (hd + 1) * HEAD_DIM)
        q_ref[:, cols] = head_norm(q[:, cols], gq).astype(q_ref.dtype)
    k = _dot(hb, w_ref[:, D_ATTN:D_ATTN + D_KV])
    o = D_ATTN + D_KV
    v = _dot(hb, w_ref[:, o:o + D_KV])
    for hd in range(N_KV_HEADS):
        cols = slice(hd * HEAD_DIM, (hd + 1) * HEAD_DIM)
        kh = head_norm(k[:, cols], gk_ref[...])
        k_ref[:, cols] = kh
        v_ref[:, cols] = v[:, cols]
        if not latent:
            kstate_ref[:, hd, :] = kh
            vstate_ref[:, hd, :] = v[:, cols]
    o += D_KV
    bg_ref[...] = _dot(hb, w_ref[:, o:o + D_CONV])
    o += D_CONV
    cg = _dot(hb, w_ref[:, o:o + D_CONV])
    o += D_CONV
    cu_ref[...] = cg * _dot(hb, w_ref[:, o:o + D_CONV])


def _inproj(x, mod, g_norm1, w_in, g_q, g_k, latent, seq, w_out=None):
    n_tok = x.shape[0]
    seq_tiles = seq // INPROJ_TILE
    tile = lambda w: pl.BlockSpec((INPROJ_TILE, w), lambda i: (i, 0))
    in_specs = [
        tile(D_MODEL),
        _resident((N_MOD, COND_ROWS, D_MODEL)),
        _resident((1, D_MODEL)),
        _resident((D_MODEL, D_IN_PROJ)) if latent else pl.BlockSpec(memory_space=pl.ANY),
        _resident((1, HEAD_DIM)),
        _resident((1, HEAD_DIM)),
    ]
    args = [x, mod, g_norm1.reshape(1, D_MODEL), w_in, g_q.reshape(1, HEAD_DIM), g_k.reshape(1, HEAD_DIM)]
    if latent:
        cos, sin = _rope_tables(seq)
        rope_spec = pl.BlockSpec((INPROJ_TILE, HEAD_DIM), lambda i: (i % seq_tiles, 0))
        in_specs += [rope_spec, rope_spec]
        args += [cos, sin]
    else:
        in_specs += [pl.BlockSpec(memory_space=pl.ANY)]
        args += [w_out]
    n_steps = n_tok // INPROJ_TILE
    scratch_shapes = []
    out_specs = [tile(D_ATTN), tile(D_KV), tile(D_KV), tile(D_CONV), tile(D_CONV)]
    out_shape = [
        jax.ShapeDtypeStruct((n_tok, D_ATTN), _BF16),
        jax.ShapeDtypeStruct((n_tok, D_KV), _F32),
        jax.ShapeDtypeStruct((n_tok, D_KV), _F32),
        jax.ShapeDtypeStruct((n_tok, D_CONV), _F32),
        jax.ShapeDtypeStruct((n_tok, D_CONV), _F32),
    ]
    if not latent:
        state_spec = pl.BlockSpec((INPROJ_TILE, N_KV_HEADS, HEAD_DIM), lambda i: (i, 0, 0))
        out_specs += [state_spec, state_spec, pl.BlockSpec(memory_space=pl.ANY), pl.BlockSpec(memory_space=pl.ANY)]
        out_shape += [jax.ShapeDtypeStruct((n_tok, N_KV_HEADS, HEAD_DIM), _F32)] * 2
        out_shape += [jax.ShapeDtypeStruct((D_MODEL, D_IN_PROJ), _BF16), jax.ShapeDtypeStruct(w_out.shape, _BF16)]
        assert n_steps >= 2 and w_out.shape[0] % (16 * n_steps) == 0
        scratch_shapes = _weight_scratch(D_MODEL, D_IN_PROJ) + _stream_scratch(*w_out.shape, n_steps)
    return pl.pallas_call(
        functools.partial(_inproj_kernel, latent, seq_tiles),
        grid=(n_steps,),
        in_specs=in_specs,
        out_specs=out_specs,
        out_shape=out_shape,
        scratch_shapes=scratch_shapes,
        compiler_params=_params(1),
        name="inproj_lat" if latent else "inproj_ctx",
    )(*args)


def _mix_kernel(latent, seqs, seq_tiles, n_tiles, x_ref, q_ref, *rest):
    n_kv = 2 if latent else 1
    kv_refs = rest[:2 * n_kv]
    rest = rest[2 * n_kv:]
    bg_ref, cu_ref = rest[:2]
    rest = rest[2:]
    if latent:
        cu_prev_ref, cu_next_ref = rest[:2]
        rest = rest[2:]
    (convw_ref, ga_ref, gc_ref, wout_ref, mod_ref, g2_ref, wr_ref, xo_ref, h2_ref, lg_ref, *rest) = rest
    (mixed_scr,) = rest

    s = pl.program_id(0)

    @pl.when(s == 0)
    def _():
        mixed_scr[...] = jnp.zeros_like(mixed_scr)

    r = 1 + jnp.maximum(s - 1, 0) // seq_tiles if latent else 0
    y = _dot(mixed_scr[...], wout_ref[...])
    x_new = x_ref[...] + mod_ref[2, pl.ds(r, 1), :] * y
    xo_ref[...] = x_new
    h2 = _rms(x_new) * (g2_ref[...] * (1.0 + mod_ref[4, pl.ds(r, 1), :])) + mod_ref[3, pl.ds(r, 1), :]
    h2_hi = h2.astype(_BF16)
    h2_ref[...] = h2_hi
    lg_ref[...] = _dot_nt(wr_ref[...].astype(_BF16), h2_hi)

    rows_per_seq = x_ref.shape[0] // seqs
    row_sum_on_mxu = latent
    stacked = 1 if latent else CONTEXT_HEADS_STACKED
    sub = lax.broadcasted_iota(jnp.int32, (HALO_ROWS, D_CONV), 0)
    for i in range(seqs):
        rs = slice(i * rows_per_seq, (i + 1) * rows_per_seq)

        cu = cu_ref[rs, :]
        if latent:
            pos = jnp.minimum(s, n_tiles - 1) % seq_tiles
            edge_prev = jnp.where(pos == 0, 0.0, 1.0) * cu_prev_ref[HALO_ROWS - 1:HALO_ROWS, :]
            edge_next = jnp.where(pos == seq_tiles - 1, 0.0, 1.0) * cu_next_ref[0:1, :]
        else:
            edge_prev = jnp.zeros((1, D_CONV), _F32)
            edge_next = edge_prev
        prev = pltpu.roll(cu, 1, axis=0)
        nxt = pltpu.roll(cu, rows_per_seq - 1, axis=0)
        prev = jnp.concatenate([jnp.where(sub == 0, edge_prev, prev[0:HALO_ROWS]), prev[HALO_ROWS:]], axis=0)
        nxt = jnp.concatenate([nxt[:rows_per_seq - HALO_ROWS],
                               jnp.where(sub == HALO_ROWS - 1, edge_next, nxt[rows_per_seq - HALO_ROWS:])], axis=0)
        conv = bg_ref[rs, :] * (prev * convw_ref[0:1, :] + cu * convw_ref[1:2, :] + nxt * convw_ref[2:3, :])
        mixed_scr[rs, D_ATTN:D_ATTN + D_CONV] = (_rms(conv) * gc_ref[...]).astype(_BF16)

        heads = []
        for kvh in range(N_KV_HEADS):
            cols = slice(kvh * HEAD_DIM, (kvh + 1) * HEAD_DIM)
            if latent:
                ks = [kv_refs[2 * p][:, cols].astype(_BF16) for p in range(n_kv)]
                vs = [kv_refs[2 * p + 1][:, cols].astype(_BF16) for p in range(n_kv)]
            else:
                ks = [kv_refs[0][i, :, cols].astype(_BF16)]
                vs = [kv_refs[1][i, :, cols].astype(_BF16)]
            if row_sum_on_mxu:
                vs = [jnp.concatenate([vp, jnp.ones_like(vp)], axis=1) for vp in vs]
            for g0 in range(0, GQA_GROUP, stacked):
                first = kvh * GQA_GROUP + g0
                qh = jnp.concatenate([q_ref[rs, (first + j) * HEAD_DIM:(first + j + 1) * HEAD_DIM]
                                      for j in range(stacked)], axis=0)
                sc = [_dot_nt(qh, kp) for kp in ks]
                m = functools.reduce(jnp.maximum, [jnp.max(sp, axis=-1, keepdims=True) for sp in sc])
                if row_sum_on_mxu:
                    ol = functools.reduce(jnp.add,
                                          [_dot(jnp.exp2(sp - m).astype(_BF16), vp) for sp, vp in zip(sc, vs)])
                    o = ol[:, 0:HEAD_DIM] / ol[:, HEAD_DIM:HEAD_DIM + 1]
                else:
                    p = [jnp.exp2(sp - m) for sp in sc]
                    l = functools.reduce(jnp.add, [jnp.sum(pp, axis=-1, keepdims=True) for pp in p])
                    o = functools.reduce(jnp.add, [_dot(pp.astype(_BF16), vp) for pp, vp in zip(p, vs)]) / l
                heads += [o[j * rows_per_seq:(j + 1) * rows_per_seq] for j in range(stacked)]
        ssq = functools.reduce(jnp.add, [jnp.sum(h * h, axis=-1, keepdims=True) for h in heads])
        inv = lax.rsqrt(ssq * (1.0 / D_ATTN) + RMS_EPS)
        for hd, h in enumerate(heads):
            hc = slice(hd * HEAD_DIM, (hd + 1) * HEAD_DIM)
            mixed_scr[rs, hc] = (h * inv * ga_ref[:, hc]).astype(_BF16)


def _mix(x, q, kv_parts, bg, cu, mod, conv_w, g_attn_out, g_conv_out, w_out, g_norm2, w_router_t,
         latent, seq):
    n_tok = x.shape[0]
    tile = LATENT_MIX_TOKENS if latent else CONTEXT_MIX_TOKENS
    seqs = max(1, tile // seq)
    seq_tiles = max(1, seq // tile)
    n_tiles = n_tok // tile
    started = lambda i: jnp.minimum(i, n_tiles - 1)
    finished = lambda i: jnp.maximum(i - 1, 0)
    start_tile = lambda w: pl.BlockSpec((tile, w), lambda i: (started(i), 0))
    finish_tile = lambda w: pl.BlockSpec((tile, w), lambda i: (finished(i), 0))
    in_specs = [finish_tile(D_MODEL), start_tile(D_ATTN)]
    args = [x, q]
    for k, v in kv_parts:
        if latent:
            spec = pl.BlockSpec((None, k.shape[1], D_KV), lambda i: (started(i) // seq_tiles, 0, 0))
        else:
            spec = pl.BlockSpec((seqs, k.shape[1], D_KV), lambda i: (started(i), 0, 0))
        in_specs += [spec, spec]
        args += [k, v]
    in_specs += [start_tile(D_CONV), start_tile(D_CONV)]
    args += [bg, cu]
    if latent:
        per_tile = tile // HALO_ROWS
        last = n_tok // HALO_ROWS - 1
        in_specs += [
            pl.BlockSpec((HALO_ROWS, D_CONV), lambda i: (jnp.maximum(started(i) * per_tile - 1, 0), 0)),
            pl.BlockSpec((HALO_ROWS, D_CONV), lambda i: (jnp.minimum((started(i) + 1) * per_tile, last), 0)),
        ]
        args += [cu, cu]
    in_specs += [
        _resident((3, D_CONV)),
        _resident((1, D_ATTN)),
        _resident((1, D_CONV)),
        _resident((D_ATTN + D_CONV, D_MODEL)),
        _resident((N_MOD, COND_ROWS, D_MODEL)),
        _resident((1, D_MODEL)),
        _resident((N_EXPERTS, D_MODEL)),
    ]
    args += [conv_w, g_attn_out.reshape(1, D_ATTN), g_conv_out.reshape(1, D_CONV), w_out, mod,
             g_norm2.reshape(1, D_MODEL), w_router_t]
    out_specs = [finish_tile(D_MODEL), finish_tile(D_MODEL),
                 pl.BlockSpec((N_EXPERTS, tile), lambda i: (0, finished(i)))]
    out_shape = [
        jax.ShapeDtypeStruct((n_tok, D_MODEL), _F32),
        jax.ShapeDtypeStruct((n_tok, D_MODEL), _BF16),
        jax.ShapeDtypeStruct((N_EXPERTS, n_tok), _F32),
    ]
    scratch_shapes = [pltpu.VMEM((tile, D_ATTN + D_CONV), _BF16)]
    return pl.pallas_call(
        functools.partial(_mix_kernel, latent, seqs, seq_tiles, n_tiles),
        grid=(n_tiles + 1,),
        in_specs=in_specs,
        out_specs=out_specs,
        out_shape=out_shape,
        scratch_shapes=scratch_shapes,
        compiler_params=_params(1),
        name="mix_lat" if latent else "mix_ctx",
    )(*args)


def _route_kernel(seq, cap, sets, e_blk, lg_ref, h2_ref, xs_ref, gw_ref, aff_scr, key_scr):
    j = pl.program_id(1)

    @pl.when(j == 0)
    def _():
        other = lax.broadcasted_iota(jnp.int32, (seq, seq), 0)
        token = lax.broadcasted_iota(jnp.int32, (seq, seq), 1)
        earlier = jnp.where(other < token, 1.0, 0.0)
        tie = earlier[0:LANES, 0:LANES]
        before = earlier.astype(_BF16)
        for s in range(sets):
            lg = lg_ref[:, s * seq:(s + 1) * seq]
            ex = jnp.exp(lg - jnp.max(lg, axis=0, keepdims=True))
            aff = ex / jnp.sum(ex, axis=0, keepdims=True)
            aff_t = jnp.concatenate([aff, jnp.zeros((LANES - N_EXPERTS, seq), _F32)], axis=0).T
            ranks = []
            for e in range(N_EXPERTS):
                theirs = aff_t[:, e:e + 1]
                blocks = []
                for lo in range(0, seq, LANES):
                    hi = lo + LANES
                    mine = aff[e:e + 1, lo:hi]
                    diag = theirs[lo:hi]
                    cnt = jnp.sum(jnp.where(diag > mine, 1.0, jnp.where(diag == mine, tie, 0.0)),
                                  axis=0, keepdims=True)
                    if lo > 0:
                        cnt += jnp.sum(jnp.where(theirs[0:lo] >= mine, 1.0, 0.0), axis=0, keepdims=True)
                    if hi < seq:
                        cnt += jnp.sum(jnp.where(theirs[hi:seq] > mine, 1.0, 0.0), axis=0, keepdims=True)
                    blocks.append(cnt)
                ranks.append(jnp.concatenate(blocks, axis=1))
            rank = jnp.concatenate(ranks, axis=0)
            sel = rank < float(cap)
            slot = _dot(jnp.where(sel, 1.0, 0.0).astype(_BF16), before)
            rows = slice(s * N_EXPERTS, (s + 1) * N_EXPERTS)
            aff_scr[rows, :] = aff
            key_scr[rows, :] = jnp.where(sel, slot, -1.0)

    c_idx = lax.broadcasted_iota(jnp.int32, (cap, seq), 0).astype(_F32)
    for s in range(sets):
        onehot, gated = [], []
        for i in range(e_blk):
            e = s * N_EXPERTS + j * e_blk + i
            hit = c_idx == key_scr[pl.ds(e, 1), :]
            onehot.append(jnp.where(hit, 1.0, 0.0))
            gated.append(jnp.where(hit, aff_scr[pl.ds(e, 1), :], 0.0))
        xs = _dot(jnp.concatenate(onehot, axis=0).astype(_BF16), h2_ref[s * seq:(s + 1) * seq, :]).astype(xs_ref.dtype)
        for i in range(e_blk):
            xs_ref[i, s * cap:(s + 1) * cap, :] = xs[i * cap:(i + 1) * cap, :]
        gw_ref[s] = jnp.concatenate(gated, axis=0).astype(gw_ref.dtype)


def _route(logits_t, h2, seq, sets, e_blk):
    n_tok = h2.shape[0]
    n_sets = n_tok // seq
    cap = EC_CAPACITY_FACTOR * seq // N_EXPERTS
    return pl.pallas_call(
        functools.partial(_route_kernel, seq, cap, sets, e_blk),
        grid=(n_sets // sets, N_EXPERTS // e_blk),
        in_specs=[
            pl.BlockSpec((N_EXPERTS, sets * seq), lambda s, j: (0, s)),
            pl.BlockSpec((sets * seq, D_MODEL), lambda s, j: (s, 0)),
        ],
        out_specs=[
            pl.BlockSpec((e_blk, sets * cap, D_MODEL), lambda s, j: (j, s, 0)),
            pl.BlockSpec((sets, e_blk * cap, seq), lambda s, j: (s, j, 0)),
        ],
        out_shape=[
            jax.ShapeDtypeStruct((N_EXPERTS, n_sets * cap, D_MODEL), _BF16),
            jax.ShapeDtypeStruct((n_sets, N_EXPERTS * cap, seq), _BF16),
        ],
        scratch_shapes=[pltpu.VMEM((sets * N_EXPERTS, seq), _F32), pltpu.VMEM((sets * N_EXPERTS, seq), _F32)],
        compiler_params=_params(2),
        name="route_%d" % seq,
    )(logits_t, h2)


def _ffn_kernel(xa_ref, xb_ref, wg_ref, wu_ref, wd_ref, ya_ref, yb_ref):
    def run(first):
        wg = wg_ref[...].astype(_BF16)
        wu = wu_ref[...].astype(_BF16)
        wd = wd_ref[...].astype(_BF16)
        for x_ref, y_ref in ((xa_ref, ya_ref), (xb_ref, yb_ref)):
            chunk = min(FFN_ROW_CHUNK, x_ref.shape[0])
            for c in range(x_ref.shape[0] // chunk):
                rows = slice(c * chunk, (c + 1) * chunk)
                x = x_ref[rows, :]
                a = _dot(x, wg)
                u = _dot(x, wu)
                total = _dot((a * jax.nn.sigmoid(a) * u).astype(_BF16), wd)
                if not first:
                    total = y_ref[rows, :].astype(_F32) + total
                y_ref[rows, :] = total.astype(y_ref.dtype)

    @pl.when(pl.program_id(1) == 0)
    def _():
        run(True)

    @pl.when(pl.program_id(1) > 0)
    def _():
        run(False)


def _ffn(xs_a, xs_b, w_gate, w_up, w_down):
    rows_a, rows_b = xs_a.shape[1], xs_b.shape[1]
    rows = lambda n: pl.BlockSpec((None, n, D_MODEL), lambda e, f: (e, 0, 0))
    return pl.pallas_call(
        _ffn_kernel,
        grid=(N_EXPERTS, D_FF // FF_TILE),
        in_specs=[
            rows(rows_a), rows(rows_b),
            pl.BlockSpec((None, D_MODEL, FF_TILE), lambda e, f: (e, 0, f)),
            pl.BlockSpec((None, D_MODEL, FF_TILE), lambda e, f: (e, 0, f)),
            pl.BlockSpec((None, FF_TILE, D_MODEL), lambda e, f: (e, f, 0)),
        ],
        out_specs=[rows(rows_a), rows(rows_b)],
        out_shape=[jax.ShapeDtypeStruct(xs_a.shape, _BF16), jax.ShapeDtypeStruct(xs_b.shape, _BF16)],
        compiler_params=_params(2),
        name="ffn",
    )(xs_a, xs_b, w_gate, w_up, w_down)


def _combine_kernel(latent, x_ref, y_ref, gw_ref, mod_ref, o_ref):
    r = 1 + pl.program_id(0) if latent else 0
    sets, rows, tile = gw_ref.shape
    cap = rows // N_EXPERTS
    gate = mod_ref[5, pl.ds(r, 1), :]
    for i in range(sets):
        y = y_ref[:, i * cap:(i + 1) * cap, :].reshape(rows, D_MODEL)
        moe = lax.dot_general(gw_ref[i], y, (((0,), (0,)), ((), ())), preferred_element_type=_F32)
        tok = slice(i * tile, (i + 1) * tile)
        o_ref[tok, :] = x_ref[tok, :] + gate * moe


def _combine(x_new, y, gw, mod, latent, seq, step_tokens):
    n_tok = x_new.shape[0]
    n_sets = n_tok // seq
    cap = EC_CAPACITY_FACTOR * seq // N_EXPERTS
    sets = max(1, step_tokens // seq)
    tile = step_tokens // sets
    per_set = seq // tile
    tokens = pl.BlockSpec((step_tokens, D_MODEL), lambda s, t: (s * per_set + t, 0))
    return pl.pallas_call(
        functools.partial(_combine_kernel, latent),
        grid=(n_sets // sets, per_set),
        in_specs=[
            tokens,
            pl.BlockSpec((N_EXPERTS, sets * cap, D_MODEL), lambda s, t: (0, s, 0)),
            pl.BlockSpec((sets, N_EXPERTS * cap, tile), lambda s, t: (s, 0, t)),
            _resident((N_MOD, COND_ROWS, D_MODEL)),
        ],
        out_specs=tokens,
        out_shape=jax.ShapeDtypeStruct((n_tok, D_MODEL), _F32),
        compiler_params=_params(2),
        name="combine_lat" if latent else "combine_ctx",
    )(x_new, y, gw, mod)


def kernel(x_prompt, x_sample, cache_k, cache_v, c, c_ctx, w_ada, b_ada, g_norm1, w_in, g_q, g_k, conv_w,
           g_attn_out, g_conv_out, w_out, g_norm2, w_router, w_gate, w_up, w_down):
    batch, seq, _ = x_prompt.shape
    dec_batch, dec_seq, _ = x_sample.shape
    depth = w_ada.shape[0]
    assert 1 + dec_batch <= COND_ROWS

    cond = jnp.concatenate([c_ctx[None, :], c, jnp.zeros((COND_ROWS - 1 - dec_batch, D_MODEL), _F32)], axis=0)
    xp = x_prompt.reshape(batch * seq, D_MODEL)
    xl = x_sample.reshape(dec_batch * dec_seq, D_MODEL)
    past = cache_k.shape[2]

    new_k, new_v = [], []
    for l in range(depth):
        mod = _ada(cond, w_ada[l], b_ada[l])
        w_router_t = w_router[l].T

        q_p, k_p, v_p, bg_p, cu_p, kstate, vstate, w_in_b, w_out_b = _inproj(
            xp, mod, g_norm1[l], w_in[l], g_q[l], g_k[l], False, seq, w_out[l])
        q_l, k_l, v_l, bg_l, cu_l = _inproj(xl, mod, g_norm1[l], w_in_b, g_q[l], g_k[l], True, dec_seq)

        shared = (mod, conv_w[l], g_attn_out[l], g_conv_out[l], w_out_b, g_norm2[l], w_router_t)
        kv_p = [(k_p.reshape(batch, seq, D_KV), v_p.reshape(batch, seq, D_KV))]
        kv_l = [(cache_k[:, l].reshape(dec_batch, past, D_KV), cache_v[:, l].reshape(dec_batch, past, D_KV)),
                (k_l.reshape(dec_batch, dec_seq, D_KV), v_l.reshape(dec_batch, dec_seq, D_KV))]
        xp1, h2_p, lg_p = _mix(xp, q_p, kv_p, bg_p, cu_p, *shared, False, seq)
        xl1, h2_l, lg_l = _mix(xl, q_l, kv_l, bg_l, cu_l, *shared, True, dec_seq)

        xs_p, gw_p = _route(lg_p, h2_p, seq, CONTEXT_ROUTE_SETS, N_EXPERTS)
        xs_l, gw_l = _route(lg_l, h2_l, dec_seq, 1, LATENT_ROUTE_EXPERTS)
        y_p, y_l = _ffn(xs_p, xs_l, w_gate[l], w_up[l], w_down[l])
        xp = _combine(xp1, y_p, gw_p, mod, False, seq, CONTEXT_COMBINE_TOKENS)
        xl = _combine(xl1, y_l, gw_l, mod, True, dec_seq, LATENT_COMBINE_TOKENS)

        new_k.append(kstate.reshape(batch, seq, N_KV_HEADS, HEAD_DIM))
        new_v.append(vstate.reshape(batch, seq, N_KV_HEADS, HEAD_DIM))

    return (xp.reshape(batch, seq, D_MODEL), xl.reshape(dec_batch, dec_seq, D_MODEL),
            jnp.stack(new_k, axis=1), jnp.stack(new_v, axis=1))
```

```python
import functools

import jax
import jax.numpy as jnp
from jax import lax
from jax.experimental import pallas as pl
from jax.experimental.pallas import tpu as pltpu

D_MODEL = 2048
GRID_W = 64
HEAD_DIM = 128
D_ATTN = 1024
D_CONV = 1024
N_HEADS = 8
N_KV_HEADS = 2
GQA_GROUP = N_HEADS // N_KV_HEADS
D_KV = N_KV_HEADS * HEAD_DIM
N_EXPERTS = 16
EC_CAPACITY_FACTOR = 2
D_FF = 1024
ROPE_THETA = 10000.0
RMS_EPS = 1e-6
LOG2_E = 1.4426950408889634
N_MOD = 6
N_MOD_EARLY = 2
N_MOD_LATE = N_MOD - N_MOD_EARLY
D_IN_PROJ = D_ATTN + 2 * D_KV + 3 * D_CONV

CONTEXT_MIX_TOKENS = 512
LATENT_MIX_TOKENS = 256
INPROJ_TILE = 512
LATENT_INPROJ_TILE = 256
ADA_COL_TILE = 1024
FF_TILE = 512
FFN_ROW_CHUNK = 512
COND_ROWS = 8
HALO_ROWS = 8
LATENT_ROUTE_EXPERTS = 8
CONTEXT_ROUTE_SETS = 4
CONTEXT_HEADS_STACKED = 2
VMEM_LIMIT = 56 * 1024 * 1024
LANES = 128
STAGE_ROWS = 32
STAGE_SLOTS = 8
CONTEXT_COMBINE_TOKENS = 1024
LATENT_COMBINE_TOKENS = 512

_BF16 = jnp.bfloat16
_F32 = jnp.float32


def _params(n_axes):
    return pltpu.CompilerParams(dimension_semantics=("arbitrary",) * n_axes,
                                vmem_limit_bytes=VMEM_LIMIT)


def _resident(shape):
    return pl.BlockSpec(shape, lambda *_: (0,) * len(shape), pipeline_mode=pl.Buffered(1))


def _dot(a, b):
    return jnp.dot(a, b, preferred_element_type=_F32)


def _dot_nt(a, b):
    return lax.dot_general(a, b, (((1,), (1,)), ((), ())), preferred_element_type=_F32)


def _rms(x):
    return x * lax.rsqrt(jnp.mean(x * x, axis=-1, keepdims=True) + RMS_EPS)


def _split_bf16(x):
    hi = x.astype(_BF16)
    lo = (x - hi.astype(_F32)).astype(_BF16)
    return hi, lo


def _weight_scratch(rows, cols):
    return [pltpu.VMEM((rows, cols), _BF16), pltpu.VMEM((STAGE_SLOTS, STAGE_ROWS, cols), _F32),
            pltpu.SemaphoreType.DMA((STAGE_SLOTS,)), pltpu.SemaphoreType.DMA(())]


def _cast_weight_once(w_hbm, w_pub, w_scr, stage, sems, pub_sem, step, n_steps):
    n_chunks = w_scr.shape[0] // STAGE_ROWS

    def fetch(c):
        slot = c % STAGE_SLOTS
        return pltpu.make_async_copy(w_hbm.at[pl.ds(c * STAGE_ROWS, STAGE_ROWS), :], stage.at[slot], sems.at[slot])

    publish = pltpu.make_async_copy(w_scr, w_pub, pub_sem)

    @pl.when(step == 0)
    def _():
        for c in range(STAGE_SLOTS):
            fetch(c).start()
        for c in range(n_chunks):
            fetch(c).wait()
            w_scr[c * STAGE_ROWS:(c + 1) * STAGE_ROWS, :] = stage[c % STAGE_SLOTS].astype(_BF16)
            if c + STAGE_SLOTS < n_chunks:
                fetch(c + STAGE_SLOTS).start()
        publish.start()

    @pl.when(step == n_steps - 1)
    def _():
        publish.wait()


def _stream_scratch(rows, cols, n_steps):
    chunk = rows // n_steps
    return [pltpu.VMEM((2, chunk, cols), _F32), pltpu.VMEM((2, chunk, cols), _BF16),
            pltpu.SemaphoreType.DMA((2,)), pltpu.SemaphoreType.DMA((2,))]


def _cast_weight_alongside(w_hbm, w_pub, stage, stage_b, in_sems, out_sems, step, n_steps):
    chunk = stage.shape[1]
    slot = step % 2

    def fetch(k, sl):
        return pltpu.make_async_copy(w_hbm.at[pl.ds(k * chunk, chunk), :], stage.at[sl], in_sems.at[sl])

    def push(k, sl):
        return pltpu.make_async_copy(stage_b.at[sl], w_pub.at[pl.ds(k * chunk, chunk), :], out_sems.at[sl])

    @pl.when(step == 0)
    def _():
        fetch(0, 0).start()

    @pl.when(step + 1 < n_steps)
    def _():
        fetch(step + 1, 1 - slot).start()

    fetch(step, slot).wait()

    @pl.when(step >= 2)
    def _():
        push(step - 2, slot).wait()

    stage_b[slot] = stage[slot].astype(_BF16)
    push(step, slot).start()

    @pl.when(step == n_steps - 1)
    def _():
        push(step, slot).wait()
        push(step - 1, 1 - slot).wait()


def _ada_kernel(c_ref, w_ref, b_ref, o_ref):
    c = c_ref[...]
    a = c * jax.nn.sigmoid(c)
    a_hi, a_lo = _split_bf16(a)
    w = w_ref[...]
    w_hi, w_lo = _split_bf16(w)
    o_ref[...] = _dot(a_hi, w_hi) + (_dot(a_lo, w_hi) + _dot(a_hi, w_lo)) + b_ref[...]


def _late_modulation(cond_ref, wada_refs, bada_ref, late_ref, acc, step, n_steps):
    @pl.when(step == 0)
    def _():
        acc[...] = jnp.zeros_like(acc)

    c = cond_ref[...]
    a_hi, a_lo = _split_bf16(c * jax.nn.sigmoid(c))
    for j, w_ref in enumerate(wada_refs):
        w = w_ref[...].astype(_BF16)
        acc[j] += _dot(a_hi, w) + _dot(a_lo, w)

    @pl.when(step == n_steps - 1)
    def _():
        for j in range(len(wada_refs)):
            cols = slice((N_MOD_EARLY + j) * D_MODEL, (N_MOD_EARLY + j + 1) * D_MODEL)
            late_ref[j] = acc[j] + bada_ref[:, cols]


def _ada(cond, w_ada, b_ada):
    per_mod = D_MODEL // ADA_COL_TILE
    return pl.pallas_call(
        _ada_kernel,
        grid=(N_MOD_EARLY, per_mod),
        in_specs=[
            pl.BlockSpec((COND_ROWS, D_MODEL), lambda j, h: (0, 0)),
            pl.BlockSpec((D_MODEL, ADA_COL_TILE), lambda j, h: (0, j * per_mod + h)),
            pl.BlockSpec((1, ADA_COL_TILE), lambda j, h: (0, j * per_mod + h)),
        ],
        out_specs=pl.BlockSpec((None, COND_ROWS, ADA_COL_TILE), lambda j, h: (j, 0, h)),
        out_shape=jax.ShapeDtypeStruct((N_MOD_EARLY, COND_ROWS, D_MODEL), _F32),
        compiler_params=_params(2),
        name="ada",
    )(cond, w_ada, b_ada.reshape(1, N_MOD * D_MODEL))


def _rope_tables(seq):
    t = jnp.arange(seq)
    row = (t // GRID_W).astype(_F32)
    col = (t % GRID_W).astype(_F32)
    half = HEAD_DIM // 2
    inv_freq = ROPE_THETA ** (-jnp.arange(0, half, 2, dtype=_F32) / half)
    ang_r = row[:, None] * inv_freq[None, :]
    ang_c = col[:, None] * inv_freq[None, :]
    cos = jnp.concatenate([jnp.cos(ang_r)] * 2 + [jnp.cos(ang_c)] * 2, axis=-1)
    sin = jnp.concatenate([-jnp.sin(ang_r), jnp.sin(ang_r), -jnp.sin(ang_c), jnp.sin(ang_c)], axis=-1)
    return cos, sin


def _cond_row(latent, seq_tiles):
    if not latent:
        return 0
    return 1 + pl.program_id(0) // seq_tiles


def _inproj_kernel(latent, seq_tiles, x_ref, mod_ref, g1_ref, w_ref, gq_ref, gk_ref, *rest):
    if latent:
        cos_ref, sin_ref, cond_ref, *rest = rest
        wada_refs, rest = rest[:N_MOD_LATE], rest[N_MOD_LATE:]
        bada_ref, q_ref, k_ref, v_ref, bg_ref, cu_ref, late_ref, late_acc = rest
        _late_modulation(cond_ref, wada_refs, bada_ref, late_ref, late_acc, pl.program_id(0), pl.num_programs(0))
    else:
        (wo_hbm, q_ref, k_ref, v_ref, bg_ref, cu_ref, kstate_ref, vstate_ref, w_pub, wo_pub,
         *scratch) = rest
        cast_scratch, stream_scratch = scratch[:4], scratch[4:]
        _cast_weight_once(w_ref, w_pub, *cast_scratch, pl.program_id(0), pl.num_programs(0))
        w_ref = cast_scratch[0]
        _cast_weight_alongside(wo_hbm, wo_pub, *stream_scratch, pl.program_id(0), pl.num_programs(0))
    r = _cond_row(latent, seq_tiles)
    sh = mod_ref[0, pl.ds(r, 1), :]
    sc = mod_ref[1, pl.ds(r, 1), :]
    h = _rms(x_ref[...]) * (g1_ref[...] * (1.0 + sc)) + sh
    hb = h.astype(_BF16)

    if latent:
        cos = cos_ref[...]
        sin = sin_ref[...]
        lane = lax.broadcasted_iota(jnp.int32, cos.shape, 1)
        first = (lane % (HEAD_DIM // 2)) < (HEAD_DIM // 4)

    def head_norm(xh, g):
        xh = _rms(xh) * g
        if latent:
            partner = jnp.where(first, pltpu.roll(xh, HEAD_DIM - HEAD_DIM // 4, axis=1),
                                pltpu.roll(xh, HEAD_DIM // 4, axis=1))
            xh = xh * cos + partner * sin
        return xh

    gq = gq_ref[...] * (HEAD_DIM ** -0.5 * LOG2_E)
    q = _dot(hb, w_ref[:, 0:D_ATTN])
    for hd in range(N_HEADS):
        cols = slice(hd * HEAD_DIM, (hd + 1) * HEAD_DIM)
        q_ref[:, cols] = head_norm(q[:, cols], gq).astype(q_ref.dtype)
    k = _dot(hb, w_ref[:, D_ATTN:D_ATTN + D_KV])
    o = D_ATTN + D_KV
    v = _dot(hb, w_ref[:, o:o + D_KV])
    for hd in range(N_KV_HEADS):
        cols = slice(hd * HEAD_DIM, (hd + 1) * HEAD_DIM)
        kh = head_norm(k[:, cols], gk_ref[...])
        k_ref[:, cols] = kh
        v_ref[:, cols] = v[:, cols]
        if not latent:
            kstate_ref[:, hd, :] = kh
            vstate_ref[:, hd, :] = v[:, cols]
    o += D_KV
    bg_ref[...] = _dot(hb, w_ref[:, o:o + D_CONV])
    o += D_CONV
    cg = _dot(hb, w_ref[:, o:o + D_CONV])
    o += D_CONV
    cu_ref[...] = cg * _dot(hb, w_ref[:, o:o + D_CONV])


def _inproj(x, mod, g_norm1, w_in, g_q, g_k, latent, seq, w_out=None, ada=None):
    n_tok = x.shape[0]
    rows = LATENT_INPROJ_TILE if latent else INPROJ_TILE
    seq_tiles = seq // rows
    n_steps = n_tok // rows
    tile = lambda w: pl.BlockSpec((rows, w), lambda i: (i, 0))
    in_specs = [
        tile(D_MODEL),
        _resident((N_MOD_EARLY, COND_ROWS, D_MODEL)),
        _resident((1, D_MODEL)),
        _resident((D_MODEL, D_IN_PROJ)) if latent else pl.BlockSpec(memory_space=pl.ANY),
        _resident((1, HEAD_DIM)),
        _resident((1, HEAD_DIM)),
    ]
    args = [x, mod, g_norm1.reshape(1, D_MODEL), w_in, g_q.reshape(1, HEAD_DIM), g_k.reshape(1, HEAD_DIM)]
    if latent:
        cos, sin = _rope_tables(seq)
        rope_spec = pl.BlockSpec((rows, HEAD_DIM), lambda i: (i % seq_tiles, 0))
        in_specs += [rope_spec, rope_spec]
        args += [cos, sin]
        cond, w_ada, b_ada = ada
        k_rows = D_MODEL // n_steps
        assert k_rows % LANES == 0
        in_specs += [pl.BlockSpec((COND_ROWS, k_rows), lambda i: (0, i))]
        in_specs += [pl.BlockSpec((k_rows, D_MODEL), lambda i, j=j: (i, N_MOD_EARLY + j)) for j in range(N_MOD_LATE)]
        in_specs += [_resident((1, N_MOD * D_MODEL))]
        args += [cond] + [w_ada] * N_MOD_LATE + [b_ada.reshape(1, N_MOD * D_MODEL)]
    else:
        in_specs += [pl.BlockSpec(memory_space=pl.ANY)]
        args += [w_out]
    scratch_shapes = []
    out_specs = [tile(D_ATTN), tile(D_KV), tile(D_KV), tile(D_CONV), tile(D_CONV)]
    out_shape = [
        jax.ShapeDtypeStruct((n_tok, D_ATTN), _BF16),
        jax.ShapeDtypeStruct((n_tok, D_KV), _F32),
        jax.ShapeDtypeStruct((n_tok, D_KV), _F32),
        jax.ShapeDtypeStruct((n_tok, D_CONV), _F32),
        jax.ShapeDtypeStruct((n_tok, D_CONV), _F32),
    ]
    if latent:
        late = (N_MOD_LATE, COND_ROWS, D_MODEL)
        out_specs += [pl.BlockSpec(late, lambda i: (0, 0, 0))]
        out_shape += [jax.ShapeDtypeStruct(late, _F32)]
        scratch_shapes = [pltpu.VMEM(late, _F32)]
    else:
        state_spec = pl.BlockSpec((rows, N_KV_HEADS, HEAD_DIM), lambda i: (i, 0, 0))
        out_specs += [state_spec, state_spec, pl.BlockSpec(memory_space=pl.ANY), pl.BlockSpec(memory_space=pl.ANY)]
        out_shape += [jax.ShapeDtypeStruct((n_tok, N_KV_HEADS, HEAD_DIM), _F32)] * 2
        out_shape += [jax.ShapeDtypeStruct((D_MODEL, D_IN_PROJ), _BF16), jax.ShapeDtypeStruct(w_out.shape, _BF16)]
        assert n_steps >= 2 and w_out.shape[0] % (16 * n_steps) == 0
        scratch_shapes = _weight_scratch(D_MODEL, D_IN_PROJ) + _stream_scratch(*w_out.shape, n_steps)
    return pl.pallas_call(
        functools.partial(_inproj_kernel, latent, seq_tiles),
        grid=(n_steps,),
        in_specs=in_specs,
        out_specs=out_specs,
        out_shape=out_shape,
        scratch_shapes=scratch_shapes,
        compiler_params=_params(1),
        name="inproj_lat" if latent else "inproj_ctx",
    )(*args)


def _mix_kernel(latent, seqs, seq_tiles, n_tiles, x_ref, q_ref, *rest):
    n_kv = 2 if latent else 1
    kv_refs = rest[:2 * n_kv]
    rest = rest[2 * n_kv:]
    bg_ref, cu_ref = rest[:2]
    rest = rest[2:]
    if latent:
        cu_prev_ref, cu_next_ref = rest[:2]
        rest = rest[2:]
    (convw_ref, ga_ref, gc_ref, wout_ref, mod_ref, g2_ref, wr_ref, xo_ref, h2_ref, lg_ref, *rest) = rest
    (mixed_scr,) = rest

    s = pl.program_id(0)

    @pl.when(s == 0)
    def _():
        mixed_scr[...] = jnp.zeros_like(mixed_scr)

    r = 1 + jnp.maximum(s - 1, 0) // seq_tiles if latent else 0
    y = _dot(mixed_scr[...], wout_ref[...])
    x_new = x_ref[...] + mod_ref[0, pl.ds(r, 1), :] * y
    xo_ref[...] = x_new
    h2 = _rms(x_new) * (g2_ref[...] * (1.0 + mod_ref[2, pl.ds(r, 1), :])) + mod_ref[1, pl.ds(r, 1), :]
    h2_hi = h2.astype(_BF16)
    h2_ref[...] = h2_hi
    lg_ref[...] = _dot_nt(wr_ref[...].astype(_BF16), h2_hi)

    rows_per_seq = x_ref.shape[0] // seqs
    row_sum_on_mxu = latent
    stacked = 1 if latent else CONTEXT_HEADS_STACKED
    sub = lax.broadcasted_iota(jnp.int32, (HALO_ROWS, D_CONV), 0)
    for i in range(seqs):
        rs = slice(i * rows_per_seq, (i + 1) * rows_per_seq)

        cu = cu_ref[rs, :]
        if latent:
            pos = jnp.minimum(s, n_tiles - 1) % seq_tiles
            edge_prev = jnp.where(pos == 0, 0.0, 1.0) * cu_prev_ref[HALO_ROWS - 1:HALO_ROWS, :]
            edge_next = jnp.where(pos == seq_tiles - 1, 0.0, 1.0) * cu_next_ref[0:1, :]
        else:
            edge_prev = jnp.zeros((1, D_CONV), _F32)
            edge_next = edge_prev
        prev = pltpu.roll(cu, 1, axis=0)
        nxt = pltpu.roll(cu, rows_per_seq - 1, axis=0)
        prev = jnp.concatenate([jnp.where(sub == 0, edge_prev, prev[0:HALO_ROWS]), prev[HALO_ROWS:]], axis=0)
        nxt = jnp.concatenate([nxt[:rows_per_seq - HALO_ROWS],
                               jnp.where(sub == HALO_ROWS - 1, edge_next, nxt[rows_per_seq - HALO_ROWS:])], axis=0)
        conv = bg_ref[rs, :] * (prev * convw_ref[0:1, :] + cu * convw_ref[1:2, :] + nxt * convw_ref[2:3, :])
        mixed_scr[rs, D_ATTN:D_ATTN + D_CONV] = (_rms(conv) * gc_ref[...]).astype(_BF16)

        heads = []
        for kvh in range(N_KV_HEADS):
            cols = slice(kvh * HEAD_DIM, (kvh + 1) * HEAD_DIM)
            if latent:
                ks = [kv_refs[2 * p][:, cols].astype(_BF16) for p in range(n_kv)]
                vs = [kv_refs[2 * p + 1][:, cols].astype(_BF16) for p in range(n_kv)]
            else:
                ks = [kv_refs[0][i, :, cols].astype(_BF16)]
                vs = [kv_refs[1][i, :, cols].astype(_BF16)]
            if row_sum_on_mxu:
                vs = [jnp.concatenate([vp, jnp.ones_like(vp)], axis=1) for vp in vs]
            for g0 in range(0, GQA_GROUP, stacked):
                first = kvh * GQA_GROUP + g0
                qh = jnp.concatenate([q_ref[rs, (first + j) * HEAD_DIM:(first + j + 1) * HEAD_DIM]
                                      for j in range(stacked)], axis=0)
                sc = [_dot_nt(qh, kp) for kp in ks]
                m = functools.reduce(jnp.maximum, [jnp.max(sp, axis=-1, keepdims=True) for sp in sc])
                if row_sum_on_mxu:
                    ol = functools.reduce(jnp.add,
                                          [_dot(jnp.exp2(sp - m).astype(_BF16), vp) for sp, vp in zip(sc, vs)])
                    o = ol[:, 0:HEAD_DIM] / ol[:, HEAD_DIM:HEAD_DIM + 1]
                else:
                    p = [jnp.exp2(sp - m) for sp in sc]
                    l = functools.reduce(jnp.add, [jnp.sum(pp, axis=-1, keepdims=True) for pp in p])
                    o = functools.reduce(jnp.add, [_dot(pp.astype(_BF16), vp) for pp, vp in zip(p, vs)]) / l
                heads += [o[j * rows_per_seq:(j + 1) * rows_per_seq] for j in range(stacked)]
        ssq = functools.reduce(jnp.add, [jnp.sum(h * h, axis=-1, keepdims=True) for h in heads])
        inv = lax.rsqrt(ssq * (1.0 / D_ATTN) + RMS_EPS)
        for hd, h in enumerate(heads):
            hc = slice(hd * HEAD_DIM, (hd + 1) * HEAD_DIM)
            mixed_scr[rs, hc] = (h * inv * ga_ref[:, hc]).astype(_BF16)


def _mix(x, q, kv_parts, bg, cu, mod, conv_w, g_attn_out, g_conv_out, w_out, g_norm2, w_router_t,
         latent, seq):
    n_tok = x.shape[0]
    tile = LATENT_MIX_TOKENS if latent else CONTEXT_MIX_TOKENS
    seqs = max(1, tile // seq)
    seq_tiles = max(1, seq // tile)
    n_tiles = n_tok // tile
    started = lambda i: jnp.minimum(i, n_tiles - 1)
    finished = lambda i: jnp.maximum(i - 1, 0)
    start_tile = lambda w: pl.BlockSpec((tile, w), lambda i: (started(i), 0))
    finish_tile = lambda w: pl.BlockSpec((tile, w), lambda i: (finished(i), 0))
    in_specs = [finish_tile(D_MODEL), start_tile(D_ATTN)]
    args = [x, q]
    for k, v in kv_parts:
        if latent:
            spec = pl.BlockSpec((None, k.shape[1], D_KV), lambda i: (started(i) // seq_tiles, 0, 0))
        else:
            spec = pl.BlockSpec((seqs, k.shape[1], D_KV), lambda i: (started(i), 0, 0))
        in_specs += [spec, spec]
        args += [k, v]
    in_specs += [start_tile(D_CONV), start_tile(D_CONV)]
    args += [bg, cu]
    if latent:
        per_tile = tile // HALO_ROWS
        last = n_tok // HALO_ROWS - 1
        in_specs += [
            pl.BlockSpec((HALO_ROWS, D_CONV), lambda i: (jnp.maximum(started(i) * per_tile - 1, 0), 0)),
            pl.BlockSpec((HALO_ROWS, D_CONV), lambda i: (jnp.minimum((started(i) + 1) * per_tile, last), 0)),
        ]
        args += [cu, cu]
    in_specs += [
        _resident((3, D_CONV)),
        _resident((1, D_ATTN)),
        _resident((1, D_CONV)),
        _resident((D_ATTN + D_CONV, D_MODEL)),
        _resident((N_MOD_LATE, COND_ROWS, D_MODEL)),
        _resident((1, D_MODEL)),
        _resident((N_EXPERTS, D_MODEL)),
    ]
    args += [conv_w, g_attn_out.reshape(1, D_ATTN), g_conv_out.reshape(1, D_CONV), w_out, mod,
             g_norm2.reshape(1, D_MODEL), w_router_t]
    out_specs = [finish_tile(D_MODEL), finish_tile(D_MODEL),
                 pl.BlockSpec((N_EXPERTS, tile), lambda i: (0, finished(i)))]
    out_shape = [
        jax.ShapeDtypeStruct((n_tok, D_MODEL), _F32),
        jax.ShapeDtypeStruct((n_tok, D_MODEL), _BF16),
        jax.ShapeDtypeStruct((N_EXPERTS, n_tok), _F32),
    ]
    scratch_shapes = [pltpu.VMEM((tile, D_ATTN + D_CONV), _BF16)]
    return pl.pallas_call(
        functools.partial(_mix_kernel, latent, seqs, seq_tiles, n_tiles),
        grid=(n_tiles + 1,),
        in_specs=in_specs,
        out_specs=out_specs,
        out_shape=out_shape,
        scratch_shapes=scratch_shapes,
        compiler_params=_params(1),
        name="mix_lat" if latent else "mix_ctx",
    )(*args)


def _route_kernel(seq, cap, sets, e_blk, lg_ref, h2_ref, xs_ref, gw_ref, aff_scr, key_scr):
    j = pl.program_id(1)

    @pl.when(j == 0)
    def _():
        other = lax.broadcasted_iota(jnp.int32, (seq, seq), 0)
        token = lax.broadcasted_iota(jnp.int32, (seq, seq), 1)
        earlier = jnp.where(other < token, 1.0, 0.0)
        tie = earlier[0:LANES, 0:LANES]
        before = earlier.astype(_BF16)
        for s in range(sets):
            lg = lg_ref[:, s * seq:(s + 1) * seq]
            ex = jnp.exp(lg - jnp.max(lg, axis=0, keepdims=True))
            aff = ex / jnp.sum(ex, axis=0, keepdims=True)
            aff_t = jnp.concatenate([aff, jnp.zeros((LANES - N_EXPERTS, seq), _F32)], axis=0).T
            ranks = []
            for e in range(N_EXPERTS):
                theirs = aff_t[:, e:e + 1]
                blocks = []
                for lo in range(0, seq, LANES):
                    hi = lo + LANES
                    mine = aff[e:e + 1, lo:hi]
                    diag = theirs[lo:hi]
                    cnt = jnp.sum(jnp.where(diag > mine, 1.0, jnp.where(diag == mine, tie, 0.0)),
                                  axis=0, keepdims=True)
                    if lo > 0:
                        cnt += jnp.sum(jnp.where(theirs[0:lo] >= mine, 1.0, 0.0), axis=0, keepdims=True)
                    if hi < seq:
                        cnt += jnp.sum(jnp.where(theirs[hi:seq] > mine, 1.0, 0.0), axis=0, keepdims=True)
                    blocks.append(cnt)
                ranks.append(jnp.concatenate(blocks, axis=1))
            rank = jnp.concatenate(ranks, axis=0)
            sel = rank < float(cap)
            slot = _dot(jnp.where(sel, 1.0, 0.0).astype(_BF16), before)
            rows = slice(s * N_EXPERTS, (s + 1) * N_EXPERTS)
            aff_scr[rows, :] = aff
            key_scr[rows, :] = jnp.where(sel, slot, -1.0)

    c_idx = lax.broadcasted_iota(jnp.int32, (cap, seq), 0).astype(_F32)
    for s in range(sets):
        onehot, gated = [], []
        for i in range(e_blk):
            e = s * N_EXPERTS + j * e_blk + i
            hit = c_idx == key_scr[pl.ds(e, 1), :]
            onehot.append(jnp.where(hit, 1.0, 0.0))
            gated.append(jnp.where(hit, aff_scr[pl.ds(e, 1), :], 0.0))
        xs = _dot(jnp.concatenate(onehot, axis=0).astype(_BF16), h2_ref[s * seq:(s + 1) * seq, :]).astype(xs_ref.dtype)
        for i in range(e_blk):
            xs_ref[i, s * cap:(s + 1) * cap, :] = xs[i * cap:(i + 1) * cap, :]
        gw_ref[s] = jnp.concatenate(gated, axis=0).astype(gw_ref.dtype)


def _route(logits_t, h2, seq, sets, e_blk):
    n_tok = h2.shape[0]
    n_sets = n_tok // seq
    cap = EC_CAPACITY_FACTOR * seq // N_EXPERTS
    return pl.pallas_call(
        functools.partial(_route_kernel, seq, cap, sets, e_blk),
        grid=(n_sets // sets, N_EXPERTS // e_blk),
        in_specs=[
            pl.BlockSpec((N_EXPERTS, sets * seq), lambda s, j: (0, s)),
            pl.BlockSpec((sets * seq, D_MODEL), lambda s, j: (s, 0)),
        ],
        out_specs=[
            pl.BlockSpec((e_blk, sets * cap, D_MODEL), lambda s, j: (j, s, 0)),
            pl.BlockSpec((sets, e_blk * cap, seq), lambda s, j: (s, j, 0)),
        ],
        out_shape=[
            jax.ShapeDtypeStruct((N_EXPERTS, n_sets * cap, D_MODEL), _BF16),
            jax.ShapeDtypeStruct((n_sets, N_EXPERTS * cap, seq), _BF16),
        ],
        scratch_shapes=[pltpu.VMEM((sets * N_EXPERTS, seq), _F32), pltpu.VMEM((sets * N_EXPERTS, seq), _F32)],
        compiler_params=_params(2),
        name="route_%d" % seq,
    )(logits_t, h2)


def _ffn_kernel(xa_ref, xb_ref, wg_ref, wu_ref, wd_ref, ya_ref, yb_ref):
    def run(first):
        wg = wg_ref[...].astype(_BF16)
        wu = wu_ref[...].astype(_BF16)
        wd = wd_ref[...].astype(_BF16)
        for x_ref, y_ref in ((xa_ref, ya_ref), (xb_ref, yb_ref)):
            chunk = min(FFN_ROW_CHUNK, x_ref.shape[0])
            for c in range(x_ref.shape[0] // chunk):
                rows = slice(c * chunk, (c + 1) * chunk)
                x = x_ref[rows, :]
                a = _dot(x, wg)
                u = _dot(x, wu)
                total = _dot((a * jax.nn.sigmoid(a) * u).astype(_BF16), wd)
                if not first:
                    total = y_ref[rows, :].astype(_F32) + total
                y_ref[rows, :] = total.astype(y_ref.dtype)

    @pl.when(pl.program_id(1) == 0)
    def _():
        run(True)

    @pl.when(pl.program_id(1) > 0)
    def _():
        run(False)


def _ffn(xs_a, xs_b, w_gate, w_up, w_down):
    rows_a, rows_b = xs_a.shape[1], xs_b.shape[1]
    rows = lambda n: pl.BlockSpec((None, n, D_MODEL), lambda e, f: (e, 0, 0))
    return pl.pallas_call(
        _ffn_kernel,
        grid=(N_EXPERTS, D_FF // FF_TILE),
        in_specs=[
            rows(rows_a), rows(rows_b),
            pl.BlockSpec((None, D_MODEL, FF_TILE), lambda e, f: (e, 0, f)),
            pl.BlockSpec((None, D_MODEL, FF_TILE), lambda e, f: (e, 0, f)),
            pl.BlockSpec((None, FF_TILE, D_MODEL), lambda e, f: (e, f, 0)),
        ],
        out_specs=[rows(rows_a), rows(rows_b)],
        out_shape=[jax.ShapeDtypeStruct(xs_a.shape, _BF16), jax.ShapeDtypeStruct(xs_b.shape, _BF16)],
        compiler_params=_params(2),
        name="ffn",
    )(xs_a, xs_b, w_gate, w_up, w_down)


def _combine_kernel(latent, x_ref, y_ref, gw_ref, mod_ref, o_ref):
    r = 1 + pl.program_id(0) if latent else 0
    sets, rows, tile = gw_ref.shape
    cap = rows // N_EXPERTS
    gate = mod_ref[3, pl.ds(r, 1), :]
    for i in range(sets):
        y = y_ref[:, i * cap:(i + 1) * cap, :].reshape(rows, D_MODEL)
        moe = lax.dot_general(gw_ref[i], y, (((0,), (0,)), ((), ())), preferred_element_type=_F32)
        tok = slice(i * tile, (i + 1) * tile)
        o_ref[tok, :] = x_ref[tok, :] + gate * moe


def _combine(x_new, y, gw, mod, latent, seq, step_tokens):
    n_tok = x_new.shape[0]
    n_sets = n_tok // seq
    cap = EC_CAPACITY_FACTOR * seq // N_EXPERTS
    sets = max(1, step_tokens // seq)
    tile = step_tokens // sets
    per_set = seq // tile
    tokens = pl.BlockSpec((step_tokens, D_MODEL), lambda s, t: (s * per_set + t, 0))
    return pl.pallas_call(
        functools.partial(_combine_kernel, latent),
        grid=(n_sets // sets, per_set),
        in_specs=[
            tokens,
            pl.BlockSpec((N_EXPERTS, sets * cap, D_MODEL), lambda s, t: (0, s, 0)),
            pl.BlockSpec((sets, N_EXPERTS * cap, tile), lambda s, t: (s, 0, t)),
            _resident((N_MOD_LATE, COND_ROWS, D_MODEL)),
        ],
        out_specs=tokens,
        out_shape=jax.ShapeDtypeStruct((n_tok, D_MODEL), _F32),
        compiler_params=_params(2),
        name="combine_lat" if latent else "combine_ctx",
    )(x_new, y, gw, mod)


def kernel(x_prompt, x_sample, cache_k, cache_v, c, c_ctx, w_ada, b_ada, g_norm1, w_in, g_q, g_k, conv_w,
           g_attn_out, g_conv_out, w_out, g_norm2, w_router, w_gate, w_up, w_down):
    batch, seq, _ = x_prompt.shape
    dec_batch, dec_seq, _ = x_sample.shape
    depth = w_ada.shape[0]
    assert 1 + dec_batch <= COND_ROWS

    cond = jnp.concatenate([c_ctx[None, :], c, jnp.zeros((COND_ROWS - 1 - dec_batch, D_MODEL), _F32)], axis=0)
    xp = x_prompt.reshape(batch * seq, D_MODEL)
    xl = x_sample.reshape(dec_batch * dec_seq, D_MODEL)
    past = cache_k.shape[2]

    new_k, new_v = [], []
    for l in range(depth):
        mod = _ada(cond, w_ada[l], b_ada[l])
        w_router_t = w_router[l].T

        q_p, k_p, v_p, bg_p, cu_p, kstate, vstate, w_in_b, w_out_b = _inproj(
            xp, mod, g_norm1[l], w_in[l], g_q[l], g_k[l], False, seq, w_out[l])
        q_l, k_l, v_l, bg_l, cu_l, mod_late = _inproj(xl, mod, g_norm1[l], w_in_b, g_q[l], g_k[l], True, dec_seq,
                                                      ada=(cond, w_ada[l], b_ada[l]))

        shared = (mod_late, conv_w[l], g_attn_out[l], g_conv_out[l], w_out_b, g_norm2[l], w_router_t)
        kv_p = [(k_p.reshape(batch, seq, D_KV), v_p.reshape(batch, seq, D_KV))]
        kv_l = [(cache_k[:, l].reshape(dec_batch, past, D_KV), cache_v[:, l].reshape(dec_batch, past, D_KV)),
                (k_l.reshape(dec_batch, dec_seq, D_KV), v_l.reshape(dec_batch, dec_seq, D_KV))]
        xp1, h2_p, lg_p = _mix(xp, q_p, kv_p, bg_p, cu_p, *shared, False, seq)
        xl1, h2_l, lg_l = _mix(xl, q_l, kv_l, bg_l, cu_l, *shared, True, dec_seq)

        xs_p, gw_p = _route(lg_p, h2_p, seq, CONTEXT_ROUTE_SETS, N_EXPERTS)
        xs_l, gw_l = _route(lg_l, h2_l, dec_seq, 1, LATENT_ROUTE_EXPERTS)
        y_p, y_l = _ffn(xs_p, xs_l, w_gate[l], w_up[l], w_down[l])
        xp = _combine(xp1, y_p, gw_p, mod_late, False, seq, CONTEXT_COMBINE_TOKENS)
        xl = _combine(xl1, y_l, gw_l, mod_late, True, dec_seq, LATENT_COMBINE_TOKENS)

        new_k.append(kstate.reshape(batch, seq, N_KV_HEADS, HEAD_DIM))
        new_v.append(vstate.reshape(batch, seq, N_KV_HEADS, HEAD_DIM))

    return (xp.reshape(batch, seq, D_MODEL), xl.reshape(dec_batch, dec_seq, D_MODEL),
            jnp.stack(new_k, axis=1), jnp.stack(new_v, axis=1))
```

```python
import functools

import jax
import jax.numpy as jnp
from jax import lax
from jax.experimental import pallas as pl
from jax.experimental.pallas import tpu as pltpu

D_MODEL = 2048
GRID_W = 64
HEAD_DIM = 128
D_ATTN = 1024
D_CONV = 1024
N_HEADS = 8
N_KV_HEADS = 2
GQA_GROUP = N_HEADS // N_KV_HEADS
D_KV = N_KV_HEADS * HEAD_DIM
N_EXPERTS = 16
EC_CAPACITY_FACTOR = 2
D_FF = 1024
ROPE_THETA = 10000.0
RMS_EPS = 1e-6
LOG2_E = 1.4426950408889634
N_MOD = 6
N_MOD_EARLY = 2
N_MOD_LATE = N_MOD - N_MOD_EARLY
D_IN_PROJ = D_ATTN + 2 * D_KV + 3 * D_CONV

CONTEXT_MIX_TOKENS = 512
LATENT_MIX_TOKENS = 256
INPROJ_TILE = 512
LATENT_INPROJ_TILE = 256
ADA_COL_TILE = 1024
FF_TILE = 512
FFN_ROW_CHUNK = 512
COND_ROWS = 8
HALO_ROWS = 8
LATENT_ROUTE_EXPERTS = 8
CONTEXT_ROUTE_SETS = 4
CONTEXT_HEADS_STACKED = 2
VMEM_LIMIT = 56 * 1024 * 1024
LANES = 128
STAGE_ROWS = 32
STAGE_SLOTS = 8
CONTEXT_COMBINE_TOKENS = 1024
LATENT_COMBINE_TOKENS = 512

_BF16 = jnp.bfloat16
_F32 = jnp.float32


def _params(n_axes):
    return pltpu.CompilerParams(dimension_semantics=("arbitrary",) * n_axes,
                                vmem_limit_bytes=VMEM_LIMIT)


def _resident(shape):
    return pl.BlockSpec(shape, lambda *_: (0,) * len(shape), pipeline_mode=pl.Buffered(1))


def _dot(a, b):
    return jnp.dot(a, b, preferred_element_type=_F32)


def _dot_nt(a, b):
    return lax.dot_general(a, b, (((1,), (1,)), ((), ())), preferred_element_type=_F32)


def _rms(x):
    return x * lax.rsqrt(jnp.mean(x * x, axis=-1, keepdims=True) + RMS_EPS)


def _split_bf16(x):
    hi = x.astype(_BF16)
    lo = (x - hi.astype(_F32)).astype(_BF16)
    return hi, lo


def _weight_scratch(rows, cols):
    return [pltpu.VMEM((rows, cols), _BF16), pltpu.VMEM((STAGE_SLOTS, STAGE_ROWS, cols), _F32),
            pltpu.SemaphoreType.DMA((STAGE_SLOTS,)), pltpu.SemaphoreType.DMA(())]


def _cast_weight_once(w_hbm, w_pub, w_scr, stage, sems, pub_sem, step, n_steps):
    n_chunks = w_scr.shape[0] // STAGE_ROWS

    def fetch(c):
        slot = c % STAGE_SLOTS
        return pltpu.make_async_copy(w_hbm.at[pl.ds(c * STAGE_ROWS, STAGE_ROWS), :], stage.at[slot], sems.at[slot])

    publish = pltpu.make_async_copy(w_scr, w_pub, pub_sem)

    @pl.when(step == 0)
    def _():
        for c in range(STAGE_SLOTS):
            fetch(c).start()
        for c in range(n_chunks):
            fetch(c).wait()
            w_scr[c * STAGE_ROWS:(c + 1) * STAGE_ROWS, :] = stage[c % STAGE_SLOTS].astype(_BF16)
            if c + STAGE_SLOTS < n_chunks:
                fetch(c + STAGE_SLOTS).start()
        publish.start()

    @pl.when(step == n_steps - 1)
    def _():
        publish.wait()


def _stream_scratch(rows, cols, n_steps):
    chunk = rows // n_steps
    return [pltpu.VMEM((2, chunk, cols), _F32), pltpu.VMEM((2, chunk, cols), _BF16),
            pltpu.SemaphoreType.DMA((2,)), pltpu.SemaphoreType.DMA((2,))]


def _cast_weight_alongside(w_hbm, w_pub, stage, stage_b, in_sems, out_sems, step, n_steps):
    chunk = stage.shape[1]
    slot = step % 2

    def fetch(k, sl):
        return pltpu.make_async_copy(w_hbm.at[pl.ds(k * chunk, chunk), :], stage.at[sl], in_sems.at[sl])

    def push(k, sl):
        return pltpu.make_async_copy(stage_b.at[sl], w_pub.at[pl.ds(k * chunk, chunk), :], out_sems.at[sl])

    @pl.when(step == 0)
    def _():
        fetch(0, 0).start()

    @pl.when(step + 1 < n_steps)
    def _():
        fetch(step + 1, 1 - slot).start()

    fetch(step, slot).wait()

    @pl.when(step >= 2)
    def _():
        push(step - 2, slot).wait()

    stage_b[slot] = stage[slot].astype(_BF16)
    push(step, slot).start()

    @pl.when(step == n_steps - 1)
    def _():
        push(step, slot).wait()
        push(step - 1, 1 - slot).wait()


def _ada_kernel(c_ref, w_ref, b_ref, o_ref):
    c = c_ref[...]
    a = c * jax.nn.sigmoid(c)
    a_hi, a_lo = _split_bf16(a)
    w = w_ref[...]
    w_hi, w_lo = _split_bf16(w)
    o_ref[...] = _dot(a_hi, w_hi) + (_dot(a_lo, w_hi) + _dot(a_hi, w_lo)) + b_ref[...]


def _late_modulation(cond_ref, wada_refs, bada_ref, late_ref, acc, step, n_steps):
    @pl.when(step == 0)
    def _():
        acc[...] = jnp.zeros_like(acc)

    c = cond_ref[...]
    a = jnp.concatenate(_split_bf16(c * jax.nn.sigmoid(c)), axis=0)
    for j, w_ref in enumerate(wada_refs):
        d = _dot(a, w_ref[...].astype(_BF16))
        acc[j] += d[0:COND_ROWS] + d[COND_ROWS:2 * COND_ROWS]

    @pl.when(step == n_steps - 1)
    def _():
        for j in range(len(wada_refs)):
            cols = slice((N_MOD_EARLY + j) * D_MODEL, (N_MOD_EARLY + j + 1) * D_MODEL)
            late_ref[j] = acc[j] + bada_ref[:, cols]


def _ada(cond, w_ada, b_ada):
    per_mod = D_MODEL // ADA_COL_TILE
    return pl.pallas_call(
        _ada_kernel,
        grid=(N_MOD_EARLY, per_mod),
        in_specs=[
            pl.BlockSpec((COND_ROWS, D_MODEL), lambda j, h: (0, 0)),
            pl.BlockSpec((D_MODEL, ADA_COL_TILE), lambda j, h: (0, j * per_mod + h)),
            pl.BlockSpec((1, ADA_COL_TILE), lambda j, h: (0, j * per_mod + h)),
        ],
        out_specs=pl.BlockSpec((None, COND_ROWS, ADA_COL_TILE), lambda j, h: (j, 0, h)),
        out_shape=jax.ShapeDtypeStruct((N_MOD_EARLY, COND_ROWS, D_MODEL), _F32),
        compiler_params=_params(2),
        name="ada",
    )(cond, w_ada, b_ada.reshape(1, N_MOD * D_MODEL))


def _rope_tables(seq):
    t = jnp.arange(seq)
    row = (t // GRID_W).astype(_F32)
    col = (t % GRID_W).astype(_F32)
    half = HEAD_DIM // 2
    inv_freq = ROPE_THETA ** (-jnp.arange(0, half, 2, dtype=_F32) / half)
    ang_r = row[:, None] * inv_freq[None, :]
    ang_c = col[:, None] * inv_freq[None, :]
    cos = jnp.concatenate([jnp.cos(ang_r)] * 2 + [jnp.cos(ang_c)] * 2, axis=-1)
    sin = jnp.concatenate([-jnp.sin(ang_r), jnp.sin(ang_r), -jnp.sin(ang_c), jnp.sin(ang_c)], axis=-1)
    return cos, sin


def _cond_row(latent, seq_tiles):
    if not latent:
        return 0
    return 1 + pl.program_id(0) // seq_tiles


def _inproj_kernel(latent, seq_tiles, x_ref, mod_ref, g1_ref, w_ref, gq_ref, gk_ref, *rest):
    if latent:
        cos_ref, sin_ref, cond_ref, *rest = rest
        wada_refs, rest = rest[:N_MOD_LATE], rest[N_MOD_LATE:]
        bada_ref, q_ref, k_ref, v_ref, bg_ref, cu_ref, late_ref, late_acc = rest
        _late_modulation(cond_ref, wada_refs, bada_ref, late_ref, late_acc, pl.program_id(0), pl.num_programs(0))
    else:
        (wo_hbm, q_ref, k_ref, v_ref, bg_ref, cu_ref, kstate_ref, vstate_ref, w_pub, wo_pub,
         *scratch) = rest
        cast_scratch, stream_scratch = scratch[:4], scratch[4:]
        _cast_weight_once(w_ref, w_pub, *cast_scratch, pl.program_id(0), pl.num_programs(0))
        w_ref = cast_scratch[0]
        _cast_weight_alongside(wo_hbm, wo_pub, *stream_scratch, pl.program_id(0), pl.num_programs(0))
    r = _cond_row(latent, seq_tiles)
    sh = mod_ref[0, pl.ds(r, 1), :]
    sc = mod_ref[1, pl.ds(r, 1), :]
    h = _rms(x_ref[...]) * (g1_ref[...] * (1.0 + sc)) + sh
    hb = h.astype(_BF16)

    if latent:
        cos = cos_ref[...]
        sin = sin_ref[...]
        lane = lax.broadcasted_iota(jnp.int32, cos.shape, 1)
        first = (lane % (HEAD_DIM // 2)) < (HEAD_DIM // 4)

    def head_norm(xh, g):
        xh = _rms(xh) * g
        if latent:
            partner = jnp.where(first, pltpu.roll(xh, HEAD_DIM - HEAD_DIM // 4, axis=1),
                                pltpu.roll(xh, HEAD_DIM // 4, axis=1))
            xh = xh * cos + partner * sin
        return xh

    gq = gq_ref[...] * (HEAD_DIM ** -0.5 * LOG2_E)
    q = _dot(hb, w_ref[:, 0:D_ATTN])
    for hd in range(N_HEADS):
        cols = slice(hd * HEAD_DIM, (hd + 1) * HEAD_DIM)
        q_ref[:, cols] = head_norm(q[:, cols], gq).astype(q_ref.dtype)
    k = _dot(hb, w_ref[:, D_ATTN:D_ATTN + D_KV])
    o = D_ATTN + D_KV
    v = _dot(hb, w_ref[:, o:o + D_KV])
    for hd in range(N_KV_HEADS):
        cols = slice(hd * HEAD_DIM, (hd + 1) * HEAD_DIM)
        kh = head_norm(k[:, cols], gk_ref[...])
        k_ref[:, cols] = kh
        v_ref[:, cols] = v[:, cols]
        if not latent:
            kstate_ref[:, hd, :] = kh
            vstate_ref[:, hd, :] = v[:, cols]
    o += D_KV
    bg_ref[...] = _dot(hb, w_ref[:, o:o + D_CONV])
    o += D_CONV
    cg = _dot(hb, w_ref[:, o:o + D_CONV])
    o += D_CONV
    cu_ref[...] = cg * _dot(hb, w_ref[:, o:o + D_CONV])


def _inproj(x, mod, g_norm1, w_in, g_q, g_k, latent, seq, w_out=None, ada=None):
    n_tok = x.shape[0]
    rows = LATENT_INPROJ_TILE if latent else INPROJ_TILE
    seq_tiles = seq // rows
    n_steps = n_tok // rows
    tile = lambda w: pl.BlockSpec((rows, w), lambda i: (i, 0))
    in_specs = [
        tile(D_MODEL),
        _resident((N_MOD_EARLY, COND_ROWS, D_MODEL)),
        _resident((1, D_MODEL)),
        _resident((D_MODEL, D_IN_PROJ)) if latent else pl.BlockSpec(memory_space=pl.ANY),
        _resident((1, HEAD_DIM)),
        _resident((1, HEAD_DIM)),
    ]
    args = [x, mod, g_norm1.reshape(1, D_MODEL), w_in, g_q.reshape(1, HEAD_DIM), g_k.reshape(1, HEAD_DIM)]
    if latent:
        cos, sin = _rope_tables(seq)
        rope_spec = pl.BlockSpec((rows, HEAD_DIM), lambda i: (i % seq_tiles, 0))
        in_specs += [rope_spec, rope_spec]
        args += [cos, sin]
        cond, w_ada, b_ada = ada
        k_rows = D_MODEL // n_steps
        assert k_rows % LANES == 0
        in_specs += [pl.BlockSpec((COND_ROWS, k_rows), lambda i: (0, i))]
        in_specs += [pl.BlockSpec((k_rows, D_MODEL), lambda i, j=j: (i, N_MOD_EARLY + j)) for j in range(N_MOD_LATE)]
        in_specs += [_resident((1, N_MOD * D_MODEL))]
        args += [cond] + [w_ada] * N_MOD_LATE + [b_ada.reshape(1, N_MOD * D_MODEL)]
    else:
        in_specs += [pl.BlockSpec(memory_space=pl.ANY)]
        args += [w_out]
    scratch_shapes = []
    out_specs = [tile(D_ATTN), tile(D_KV), tile(D_KV), tile(D_CONV), tile(D_CONV)]
    out_shape = [
        jax.ShapeDtypeStruct((n_tok, D_ATTN), _BF16),
        jax.ShapeDtypeStruct((n_tok, D_KV), _F32),
        jax.ShapeDtypeStruct((n_tok, D_KV), _F32),
        jax.ShapeDtypeStruct((n_tok, D_CONV), _F32),
        jax.ShapeDtypeStruct((n_tok, D_CONV), _F32),
    ]
    if latent:
        late = (N_MOD_LATE, COND_ROWS, D_MODEL)
        out_specs += [pl.BlockSpec(late, lambda i: (0, 0, 0))]
        out_shape += [jax.ShapeDtypeStruct(late, _F32)]
        scratch_shapes = [pltpu.VMEM(late, _F32)]
    else:
        state_spec = pl.BlockSpec((rows, N_KV_HEADS, HEAD_DIM), lambda i: (i, 0, 0))
        out_specs += [state_spec, state_spec, pl.BlockSpec(memory_space=pl.ANY), pl.BlockSpec(memory_space=pl.ANY)]
        out_shape += [jax.ShapeDtypeStruct((n_tok, N_KV_HEADS, HEAD_DIM), _F32)] * 2
        out_shape += [jax.ShapeDtypeStruct((D_MODEL, D_IN_PROJ), _BF16), jax.ShapeDtypeStruct(w_out.shape, _BF16)]
        assert n_steps >= 2 and w_out.shape[0] % (16 * n_steps) == 0
        scratch_shapes = _weight_scratch(D_MODEL, D_IN_PROJ) + _stream_scratch(*w_out.shape, n_steps)
    return pl.pallas_call(
        functools.partial(_inproj_kernel, latent, seq_tiles),
        grid=(n_steps,),
        in_specs=in_specs,
        out_specs=out_specs,
        out_shape=out_shape,
        scratch_shapes=scratch_shapes,
        compiler_params=_params(1),
        name="inproj_lat" if latent else "inproj_ctx",
    )(*args)


def _mix_kernel(latent, seqs, seq_tiles, n_tiles, x_ref, q_ref, *rest):
    n_kv = 2 if latent else 1
    kv_refs = rest[:2 * n_kv]
    rest = rest[2 * n_kv:]
    bg_ref, cu_ref = rest[:2]
    rest = rest[2:]
    if latent:
        cu_prev_ref, cu_next_ref = rest[:2]
        rest = rest[2:]
    (convw_ref, ga_ref, gc_ref, wout_ref, mod_ref, g2_ref, wr_ref, xo_ref, h2_ref, lg_ref, *rest) = rest
    (mixed_scr,) = rest

    s = pl.program_id(0)

    @pl.when(s == 0)
    def _():
        mixed_scr[...] = jnp.zeros_like(mixed_scr)

    r = 1 + jnp.maximum(s - 1, 0) // seq_tiles if latent else 0
    y = _dot(mixed_scr[...], wout_ref[...])
    x_new = x_ref[...] + mod_ref[0, pl.ds(r, 1), :] * y
    xo_ref[...] = x_new
    h2 = _rms(x_new) * (g2_ref[...] * (1.0 + mod_ref[2, pl.ds(r, 1), :])) + mod_ref[1, pl.ds(r, 1), :]
    h2_hi = h2.astype(_BF16)
    h2_ref[...] = h2_hi
    lg_ref[...] = _dot_nt(wr_ref[...].astype(_BF16), h2_hi)

    rows_per_seq = x_ref.shape[0] // seqs
    row_sum_on_mxu = latent
    stacked = 1 if latent else CONTEXT_HEADS_STACKED
    sub = lax.broadcasted_iota(jnp.int32, (HALO_ROWS, D_CONV), 0)
    for i in range(seqs):
        rs = slice(i * rows_per_seq, (i + 1) * rows_per_seq)

        cu = cu_ref[rs, :]
        if latent:
            pos = jnp.minimum(s, n_tiles - 1) % seq_tiles
            edge_prev = jnp.where(pos == 0, 0.0, 1.0) * cu_prev_ref[HALO_ROWS - 1:HALO_ROWS, :]
            edge_next = jnp.where(pos == seq_tiles - 1, 0.0, 1.0) * cu_next_ref[0:1, :]
        else:
            edge_prev = jnp.zeros((1, D_CONV), _F32)
            edge_next = edge_prev
        prev = pltpu.roll(cu, 1, axis=0)
        nxt = pltpu.roll(cu, rows_per_seq - 1, axis=0)
        prev = jnp.concatenate([jnp.where(sub == 0, edge_prev, prev[0:HALO_ROWS]), prev[HALO_ROWS:]], axis=0)
        nxt = jnp.concatenate([nxt[:rows_per_seq - HALO_ROWS],
                               jnp.where(sub == HALO_ROWS - 1, edge_next, nxt[rows_per_seq - HALO_ROWS:])], axis=0)
        conv = bg_ref[rs, :] * (prev * convw_ref[0:1, :] + cu * convw_ref[1:2, :] + nxt * convw_ref[2:3, :])
        mixed_scr[rs, D_ATTN:D_ATTN + D_CONV] = (_rms(conv) * gc_ref[...]).astype(_BF16)

        heads = []
        for kvh in range(N_KV_HEADS):
            cols = slice(kvh * HEAD_DIM, (kvh + 1) * HEAD_DIM)
            if latent:
                ks = [kv_refs[2 * p][:, cols].astype(_BF16) for p in range(n_kv)]
                vs = [kv_refs[2 * p + 1][:, cols].astype(_BF16) for p in range(n_kv)]
            else:
                ks = [kv_refs[0][i, :, cols].astype(_BF16)]
                vs = [kv_refs[1][i, :, cols].astype(_BF16)]
            if row_sum_on_mxu:
                vs = [jnp.concatenate([vp, jnp.ones_like(vp)], axis=1) for vp in vs]
            for g0 in range(0, GQA_GROUP, stacked):
                first = kvh * GQA_GROUP + g0
                qh = jnp.concatenate([q_ref[rs, (first + j) * HEAD_DIM:(first + j + 1) * HEAD_DIM]
                                      for j in range(stacked)], axis=0)
                sc = [_dot_nt(qh, kp) for kp in ks]
                m = functools.reduce(jnp.maximum, [jnp.max(sp, axis=-1, keepdims=True) for sp in sc])
                if row_sum_on_mxu:
                    ol = functools.reduce(jnp.add,
                                          [_dot(jnp.exp2(sp - m).astype(_BF16), vp) for sp, vp in zip(sc, vs)])
                    o = ol[:, 0:HEAD_DIM] / ol[:, HEAD_DIM:HEAD_DIM + 1]
                else:
                    p = [jnp.exp2(sp - m) for sp in sc]
                    l = functools.reduce(jnp.add, [jnp.sum(pp, axis=-1, keepdims=True) for pp in p])
                    o = functools.reduce(jnp.add, [_dot(pp.astype(_BF16), vp) for pp, vp in zip(p, vs)]) / l
                heads += [o[j * rows_per_seq:(j + 1) * rows_per_seq] for j in range(stacked)]
        ssq = functools.reduce(jnp.add, [jnp.sum(h * h, axis=-1, keepdims=True) for h in heads])
        inv = lax.rsqrt(ssq * (1.0 / D_ATTN) + RMS_EPS)
        for hd, h in enumerate(heads):
            hc = slice(hd * HEAD_DIM, (hd + 1) * HEAD_DIM)
            mixed_scr[rs, hc] = (h * inv * ga_ref[:, hc]).astype(_BF16)


def _mix(x, q, kv_parts, bg, cu, mod, conv_w, g_attn_out, g_conv_out, w_out, g_norm2, w_router_t,
         latent, seq):
    n_tok = x.shape[0]
    tile = LATENT_MIX_TOKENS if latent else CONTEXT_MIX_TOKENS
    seqs = max(1, tile // seq)
    seq_tiles = max(1, seq // tile)
    n_tiles = n_tok // tile
    started = lambda i: jnp.minimum(i, n_tiles - 1)
    finished = lambda i: jnp.maximum(i - 1, 0)
    start_tile = lambda w: pl.BlockSpec((tile, w), lambda i: (started(i), 0))
    finish_tile = lambda w: pl.BlockSpec((tile, w), lambda i: (finished(i), 0))
    in_specs = [finish_tile(D_MODEL), start_tile(D_ATTN)]
    args = [x, q]
    for k, v in kv_parts:
        if latent:
            spec = pl.BlockSpec((None, k.shape[1], D_KV), lambda i: (started(i) // seq_tiles, 0, 0))
        else:
            spec = pl.BlockSpec((seqs, k.shape[1], D_KV), lambda i: (started(i), 0, 0))
        in_specs += [spec, spec]
        args += [k, v]
    in_specs += [start_tile(D_CONV), start_tile(D_CONV)]
    args += [bg, cu]
    if latent:
        per_tile = tile // HALO_ROWS
        last = n_tok // HALO_ROWS - 1
        in_specs += [
            pl.BlockSpec((HALO_ROWS, D_CONV), lambda i: (jnp.maximum(started(i) * per_tile - 1, 0), 0)),
            pl.BlockSpec((HALO_ROWS, D_CONV), lambda i: (jnp.minimum((started(i) + 1) * per_tile, last), 0)),
        ]
        args += [cu, cu]
    in_specs += [
        _resident((3, D_CONV)),
        _resident((1, D_ATTN)),
        _resident((1, D_CONV)),
        _resident((D_ATTN + D_CONV, D_MODEL)),
        _resident((N_MOD_LATE, COND_ROWS, D_MODEL)),
        _resident((1, D_MODEL)),
        _resident((N_EXPERTS, D_MODEL)),
    ]
    args += [conv_w, g_attn_out.reshape(1, D_ATTN), g_conv_out.reshape(1, D_CONV), w_out, mod,
             g_norm2.reshape(1, D_MODEL), w_router_t]
    out_specs = [finish_tile(D_MODEL), finish_tile(D_MODEL),
                 pl.BlockSpec((N_EXPERTS, tile), lambda i: (0, finished(i)))]
    out_shape = [
        jax.ShapeDtypeStruct((n_tok, D_MODEL), _F32),
        jax.ShapeDtypeStruct((n_tok, D_MODEL), _BF16),
        jax.ShapeDtypeStruct((N_EXPERTS, n_tok), _F32),
    ]
    scratch_shapes = [pltpu.VMEM((tile, D_ATTN + D_CONV), _BF16)]
    return pl.pallas_call(
        functools.partial(_mix_kernel, latent, seqs, seq_tiles, n_tiles),
        grid=(n_tiles + 1,),
        in_specs=in_specs,
        out_specs=out_specs,
        out_shape=out_shape,
        scratch_shapes=scratch_shapes,
        compiler_params=_params(1),
        name="mix_lat" if latent else "mix_ctx",
    )(*args)


def _route_kernel(seq, cap, sets, e_blk, lg_ref, h2_ref, xs_ref, gw_ref, aff_scr, key_scr):
    j = pl.program_id(1)

    @pl.when(j == 0)
    def _():
        other = lax.broadcasted_iota(jnp.int32, (seq, seq), 0)
        token = lax.broadcasted_iota(jnp.int32, (seq, seq), 1)
        earlier = jnp.where(other < token, 1.0, 0.0)
        tie = earlier[0:LANES, 0:LANES]
        before = earlier.astype(_BF16)
        for s in range(sets):
            lg = lg_ref[:, s * seq:(s + 1) * seq]
            ex = jnp.exp(lg - jnp.max(lg, axis=0, keepdims=True))
            aff = ex / jnp.sum(ex, axis=0, keepdims=True)
            aff_t = jnp.concatenate([aff, jnp.zeros((LANES - N_EXPERTS, seq), _F32)], axis=0).T
            ranks = []
            for e in range(N_EXPERTS):
                theirs = aff_t[:, e:e + 1]
                blocks = []
                for lo in range(0, seq, LANES):
                    hi = lo + LANES
                    mine = aff[e:e + 1, lo:hi]
                    diag = theirs[lo:hi]
                    cnt = jnp.sum(jnp.where(diag > mine, 1.0, jnp.where(diag == mine, tie, 0.0)),
                                  axis=0, keepdims=True)
                    if lo > 0:
                        cnt += jnp.sum(jnp.where(theirs[0:lo] >= mine, 1.0, 0.0), axis=0, keepdims=True)
                    if hi < seq:
                        cnt += jnp.sum(jnp.where(theirs[hi:seq] > mine, 1.0, 0.0), axis=0, keepdims=True)
                    blocks.append(cnt)
                ranks.append(jnp.concatenate(blocks, axis=1))
            rank = jnp.concatenate(ranks, axis=0)
            sel = rank < float(cap)
            slot = _dot(jnp.where(sel, 1.0, 0.0).astype(_BF16), before)
            rows = slice(s * N_EXPERTS, (s + 1) * N_EXPERTS)
            aff_scr[rows, :] = aff
            key_scr[rows, :] = jnp.where(sel, slot, -1.0)

    c_idx = lax.broadcasted_iota(jnp.int32, (cap, seq), 0).astype(_F32)
    for s in range(sets):
        onehot, gated = [], []
        for i in range(e_blk):
            e = s * N_EXPERTS + j * e_blk + i
            hit = c_idx == key_scr[pl.ds(e, 1), :]
            onehot.append(jnp.where(hit, 1.0, 0.0))
            gated.append(jnp.where(hit, aff_scr[pl.ds(e, 1), :], 0.0))
        xs = _dot(jnp.concatenate(onehot, axis=0).astype(_BF16), h2_ref[s * seq:(s + 1) * seq, :]).astype(xs_ref.dtype)
        for i in range(e_blk):
            xs_ref[i, s * cap:(s + 1) * cap, :] = xs[i * cap:(i + 1) * cap, :]
        gw_ref[s] = jnp.concatenate(gated, axis=0).astype(gw_ref.dtype)


def _route(logits_t, h2, seq, sets, e_blk):
    n_tok = h2.shape[0]
    n_sets = n_tok // seq
    cap = EC_CAPACITY_FACTOR * seq // N_EXPERTS
    return pl.pallas_call(
        functools.partial(_route_kernel, seq, cap, sets, e_blk),
        grid=(n_sets // sets, N_EXPERTS // e_blk),
        in_specs=[
            pl.BlockSpec((N_EXPERTS, sets * seq), lambda s, j: (0, s)),
            pl.BlockSpec((sets * seq, D_MODEL), lambda s, j: (s, 0)),
        ],
        out_specs=[
            pl.BlockSpec((e_blk, sets * cap, D_MODEL), lambda s, j: (j, s, 0)),
            pl.BlockSpec((sets, e_blk * cap, seq), lambda s, j: (s, j, 0)),
        ],
        out_shape=[
            jax.ShapeDtypeStruct((N_EXPERTS, n_sets * cap, D_MODEL), _BF16),
            jax.ShapeDtypeStruct((n_sets, N_EXPERTS * cap, seq), _BF16),
        ],
        scratch_shapes=[pltpu.VMEM((sets * N_EXPERTS, seq), _F32), pltpu.VMEM((sets * N_EXPERTS, seq), _F32)],
        compiler_params=_params(2),
        name="route_%d" % seq,
    )(logits_t, h2)


def _ffn_kernel(xa_ref, xb_ref, wg_ref, wu_ref, wd_ref, ya_ref, yb_ref):
    def run(first):
        wg = wg_ref[...].astype(_BF16)
        wu = wu_ref[...].astype(_BF16)
        wd = wd_ref[...].astype(_BF16)
        for x_ref, y_ref in ((xa_ref, ya_ref), (xb_ref, yb_ref)):
            chunk = min(FFN_ROW_CHUNK, x_ref.shape[0])
            for c in range(x_ref.shape[0] // chunk):
                rows = slice(c * chunk, (c + 1) * chunk)
                x = x_ref[rows, :]
                a = _dot(x, wg)
                u = _dot(x, wu)
                total = _dot((a * jax.nn.sigmoid(a) * u).astype(_BF16), wd)
                if not first:
                    total = y_ref[rows, :].astype(_F32) + total
                y_ref[rows, :] = total.astype(y_ref.dtype)

    @pl.when(pl.program_id(1) == 0)
    def _():
        run(True)

    @pl.when(pl.program_id(1) > 0)
    def _():
        run(False)


def _ffn(xs_a, xs_b, w_gate, w_up, w_down):
    rows_a, rows_b = xs_a.shape[1], xs_b.shape[1]
    rows = lambda n: pl.BlockSpec((None, n, D_MODEL), lambda e, f: (e, 0, 0))
    return pl.pallas_call(
        _ffn_kernel,
        grid=(N_EXPERTS, D_FF // FF_TILE),
        in_specs=[
            rows(rows_a), rows(rows_b),
            pl.BlockSpec((None, D_MODEL, FF_TILE), lambda e, f: (e, 0, f)),
            pl.BlockSpec((None, D_MODEL, FF_TILE), lambda e, f: (e, 0, f)),
            pl.BlockSpec((None, FF_TILE, D_MODEL), lambda e, f: (e, f, 0)),
        ],
        out_specs=[rows(rows_a), rows(rows_b)],
        out_shape=[jax.ShapeDtypeStruct(xs_a.shape, _BF16), jax.ShapeDtypeStruct(xs_b.shape, _BF16)],
        compiler_params=_params(2),
        name="ffn",
    )(xs_a, xs_b, w_gate, w_up, w_down)


def _combine_kernel(latent, x_ref, y_ref, gw_ref, mod_ref, o_ref):
    r = 1 + pl.program_id(0) if latent else 0
    sets, rows, tile = gw_ref.shape
    cap = rows // N_EXPERTS
    gate = mod_ref[3, pl.ds(r, 1), :]
    for i in range(sets):
        y = y_ref[:, i * cap:(i + 1) * cap, :].reshape(rows, D_MODEL)
        moe = lax.dot_general(gw_ref[i], y, (((0,), (0,)), ((), ())), preferred_element_type=_F32)
        tok = slice(i * tile, (i + 1) * tile)
        o_ref[tok, :] = x_ref[tok, :] + gate * moe


def _combine(x_new, y, gw, mod, latent, seq, step_tokens):
    n_tok = x_new.shape[0]
    n_sets = n_tok // seq
    cap = EC_CAPACITY_FACTOR * seq // N_EXPERTS
    sets = max(1, step_tokens // seq)
    tile = step_tokens // sets
    per_set = seq // tile
    tokens = pl.BlockSpec((step_tokens, D_MODEL), lambda s, t: (s * per_set + t, 0))
    return pl.pallas_call(
        functools.partial(_combine_kernel, latent),
        grid=(n_sets // sets, per_set),
        in_specs=[
            tokens,
            pl.BlockSpec((N_EXPERTS, sets * cap, D_MODEL), lambda s, t: (0, s, 0)),
            pl.BlockSpec((sets, N_EXPERTS * cap, tile), lambda s, t: (s, 0, t)),
            _resident((N_MOD_LATE, COND_ROWS, D_MODEL)),
        ],
        out_specs=tokens,
        out_shape=jax.ShapeDtypeStruct((n_tok, D_MODEL), _F32),
        compiler_params=_params(2),
        name="combine_lat" if latent else "combine_ctx",
    )(x_new, y, gw, mod)


def kernel(x_prompt, x_sample, cache_k, cache_v, c, c_ctx, w_ada, b_ada, g_norm1, w_in, g_q, g_k, conv_w,
           g_attn_out, g_conv_out, w_out, g_norm2, w_router, w_gate, w_up, w_down):
    batch, seq, _ = x_prompt.shape
    dec_batch, dec_seq, _ = x_sample.shape
    depth = w_ada.shape[0]
    assert 1 + dec_batch <= COND_ROWS

    cond = jnp.concatenate([c_ctx[None, :], c, jnp.zeros((COND_ROWS - 1 - dec_batch, D_MODEL), _F32)], axis=0)
    xp = x_prompt.reshape(batch * seq, D_MODEL)
    xl = x_sample.reshape(dec_batch * dec_seq, D_MODEL)
    past = cache_k.shape[2]

    new_k, new_v = [], []
    for l in range(depth):
        mod = _ada(cond, w_ada[l], b_ada[l])
        w_router_t = w_router[l].T

        q_p, k_p, v_p, bg_p, cu_p, kstate, vstate, w_in_b, w_out_b = _inproj(
            xp, mod, g_norm1[l], w_in[l], g_q[l], g_k[l], False, seq, w_out[l])
        q_l, k_l, v_l, bg_l, cu_l, mod_late = _inproj(xl, mod, g_norm1[l], w_in_b, g_q[l], g_k[l], True, dec_seq,
                                                      ada=(cond, w_ada[l], b_ada[l]))

        shared = (mod_late, conv_w[l], g_attn_out[l], g_conv_out[l], w_out_b, g_norm2[l], w_router_t)
        kv_p = [(k_p.reshape(batch, seq, D_KV), v_p.reshape(batch, seq, D_KV))]
        kv_l = [(cache_k[:, l].reshape(dec_batch, past, D_KV), cache_v[:, l].reshape(dec_batch, past, D_KV)),
                (k_l.reshape(dec_batch, dec_seq, D_KV), v_l.reshape(dec_batch, dec_seq, D_KV))]
        xp1, h2_p, lg_p = _mix(xp, q_p, kv_p, bg_p, cu_p, *shared, False, seq)
        xl1, h2_l, lg_l = _mix(xl, q_l, kv_l, bg_l, cu_l, *shared, True, dec_seq)

        xs_p, gw_p = _route(lg_p, h2_p, seq, CONTEXT_ROUTE_SETS, N_EXPERTS)
        xs_l, gw_l = _route(lg_l, h2_l, dec_seq, 1, LATENT_ROUTE_EXPERTS)
        y_p, y_l = _ffn(xs_p, xs_l, w_gate[l], w_up[l], w_down[l])
        xp = _combine(xp1, y_p, gw_p, mod_late, False, seq, CONTEXT_COMBINE_TOKENS)
        xl = _combine(xl1, y_l, gw_l, mod_late, True, dec_seq, LATENT_COMBINE_TOKENS)

        new_k.append(kstate.reshape(batch, seq, N_KV_HEADS, HEAD_DIM))
        new_v.append(vstate.reshape(batch, seq, N_KV_HEADS, HEAD_DIM))

    return (xp.reshape(batch, seq, D_MODEL), xl.reshape(dec_batch, dec_seq, D_MODEL),
            jnp.stack(new_k, axis=1), jnp.stack(new_v, axis=1))
```

```python
import functools

import jax
import jax.numpy as jnp
from jax import lax
from jax.experimental import pallas as pl
from jax.experimental.pallas import tpu as pltpu

D_MODEL = 2048
GRID_W = 64
HEAD_DIM = 128
D_ATTN = 1024
D_CONV = 1024
N_HEADS = 8
N_KV_HEADS = 2
GQA_GROUP = N_HEADS // N_KV_HEADS
D_KV = N_KV_HEADS * HEAD_DIM
N_EXPERTS = 16
EC_CAPACITY_FACTOR = 2
D_FF = 1024
ROPE_THETA = 10000.0
RMS_EPS = 1e-6
LOG2_E = 1.4426950408889634
N_MOD = 6
N_MOD_EARLY = 2
N_MOD_LATE = N_MOD - N_MOD_EARLY
D_IN_PROJ = D_ATTN + 2 * D_KV + 3 * D_CONV

CONTEXT_MIX_TOKENS = 512
LATENT_MIX_TOKENS = 256
INPROJ_TILE = 512
LATENT_INPROJ_TILE = 256
ADA_COL_TILE = 1024
FF_TILE = 512
FFN_ROW_CHUNK = 512
COND_ROWS = 8
HALO_ROWS = 8
LATENT_ROUTE_EXPERTS = 8
CONTEXT_ROUTE_SETS = 4
CONTEXT_HEADS_STACKED = 2
VMEM_LIMIT = 56 * 1024 * 1024
LANES = 128
STAGE_ROWS = 32
STAGE_SLOTS = 8
COMBINE_RING = 3
CONTEXT_COMBINE_TOKENS = 512
LATENT_COMBINE_TOKENS = 512

_BF16 = jnp.bfloat16
_F32 = jnp.float32


def _params(n_axes):
    return pltpu.CompilerParams(dimension_semantics=("arbitrary",) * n_axes,
                                vmem_limit_bytes=VMEM_LIMIT)


def _resident(shape):
    return pl.BlockSpec(shape, lambda *_: (0,) * len(shape), pipeline_mode=pl.Buffered(1))


def _dot(a, b):
    return jnp.dot(a, b, preferred_element_type=_F32)


def _dot_nt(a, b):
    return lax.dot_general(a, b, (((1,), (1,)), ((), ())), preferred_element_type=_F32)


def _rms(x):
    return x * lax.rsqrt(jnp.mean(x * x, axis=-1, keepdims=True) + RMS_EPS)


def _split_bf16(x):
    hi = x.astype(_BF16)
    lo = (x - hi.astype(_F32)).astype(_BF16)
    return hi, lo


def _weight_scratch(rows, cols):
    return [pltpu.VMEM((rows, cols), _BF16), pltpu.VMEM((STAGE_SLOTS, STAGE_ROWS, cols), _F32),
            pltpu.SemaphoreType.DMA((STAGE_SLOTS,)), pltpu.SemaphoreType.DMA(())]


def _cast_weight_once(w_hbm, w_pub, w_scr, stage, sems, pub_sem, step, n_steps):
    n_chunks = w_scr.shape[0] // STAGE_ROWS

    def fetch(c):
        slot = c % STAGE_SLOTS
        return pltpu.make_async_copy(w_hbm.at[pl.ds(c * STAGE_ROWS, STAGE_ROWS), :], stage.at[slot], sems.at[slot])

    publish = pltpu.make_async_copy(w_scr, w_pub, pub_sem)

    @pl.when(step == 0)
    def _():
        for c in range(STAGE_SLOTS):
            fetch(c).start()
        for c in range(n_chunks):
            fetch(c).wait()
            w_scr[c * STAGE_ROWS:(c + 1) * STAGE_ROWS, :] = stage[c % STAGE_SLOTS].astype(_BF16)
            if c + STAGE_SLOTS < n_chunks:
                fetch(c + STAGE_SLOTS).start()
        publish.start()

    @pl.when(step == n_steps - 1)
    def _():
        publish.wait()


def _stream_scratch(rows, cols, n_steps):
    chunk = rows // n_steps
    return [pltpu.VMEM((2, chunk, cols), _F32), pltpu.VMEM((2, chunk, cols), _BF16),
            pltpu.SemaphoreType.DMA((2,)), pltpu.SemaphoreType.DMA((2,))]


def _cast_weight_alongside(w_hbm, w_pub, stage, stage_b, in_sems, out_sems, step, n_steps):
    chunk = stage.shape[1]
    slot = step % 2

    def fetch(k, sl):
        return pltpu.make_async_copy(w_hbm.at[pl.ds(k * chunk, chunk), :], stage.at[sl], in_sems.at[sl])

    def push(k, sl):
        return pltpu.make_async_copy(stage_b.at[sl], w_pub.at[pl.ds(k * chunk, chunk), :], out_sems.at[sl])

    @pl.when(step == 0)
    def _():
        fetch(0, 0).start()

    @pl.when(step + 1 < n_steps)
    def _():
        fetch(step + 1, 1 - slot).start()

    fetch(step, slot).wait()

    @pl.when(step >= 2)
    def _():
        push(step - 2, slot).wait()

    stage_b[slot] = stage[slot].astype(_BF16)
    push(step, slot).start()

    @pl.when(step == n_steps - 1)
    def _():
        push(step, slot).wait()
        push(step - 1, 1 - slot).wait()


def _ada_kernel(c_ref, w_ref, b_ref, o_ref):
    c = c_ref[...]
    a = c * jax.nn.sigmoid(c)
    a_hi, a_lo = _split_bf16(a)
    w = w_ref[...]
    w_hi, w_lo = _split_bf16(w)
    o_ref[...] = _dot(a_hi, w_hi) + (_dot(a_lo, w_hi) + _dot(a_hi, w_lo)) + b_ref[...]


def _late_modulation(cond_ref, wada_refs, bada_ref, late_ref, acc, step, n_steps):
    @pl.when(step == 0)
    def _():
        acc[...] = jnp.zeros_like(acc)

    c = cond_ref[...]
    a = jnp.concatenate(_split_bf16(c * jax.nn.sigmoid(c)), axis=0)
    for j, w_ref in enumerate(wada_refs):
        d = _dot(a, w_ref[...].astype(_BF16))
        acc[j] += d[0:COND_ROWS] + d[COND_ROWS:2 * COND_ROWS]

    @pl.when(step == n_steps - 1)
    def _():
        for j in range(len(wada_refs)):
            cols = slice((N_MOD_EARLY + j) * D_MODEL, (N_MOD_EARLY + j + 1) * D_MODEL)
            late_ref[j] = acc[j] + bada_ref[:, cols]


def _ada(cond, w_ada, b_ada):
    per_mod = D_MODEL // ADA_COL_TILE
    return pl.pallas_call(
        _ada_kernel,
        grid=(N_MOD_EARLY, per_mod),
        in_specs=[
            pl.BlockSpec((COND_ROWS, D_MODEL), lambda j, h: (0, 0)),
            pl.BlockSpec((D_MODEL, ADA_COL_TILE), lambda j, h: (0, j * per_mod + h)),
            pl.BlockSpec((1, ADA_COL_TILE), lambda j, h: (0, j * per_mod + h)),
        ],
        out_specs=pl.BlockSpec((None, COND_ROWS, ADA_COL_TILE), lambda j, h: (j, 0, h)),
        out_shape=jax.ShapeDtypeStruct((N_MOD_EARLY, COND_ROWS, D_MODEL), _F32),
        compiler_params=_params(2),
        name="ada",
    )(cond, w_ada, b_ada.reshape(1, N_MOD * D_MODEL))


def _rope_tables(seq):
    t = jnp.arange(seq)
    row = (t // GRID_W).astype(_F32)
    col = (t % GRID_W).astype(_F32)
    half = HEAD_DIM // 2
    inv_freq = ROPE_THETA ** (-jnp.arange(0, half, 2, dtype=_F32) / half)
    ang_r = row[:, None] * inv_freq[None, :]
    ang_c = col[:, None] * inv_freq[None, :]
    cos = jnp.concatenate([jnp.cos(ang_r)] * 2 + [jnp.cos(ang_c)] * 2, axis=-1)
    sin = jnp.concatenate([-jnp.sin(ang_r), jnp.sin(ang_r), -jnp.sin(ang_c), jnp.sin(ang_c)], axis=-1)
    return cos, sin


def _cond_row(latent, seq_tiles):
    if not latent:
        return 0
    return 1 + pl.program_id(0) // seq_tiles


def _inproj_kernel(latent, seq_tiles, x_ref, mod_ref, g1_ref, w_ref, gq_ref, gk_ref, *rest):
    if latent:
        cos_ref, sin_ref, cond_ref, *rest = rest
        wada_refs, rest = rest[:N_MOD_LATE], rest[N_MOD_LATE:]
        bada_ref, q_ref, k_ref, v_ref, bg_ref, cu_ref, late_ref, late_acc = rest
        _late_modulation(cond_ref, wada_refs, bada_ref, late_ref, late_acc, pl.program_id(0), pl.num_programs(0))
    else:
        (wo_hbm, q_ref, k_ref, v_ref, bg_ref, cu_ref, kstate_ref, vstate_ref, w_pub, wo_pub,
         *scratch) = rest
        cast_scratch, stream_scratch = scratch[:4], scratch[4:]
        _cast_weight_once(w_ref, w_pub, *cast_scratch, pl.program_id(0), pl.num_programs(0))
        w_ref = cast_scratch[0]
        _cast_weight_alongside(wo_hbm, wo_pub, *stream_scratch, pl.program_id(0), pl.num_programs(0))
    r = _cond_row(latent, seq_tiles)
    sh = mod_ref[0, pl.ds(r, 1), :]
    sc = mod_ref[1, pl.ds(r, 1), :]
    h = _rms(x_ref[...]) * (g1_ref[...] * (1.0 + sc)) + sh
    hb = h.astype(_BF16)

    if latent:
        cos = cos_ref[...]
        sin = sin_ref[...]
        lane = lax.broadcasted_iota(jnp.int32, cos.shape, 1)
        first = (lane % (HEAD_DIM // 2)) < (HEAD_DIM // 4)

    def head_norm(xh, g):
        xh = _rms(xh) * g
        if latent:
            partner = jnp.where(first, pltpu.roll(xh, HEAD_DIM - HEAD_DIM // 4, axis=1),
                                pltpu.roll(xh, HEAD_DIM // 4, axis=1))
            xh = xh * cos + partner * sin
        return xh

    gq = gq_ref[...] * (HEAD_DIM ** -0.5 * LOG2_E)
    q = _dot(hb, w_ref[:, 0:D_ATTN])
    for hd in range(N_HEADS):
        cols = slice(hd * HEAD_DIM, (hd + 1) * HEAD_DIM)
        q_ref[:, cols] = head_norm(q[:, cols], gq).astype(q_ref.dtype)
    k = _dot(hb, w_ref[:, D_ATTN:D_ATTN + D_KV])
    o = D_ATTN + D_KV
    v = _dot(hb, w_ref[:, o:o + D_KV])
    for hd in range(N_KV_HEADS):
        cols = slice(hd * HEAD_DIM, (hd + 1) * HEAD_DIM)
        kh = head_norm(k[:, cols], gk_ref[...])
        k_ref[:, cols] = kh
        v_ref[:, cols] = v[:, cols]
        if not latent:
            kstate_ref[:, hd, :] = kh
            vstate_ref[:, hd, :] = v[:, cols]
    o += D_KV
    bg_ref[...] = _dot(hb, w_ref[:, o:o + D_CONV])
    o += D_CONV
    cg = _dot(hb, w_ref[:, o:o + D_CONV])
    o += D_CONV
    cu_ref[...] = cg * _dot(hb, w_ref[:, o:o + D_CONV])


def _inproj(x, mod, g_norm1, w_in, g_q, g_k, latent, seq, w_out=None, ada=None):
    n_tok = x.shape[0]
    rows = LATENT_INPROJ_TILE if latent else INPROJ_TILE
    seq_tiles = seq // rows
    n_steps = n_tok // rows
    tile = lambda w: pl.BlockSpec((rows, w), lambda i: (i, 0))
    in_specs = [
        tile(D_MODEL),
        _resident((N_MOD_EARLY, COND_ROWS, D_MODEL)),
        _resident((1, D_MODEL)),
        _resident((D_MODEL, D_IN_PROJ)) if latent else pl.BlockSpec(memory_space=pl.ANY),
        _resident((1, HEAD_DIM)),
        _resident((1, HEAD_DIM)),
    ]
    args = [x, mod, g_norm1.reshape(1, D_MODEL), w_in, g_q.reshape(1, HEAD_DIM), g_k.reshape(1, HEAD_DIM)]
    if latent:
        cos, sin = _rope_tables(seq)
        rope_spec = pl.BlockSpec((rows, HEAD_DIM), lambda i: (i % seq_tiles, 0))
        in_specs += [rope_spec, rope_spec]
        args += [cos, sin]
        cond, w_ada, b_ada = ada
        k_rows = D_MODEL // n_steps
        assert k_rows % LANES == 0
        in_specs += [pl.BlockSpec((COND_ROWS, k_rows), lambda i: (0, i))]
        in_specs += [pl.BlockSpec((k_rows, D_MODEL), lambda i, j=j: (i, N_MOD_EARLY + j)) for j in range(N_MOD_LATE)]
        in_specs += [_resident((1, N_MOD * D_MODEL))]
        args += [cond] + [w_ada] * N_MOD_LATE + [b_ada.reshape(1, N_MOD * D_MODEL)]
    else:
        in_specs += [pl.BlockSpec(memory_space=pl.ANY)]
        args += [w_out]
    scratch_shapes = []
    out_specs = [tile(D_ATTN), tile(D_KV), tile(D_KV), tile(D_CONV), tile(D_CONV)]
    out_shape = [
        jax.ShapeDtypeStruct((n_tok, D_ATTN), _BF16),
        jax.ShapeDtypeStruct((n_tok, D_KV), _F32),
        jax.ShapeDtypeStruct((n_tok, D_KV), _F32),
        jax.ShapeDtypeStruct((n_tok, D_CONV), _F32),
        jax.ShapeDtypeStruct((n_tok, D_CONV), _F32),
    ]
    if latent:
        late = (N_MOD_LATE, COND_ROWS, D_MODEL)
        out_specs += [pl.BlockSpec(late, lambda i: (0, 0, 0))]
        out_shape += [jax.ShapeDtypeStruct(late, _F32)]
        scratch_shapes = [pltpu.VMEM(late, _F32)]
    else:
        state_spec = pl.BlockSpec((rows, N_KV_HEADS, HEAD_DIM), lambda i: (i, 0, 0))
        out_specs += [state_spec, state_spec, pl.BlockSpec(memory_space=pl.ANY), pl.BlockSpec(memory_space=pl.ANY)]
        out_shape += [jax.ShapeDtypeStruct((n_tok, N_KV_HEADS, HEAD_DIM), _F32)] * 2
        out_shape += [jax.ShapeDtypeStruct((D_MODEL, D_IN_PROJ), _BF16), jax.ShapeDtypeStruct(w_out.shape, _BF16)]
        assert n_steps >= 2 and w_out.shape[0] % (16 * n_steps) == 0
        scratch_shapes = _weight_scratch(D_MODEL, D_IN_PROJ) + _stream_scratch(*w_out.shape, n_steps)
    return pl.pallas_call(
        functools.partial(_inproj_kernel, latent, seq_tiles),
        grid=(n_steps,),
        in_specs=in_specs,
        out_specs=out_specs,
        out_shape=out_shape,
        scratch_shapes=scratch_shapes,
        compiler_params=_params(1),
        name="inproj_lat" if latent else "inproj_ctx",
    )(*args)


def _mix_kernel(latent, seqs, seq_tiles, n_tiles, x_ref, q_ref, *rest):
    n_kv = 2 if latent else 1
    kv_refs = rest[:2 * n_kv]
    rest = rest[2 * n_kv:]
    bg_ref, cu_ref = rest[:2]
    rest = rest[2:]
    if latent:
        cu_prev_ref, cu_next_ref = rest[:2]
        rest = rest[2:]
    (convw_ref, ga_ref, gc_ref, wout_ref, mod_ref, g2_ref, wr_ref, xo_ref, h2_ref, lg_ref, *rest) = rest
    (mixed_scr,) = rest

    s = pl.program_id(0)

    @pl.when(s == 0)
    def _():
        mixed_scr[...] = jnp.zeros_like(mixed_scr)

    r = 1 + jnp.maximum(s - 1, 0) // seq_tiles if latent else 0
    y = _dot(mixed_scr[...], wout_ref[...])
    x_new = x_ref[...] + mod_ref[0, pl.ds(r, 1), :] * y
    xo_ref[...] = x_new
    h2 = _rms(x_new) * (g2_ref[...] * (1.0 + mod_ref[2, pl.ds(r, 1), :])) + mod_ref[1, pl.ds(r, 1), :]
    h2_hi = h2.astype(_BF16)
    h2_ref[...] = h2_hi
    lg_ref[...] = _dot_nt(wr_ref[...].astype(_BF16), h2_hi)

    rows_per_seq = x_ref.shape[0] // seqs
    row_sum_on_mxu = latent
    stacked = 1 if latent else CONTEXT_HEADS_STACKED
    sub = lax.broadcasted_iota(jnp.int32, (HALO_ROWS, D_CONV), 0)
    for i in range(seqs):
        rs = slice(i * rows_per_seq, (i + 1) * rows_per_seq)

        cu = cu_ref[rs, :]
        if latent:
            pos = jnp.minimum(s, n_tiles - 1) % seq_tiles
            edge_prev = jnp.where(pos == 0, 0.0, 1.0) * cu_prev_ref[HALO_ROWS - 1:HALO_ROWS, :]
            edge_next = jnp.where(pos == seq_tiles - 1, 0.0, 1.0) * cu_next_ref[0:1, :]
        else:
            edge_prev = jnp.zeros((1, D_CONV), _F32)
            edge_next = edge_prev
        prev = pltpu.roll(cu, 1, axis=0)
        nxt = pltpu.roll(cu, rows_per_seq - 1, axis=0)
        prev = jnp.concatenate([jnp.where(sub == 0, edge_prev, prev[0:HALO_ROWS]), prev[HALO_ROWS:]], axis=0)
        nxt = jnp.concatenate([nxt[:rows_per_seq - HALO_ROWS],
                               jnp.where(sub == HALO_ROWS - 1, edge_next, nxt[rows_per_seq - HALO_ROWS:])], axis=0)
        conv = bg_ref[rs, :] * (prev * convw_ref[0:1, :] + cu * convw_ref[1:2, :] + nxt * convw_ref[2:3, :])
        mixed_scr[rs, D_ATTN:D_ATTN + D_CONV] = (_rms(conv) * gc_ref[...]).astype(_BF16)

        heads = []
        for kvh in range(N_KV_HEADS):
            cols = slice(kvh * HEAD_DIM, (kvh + 1) * HEAD_DIM)
            if latent:
                ks = [kv_refs[2 * p][:, cols].astype(_BF16) for p in range(n_kv)]
                vs = [kv_refs[2 * p + 1][:, cols].astype(_BF16) for p in range(n_kv)]
            else:
                ks = [kv_refs[0][i, :, cols].astype(_BF16)]
                vs = [kv_refs[1][i, :, cols].astype(_BF16)]
            if row_sum_on_mxu:
                vs = [jnp.concatenate([vp, jnp.ones_like(vp)], axis=1) for vp in vs]
            for g0 in range(0, GQA_GROUP, stacked):
                first = kvh * GQA_GROUP + g0
                qh = jnp.concatenate([q_ref[rs, (first + j) * HEAD_DIM:(first + j + 1) * HEAD_DIM]
                                      for j in range(stacked)], axis=0)
                sc = [_dot_nt(qh, kp) for kp in ks]
                m = functools.reduce(jnp.maximum, [jnp.max(sp, axis=-1, keepdims=True) for sp in sc])
                if row_sum_on_mxu:
                    ol = functools.reduce(jnp.add,
                                          [_dot(jnp.exp2(sp - m).astype(_BF16), vp) for sp, vp in zip(sc, vs)])
                    o = ol[:, 0:HEAD_DIM] / ol[:, HEAD_DIM:HEAD_DIM + 1]
                else:
                    p = [jnp.exp2(sp - m) for sp in sc]
                    l = functools.reduce(jnp.add, [jnp.sum(pp, axis=-1, keepdims=True) for pp in p])
                    o = functools.reduce(jnp.add, [_dot(pp.astype(_BF16), vp) for pp, vp in zip(p, vs)]) / l
                heads += [o[j * rows_per_seq:(j + 1) * rows_per_seq] for j in range(stacked)]
        ssq = functools.reduce(jnp.add, [jnp.sum(h * h, axis=-1, keepdims=True) for h in heads])
        inv = lax.rsqrt(ssq * (1.0 / D_ATTN) + RMS_EPS)
        for hd, h in enumerate(heads):
            hc = slice(hd * HEAD_DIM, (hd + 1) * HEAD_DIM)
            mixed_scr[rs, hc] = (h * inv * ga_ref[:, hc]).astype(_BF16)


def _mix(x, q, kv_parts, bg, cu, mod, conv_w, g_attn_out, g_conv_out, w_out, g_norm2, w_router_t,
         latent, seq):
    n_tok = x.shape[0]
    tile = LATENT_MIX_TOKENS if latent else CONTEXT_MIX_TOKENS
    seqs = max(1, tile // seq)
    seq_tiles = max(1, seq // tile)
    n_tiles = n_tok // tile
    started = lambda i: jnp.minimum(i, n_tiles - 1)
    finished = lambda i: jnp.maximum(i - 1, 0)
    start_tile = lambda w: pl.BlockSpec((tile, w), lambda i: (started(i), 0))
    finish_tile = lambda w: pl.BlockSpec((tile, w), lambda i: (finished(i), 0))
    in_specs = [finish_tile(D_MODEL), start_tile(D_ATTN)]
    args = [x, q]
    for k, v in kv_parts:
        if latent:
            spec = pl.BlockSpec((None, k.shape[1], D_KV), lambda i: (started(i) // seq_tiles, 0, 0))
        else:
            spec = pl.BlockSpec((seqs, k.shape[1], D_KV), lambda i: (started(i), 0, 0))
        in_specs += [spec, spec]
        args += [k, v]
    in_specs += [start_tile(D_CONV), start_tile(D_CONV)]
    args += [bg, cu]
    if latent:
        per_tile = tile // HALO_ROWS
        last = n_tok // HALO_ROWS - 1
        in_specs += [
            pl.BlockSpec((HALO_ROWS, D_CONV), lambda i: (jnp.maximum(started(i) * per_tile - 1, 0), 0)),
            pl.BlockSpec((HALO_ROWS, D_CONV), lambda i: (jnp.minimum((started(i) + 1) * per_tile, last), 0)),
        ]
        args += [cu, cu]
    in_specs += [
        _resident((3, D_CONV)),
        _resident((1, D_ATTN)),
        _resident((1, D_CONV)),
        _resident((D_ATTN + D_CONV, D_MODEL)),
        _resident((N_MOD_LATE, COND_ROWS, D_MODEL)),
        _resident((1, D_MODEL)),
        _resident((N_EXPERTS, D_MODEL)),
    ]
    args += [conv_w, g_attn_out.reshape(1, D_ATTN), g_conv_out.reshape(1, D_CONV), w_out, mod,
             g_norm2.reshape(1, D_MODEL), w_router_t]
    out_specs = [finish_tile(D_MODEL), finish_tile(D_MODEL),
                 pl.BlockSpec((N_EXPERTS, tile), lambda i: (0, finished(i)))]
    out_shape = [
        jax.ShapeDtypeStruct((n_tok, D_MODEL), _F32),
        jax.ShapeDtypeStruct((n_tok, D_MODEL), _BF16),
        jax.ShapeDtypeStruct((N_EXPERTS, n_tok), _F32),
    ]
    scratch_shapes = [pltpu.VMEM((tile, D_ATTN + D_CONV), _BF16)]
    return pl.pallas_call(
        functools.partial(_mix_kernel, latent, seqs, seq_tiles, n_tiles),
        grid=(n_tiles + 1,),
        in_specs=in_specs,
        out_specs=out_specs,
        out_shape=out_shape,
        scratch_shapes=scratch_shapes,
        compiler_params=_params(1),
        name="mix_lat" if latent else "mix_ctx",
    )(*args)


def _route_kernel(seq, cap, sets, e_blk, lg_ref, h2_ref, xs_ref, gw_ref, aff_scr, key_scr):
    j = pl.program_id(1)

    @pl.when(j == 0)
    def _():
        other = lax.broadcasted_iota(jnp.int32, (seq, seq), 0)
        token = lax.broadcasted_iota(jnp.int32, (seq, seq), 1)
        earlier = jnp.where(other < token, 1.0, 0.0)
        tie = earlier[0:LANES, 0:LANES]
        before = earlier.astype(_BF16)
        for s in range(sets):
            lg = lg_ref[:, s * seq:(s + 1) * seq]
            ex = jnp.exp(lg - jnp.max(lg, axis=0, keepdims=True))
            aff = ex / jnp.sum(ex, axis=0, keepdims=True)
            aff_t = jnp.concatenate([aff, jnp.zeros((LANES - N_EXPERTS, seq), _F32)], axis=0).T
            ranks = []
            for e in range(N_EXPERTS):
                theirs = aff_t[:, e:e + 1]
                blocks = []
                for lo in range(0, seq, LANES):
                    hi = lo + LANES
                    mine = aff[e:e + 1, lo:hi]
                    diag = theirs[lo:hi]
                    cnt = jnp.sum(jnp.where(diag > mine, 1.0, jnp.where(diag == mine, tie, 0.0)),
                                  axis=0, keepdims=True)
                    if lo > 0:
                        cnt += jnp.sum(jnp.where(theirs[0:lo] >= mine, 1.0, 0.0), axis=0, keepdims=True)
                    if hi < seq:
                        cnt += jnp.sum(jnp.where(theirs[hi:seq] > mine, 1.0, 0.0), axis=0, keepdims=True)
                    blocks.append(cnt)
                ranks.append(jnp.concatenate(blocks, axis=1))
            rank = jnp.concatenate(ranks, axis=0)
            sel = rank < float(cap)
            slot = _dot(jnp.where(sel, 1.0, 0.0).astype(_BF16), before)
            rows = slice(s * N_EXPERTS, (s + 1) * N_EXPERTS)
            aff_scr[rows, :] = aff
            key_scr[rows, :] = jnp.where(sel, slot, -1.0)

    c_idx = lax.broadcasted_iota(jnp.int32, (cap, seq), 0).astype(_F32)
    for s in range(sets):
        onehot, gated = [], []
        for i in range(e_blk):
            e = s * N_EXPERTS + j * e_blk + i
            hit = c_idx == key_scr[pl.ds(e, 1), :]
            onehot.append(jnp.where(hit, 1.0, 0.0))
            gated.append(jnp.where(hit, aff_scr[pl.ds(e, 1), :], 0.0))
        xs = _dot(jnp.concatenate(onehot, axis=0).astype(_BF16), h2_ref[s * seq:(s + 1) * seq, :]).astype(xs_ref.dtype)
        for i in range(e_blk):
            xs_ref[i, s * cap:(s + 1) * cap, :] = xs[i * cap:(i + 1) * cap, :]
        gw_ref[s] = jnp.concatenate(gated, axis=0).astype(gw_ref.dtype)


def _route(logits_t, h2, seq, sets, e_blk):
    n_tok = h2.shape[0]
    n_sets = n_tok // seq
    cap = EC_CAPACITY_FACTOR * seq // N_EXPERTS
    return pl.pallas_call(
        functools.partial(_route_kernel, seq, cap, sets, e_blk),
        grid=(n_sets // sets, N_EXPERTS // e_blk),
        in_specs=[
            pl.BlockSpec((N_EXPERTS, sets * seq), lambda s, j: (0, s)),
            pl.BlockSpec((sets * seq, D_MODEL), lambda s, j: (s, 0)),
        ],
        out_specs=[
            pl.BlockSpec((e_blk, sets * cap, D_MODEL), lambda s, j: (j, s, 0)),
            pl.BlockSpec((sets, e_blk * cap, seq), lambda s, j: (s, j, 0)),
        ],
        out_shape=[
            jax.ShapeDtypeStruct((N_EXPERTS, n_sets * cap, D_MODEL), _BF16),
            jax.ShapeDtypeStruct((n_sets, N_EXPERTS * cap, seq), _BF16),
        ],
        scratch_shapes=[pltpu.VMEM((sets * N_EXPERTS, seq), _F32), pltpu.VMEM((sets * N_EXPERTS, seq), _F32)],
        compiler_params=_params(2),
        name="route_%d" % seq,
    )(logits_t, h2)


def _ffn_kernel(xa_ref, xb_ref, wg_ref, wu_ref, wd_ref, ya_ref, yb_ref):
    def run(first):
        wg = wg_ref[...].astype(_BF16)
        wu = wu_ref[...].astype(_BF16)
        wd = wd_ref[...].astype(_BF16)
        for x_ref, y_ref in ((xa_ref, ya_ref), (xb_ref, yb_ref)):
            chunk = min(FFN_ROW_CHUNK, x_ref.shape[0])
            for c in range(x_ref.shape[0] // chunk):
                rows = slice(c * chunk, (c + 1) * chunk)
                x = x_ref[rows, :]
                a = _dot(x, wg)
                u = _dot(x, wu)
                total = _dot((a * jax.nn.sigmoid(a) * u).astype(_BF16), wd)
                if not first:
                    total = y_ref[rows, :].astype(_F32) + total
                y_ref[rows, :] = total.astype(y_ref.dtype)

    @pl.when(pl.program_id(1) == 0)
    def _():
        run(True)

    @pl.when(pl.program_id(1) > 0)
    def _():
        run(False)


def _ffn(xs_a, xs_b, w_gate, w_up, w_down):
    rows_a, rows_b = xs_a.shape[1], xs_b.shape[1]
    rows = lambda n: pl.BlockSpec((None, n, D_MODEL), lambda e, f: (e, 0, 0))
    return pl.pallas_call(
        _ffn_kernel,
        grid=(N_EXPERTS, D_FF // FF_TILE),
        in_specs=[
            rows(rows_a), rows(rows_b),
            pl.BlockSpec((None, D_MODEL, FF_TILE), lambda e, f: (e, 0, f)),
            pl.BlockSpec((None, D_MODEL, FF_TILE), lambda e, f: (e, 0, f)),
            pl.BlockSpec((None, FF_TILE, D_MODEL), lambda e, f: (e, f, 0)),
        ],
        out_specs=[rows(rows_a), rows(rows_b)],
        out_shape=[jax.ShapeDtypeStruct(xs_a.shape, _BF16), jax.ShapeDtypeStruct(xs_b.shape, _BF16)],
        compiler_params=_params(2),
        name="ffn",
    )(xs_a, xs_b, w_gate, w_up, w_down)


def _combine_kernel(latent, x_hbm, y_ref, gw_ref, mod_ref, o_ref, x_ring, sems):
    r = 1 + pl.program_id(0) if latent else 0
    sets, rows, tile = gw_ref.shape
    cap = rows // N_EXPERTS
    gate = mod_ref[3, pl.ds(r, 1), :]

    step = pl.program_id(0) * pl.num_programs(1) + pl.program_id(1)
    n_steps = pl.num_programs(0) * pl.num_programs(1)
    tokens = x_ring.shape[1]

    def fetch(k):
        slot = k % COMBINE_RING
        return pltpu.make_async_copy(x_hbm.at[pl.ds(k * tokens, tokens), :], x_ring.at[slot], sems.at[slot])

    @pl.when(step == 0)
    def _():
        for k in range(COMBINE_RING - 1):
            fetch(k).start()

    @pl.when(step + COMBINE_RING - 1 < n_steps)
    def _():
        fetch(step + COMBINE_RING - 1).start()

    fetch(step).wait()
    slot = step % COMBINE_RING
    for i in range(sets):
        y = y_ref[:, i * cap:(i + 1) * cap, :].reshape(rows, D_MODEL)
        moe = lax.dot_general(gw_ref[i], y, (((0,), (0,)), ((), ())), preferred_element_type=_F32)
        tok = slice(i * tile, (i + 1) * tile)
        o_ref[tok, :] = x_ring[slot, tok, :] + gate * moe


def _combine(x_new, y, gw, mod, latent, seq, step_tokens):
    n_tok = x_new.shape[0]
    n_sets = n_tok // seq
    cap = EC_CAPACITY_FACTOR * seq // N_EXPERTS
    sets = max(1, step_tokens // seq)
    tile = step_tokens // sets
    per_set = seq // tile
    tokens = pl.BlockSpec((step_tokens, D_MODEL), lambda s, t: (s * per_set + t, 0))
    assert n_tok // step_tokens >= COMBINE_RING - 1
    return pl.pallas_call(
        functools.partial(_combine_kernel, latent),
        grid=(n_sets // sets, per_set),
        in_specs=[
            pl.BlockSpec(memory_space=pl.ANY),
            pl.BlockSpec((N_EXPERTS, sets * cap, D_MODEL), lambda s, t: (0, s, 0)),
            pl.BlockSpec((sets, N_EXPERTS * cap, tile), lambda s, t: (s, 0, t)),
            _resident((N_MOD_LATE, COND_ROWS, D_MODEL)),
        ],
        out_specs=tokens,
        out_shape=jax.ShapeDtypeStruct((n_tok, D_MODEL), _F32),
        scratch_shapes=[pltpu.VMEM((COMBINE_RING, step_tokens, D_MODEL), _F32),
                        pltpu.SemaphoreType.DMA((COMBINE_RING,))],
        compiler_params=_params(2),
        name="combine_lat" if latent else "combine_ctx",
    )(x_new, y, gw, mod)


def kernel(x_prompt, x_sample, cache_k, cache_v, c, c_ctx, w_ada, b_ada, g_norm1, w_in, g_q, g_k, conv_w,
           g_attn_out, g_conv_out, w_out, g_norm2, w_router, w_gate, w_up, w_down):
    batch, seq, _ = x_prompt.shape
    dec_batch, dec_seq, _ = x_sample.shape
    depth = w_ada.shape[0]
    assert 1 + dec_batch <= COND_ROWS

    cond = jnp.concatenate([c_ctx[None, :], c, jnp.zeros((COND_ROWS - 1 - dec_batch, D_MODEL), _F32)], axis=0)
    xp = x_prompt.reshape(batch * seq, D_MODEL)
    xl = x_sample.reshape(dec_batch * dec_seq, D_MODEL)
    past = cache_k.shape[2]

    new_k, new_v = [], []
    for l in range(depth):
        mod = _ada(cond, w_ada[l], b_ada[l])
        w_router_t = w_router[l].T

        q_p, k_p, v_p, bg_p, cu_p, kstate, vstate, w_in_b, w_out_b = _inproj(
            xp, mod, g_norm1[l], w_in[l], g_q[l], g_k[l], False, seq, w_out[l])
        q_l, k_l, v_l, bg_l, cu_l, mod_late = _inproj(xl, mod, g_norm1[l], w_in_b, g_q[l], g_k[l], True, dec_seq,
                                                      ada=(cond, w_ada[l], b_ada[l]))

        shared = (mod_late, conv_w[l], g_attn_out[l], g_conv_out[l], w_out_b, g_norm2[l], w_router_t)
        kv_p = [(k_p.reshape(batch, seq, D_KV), v_p.reshape(batch, seq, D_KV))]
        kv_l = [(cache_k[:, l].reshape(dec_batch, past, D_KV), cache_v[:, l].reshape(dec_batch, past, D_KV)),
                (k_l.reshape(dec_batch, dec_seq, D_KV), v_l.reshape(dec_batch, dec_seq, D_KV))]
        xp1, h2_p, lg_p = _mix(xp, q_p, kv_p, bg_p, cu_p, *shared, False, seq)
        xl1, h2_l, lg_l = _mix(xl, q_l, kv_l, bg_l, cu_l, *shared, True, dec_seq)

        xs_p, gw_p = _route(lg_p, h2_p, seq, CONTEXT_ROUTE_SETS, N_EXPERTS)
        xs_l, gw_l = _route(lg_l, h2_l, dec_seq, 1, LATENT_ROUTE_EXPERTS)
        y_p, y_l = _ffn(xs_p, xs_l, w_gate[l], w_up[l], w_down[l])
        xp = _combine(xp1, y_p, gw_p, mod_late, False, seq, CONTEXT_COMBINE_TOKENS)
        xl = _combine(xl1, y_l, gw_l, mod_late, True, dec_seq, LATENT_COMBINE_TOKENS)

        new_k.append(kstate.reshape(batch, seq, N_KV_HEADS, HEAD_DIM))
        new_v.append(vstate.reshape(batch, seq, N_KV_HEADS, HEAD_DIM))

    return (xp.reshape(batch, seq, D_MODEL), xl.reshape(dec_batch, dec_seq, D_MODEL),
            jnp.stack(new_k, axis=1), jnp.stack(new_v, axis=1))
```

```python
import functools

import jax
import jax.numpy as jnp
from jax import lax
from jax.experimental import pallas as pl
from jax.experimental.pallas import tpu as pltpu

D_MODEL = 2048
GRID_W = 64
HEAD_DIM = 128
D_ATTN = 1024
D_CONV = 1024
N_HEADS = 8
N_KV_HEADS = 2
GQA_GROUP = N_HEADS // N_KV_HEADS
D_KV = N_KV_HEADS * HEAD_DIM
N_EXPERTS = 16
EC_CAPACITY_FACTOR = 2
D_FF = 1024
ROPE_THETA = 10000.0
RMS_EPS = 1e-6
LOG2_E = 1.4426950408889634
N_MOD = 6
N_MOD_EARLY = 2
N_MOD_LATE = N_MOD - N_MOD_EARLY
D_IN_PROJ = D_ATTN + 2 * D_KV + 3 * D_CONV

CONTEXT_MIX_TOKENS = 512
LATENT_QUERY_SPLIT = 2
LATENT_MIX_TOKENS = 256
INPROJ_TILE = 512
LATENT_INPROJ_TILE = 256
ADA_COL_TILE = 1024
FF_TILE = 512
FFN_ROW_CHUNK = 512
COND_ROWS = 8
HALO_ROWS = 8
LATENT_ROUTE_EXPERTS = 8
CONTEXT_ROUTE_SETS = 4
CONTEXT_HEADS_STACKED = 2
VMEM_LIMIT = 56 * 1024 * 1024
LANES = 128
STAGE_ROWS = 32
STAGE_SLOTS = 8
CONTEXT_COMBINE_TOKENS = 1024
LATENT_COMBINE_TOKENS = 512

_BF16 = jnp.bfloat16
_F32 = jnp.float32


def _params(n_axes):
    return pltpu.CompilerParams(dimension_semantics=("arbitrary",) * n_axes,
                                vmem_limit_bytes=VMEM_LIMIT)


def _resident(shape):
    return pl.BlockSpec(shape, lambda *_: (0,) * len(shape), pipeline_mode=pl.Buffered(1))


def _dot(a, b):
    return jnp.dot(a, b, preferred_element_type=_F32)


def _dot_nt(a, b):
    return lax.dot_general(a, b, (((1,), (1,)), ((), ())), preferred_element_type=_F32)


def _rms(x):
    return x * lax.rsqrt(jnp.mean(x * x, axis=-1, keepdims=True) + RMS_EPS)


def _split_bf16(x):
    hi = x.astype(_BF16)
    lo = (x - hi.astype(_F32)).astype(_BF16)
    return hi, lo


def _weight_scratch(rows, cols):
    return [pltpu.VMEM((rows, cols), _BF16), pltpu.VMEM((STAGE_SLOTS, STAGE_ROWS, cols), _F32),
            pltpu.SemaphoreType.DMA((STAGE_SLOTS,)), pltpu.SemaphoreType.DMA(())]


def _cast_weight_once(w_hbm, w_pub, w_scr, stage, sems, pub_sem, step, n_steps):
    n_chunks = w_scr.shape[0] // STAGE_ROWS

    def fetch(c):
        slot = c % STAGE_SLOTS
        return pltpu.make_async_copy(w_hbm.at[pl.ds(c * STAGE_ROWS, STAGE_ROWS), :], stage.at[slot], sems.at[slot])

    publish = pltpu.make_async_copy(w_scr, w_pub, pub_sem)

    @pl.when(step == 0)
    def _():
        for c in range(STAGE_SLOTS):
            fetch(c).start()
        for c in range(n_chunks):
            fetch(c).wait()
            w_scr[c * STAGE_ROWS:(c + 1) * STAGE_ROWS, :] = stage[c % STAGE_SLOTS].astype(_BF16)
            if c + STAGE_SLOTS < n_chunks:
                fetch(c + STAGE_SLOTS).start()
        publish.start()

    @pl.when(step == n_steps - 1)
    def _():
        publish.wait()


def _stream_scratch(rows, cols, n_steps):
    chunk = rows // n_steps
    return [pltpu.VMEM((2, chunk, cols), _F32), pltpu.VMEM((2, chunk, cols), _BF16),
            pltpu.SemaphoreType.DMA((2,)), pltpu.SemaphoreType.DMA((2,))]


def _cast_weight_alongside(w_hbm, w_pub, stage, stage_b, in_sems, out_sems, step, n_steps):
    chunk = stage.shape[1]
    slot = step % 2

    def fetch(k, sl):
        return pltpu.make_async_copy(w_hbm.at[pl.ds(k * chunk, chunk), :], stage.at[sl], in_sems.at[sl])

    def push(k, sl):
        return pltpu.make_async_copy(stage_b.at[sl], w_pub.at[pl.ds(k * chunk, chunk), :], out_sems.at[sl])

    @pl.when(step == 0)
    def _():
        fetch(0, 0).start()

    @pl.when(step + 1 < n_steps)
    def _():
        fetch(step + 1, 1 - slot).start()

    fetch(step, slot).wait()

    @pl.when(step >= 2)
    def _():
        push(step - 2, slot).wait()

    stage_b[slot] = stage[slot].astype(_BF16)
    push(step, slot).start()

    @pl.when(step == n_steps - 1)
    def _():
        push(step, slot).wait()
        push(step - 1, 1 - slot).wait()


def _ada_kernel(c_ref, w_ref, b_ref, o_ref):
    c = c_ref[...]
    a = c * jax.nn.sigmoid(c)
    a_hi, a_lo = _split_bf16(a)
    w = w_ref[...]
    w_hi, w_lo = _split_bf16(w)
    o_ref[...] = _dot(a_hi, w_hi) + (_dot(a_lo, w_hi) + _dot(a_hi, w_lo)) + b_ref[...]


def _late_modulation(cond_ref, wada_refs, bada_ref, late_ref, acc, step, n_steps):
    @pl.when(step == 0)
    def _():
        acc[...] = jnp.zeros_like(acc)

    c = cond_ref[...]
    a = jnp.concatenate(_split_bf16(c * jax.nn.sigmoid(c)), axis=0)
    for j, w_ref in enumerate(wada_refs):
        d = _dot(a, w_ref[...].astype(_BF16))
        acc[j] += d[0:COND_ROWS] + d[COND_ROWS:2 * COND_ROWS]

    @pl.when(step == n_steps - 1)
    def _():
        for j in range(len(wada_refs)):
            cols = slice((N_MOD_EARLY + j) * D_MODEL, (N_MOD_EARLY + j + 1) * D_MODEL)
            late_ref[j] = acc[j] + bada_ref[:, cols]


def _ada(cond, w_ada, b_ada):
    per_mod = D_MODEL // ADA_COL_TILE
    return pl.pallas_call(
        _ada_kernel,
        grid=(N_MOD_EARLY, per_mod),
        in_specs=[
            pl.BlockSpec((COND_ROWS, D_MODEL), lambda j, h: (0, 0)),
            pl.BlockSpec((D_MODEL, ADA_COL_TILE), lambda j, h: (0, j * per_mod + h)),
            pl.BlockSpec((1, ADA_COL_TILE), lambda j, h: (0, j * per_mod + h)),
        ],
        out_specs=pl.BlockSpec((None, COND_ROWS, ADA_COL_TILE), lambda j, h: (j, 0, h)),
        out_shape=jax.ShapeDtypeStruct((N_MOD_EARLY, COND_ROWS, D_MODEL), _F32),
        compiler_params=_params(2),
        name="ada",
    )(cond, w_ada, b_ada.reshape(1, N_MOD * D_MODEL))


def _rope_tables(seq):
    t = jnp.arange(seq)
    row = (t // GRID_W).astype(_F32)
    col = (t % GRID_W).astype(_F32)
    half = HEAD_DIM // 2
    inv_freq = ROPE_THETA ** (-jnp.arange(0, half, 2, dtype=_F32) / half)
    ang_r = row[:, None] * inv_freq[None, :]
    ang_c = col[:, None] * inv_freq[None, :]
    cos = jnp.concatenate([jnp.cos(ang_r)] * 2 + [jnp.cos(ang_c)] * 2, axis=-1)
    sin = jnp.concatenate([-jnp.sin(ang_r), jnp.sin(ang_r), -jnp.sin(ang_c), jnp.sin(ang_c)], axis=-1)
    return cos, sin


def _cond_row(latent, seq_tiles):
    if not latent:
        return 0
    return 1 + pl.program_id(0) // seq_tiles


def _inproj_kernel(latent, seq_tiles, x_ref, mod_ref, g1_ref, w_ref, gq_ref, gk_ref, *rest):
    if latent:
        cos_ref, sin_ref, cond_ref, *rest = rest
        wada_refs, rest = rest[:N_MOD_LATE], rest[N_MOD_LATE:]
        bada_ref, q_ref, k_ref, v_ref, bg_ref, cu_ref, late_ref, late_acc = rest
        _late_modulation(cond_ref, wada_refs, bada_ref, late_ref, late_acc, pl.program_id(0), pl.num_programs(0))
    else:
        (wo_hbm, q_ref, k_ref, v_ref, bg_ref, cu_ref, kstate_ref, vstate_ref, w_pub, wo_pub,
         *scratch) = rest
        cast_scratch, stream_scratch = scratch[:4], scratch[4:]
        _cast_weight_once(w_ref, w_pub, *cast_scratch, pl.program_id(0), pl.num_programs(0))
        w_ref = cast_scratch[0]
        _cast_weight_alongside(wo_hbm, wo_pub, *stream_scratch, pl.program_id(0), pl.num_programs(0))
    r = _cond_row(latent, seq_tiles)
    sh = mod_ref[0, pl.ds(r, 1), :]
    sc = mod_ref[1, pl.ds(r, 1), :]
    h = _rms(x_ref[...]) * (g1_ref[...] * (1.0 + sc)) + sh
    hb = h.astype(_BF16)

    if latent:
        cos = cos_ref[...]
        sin = sin_ref[...]
        lane = lax.broadcasted_iota(jnp.int32, cos.shape, 1)
        first = (lane % (HEAD_DIM // 2)) < (HEAD_DIM // 4)

    def head_norm(xh, g):
        xh = _rms(xh) * g
        if latent:
            partner = jnp.where(first, pltpu.roll(xh, HEAD_DIM - HEAD_DIM // 4, axis=1),
                                pltpu.roll(xh, HEAD_DIM // 4, axis=1))
            xh = xh * cos + partner * sin
        return xh

    gq = gq_ref[...] * (HEAD_DIM ** -0.5 * LOG2_E)
    q = _dot(hb, w_ref[:, 0:D_ATTN])
    for hd in range(N_HEADS):
        cols = slice(hd * HEAD_DIM, (hd + 1) * HEAD_DIM)
        q_ref[:, cols] = head_norm(q[:, cols], gq).astype(q_ref.dtype)
    k = _dot(hb, w_ref[:, D_ATTN:D_ATTN + D_KV])
    o = D_ATTN + D_KV
    v = _dot(hb, w_ref[:, o:o + D_KV])
    for hd in range(N_KV_HEADS):
        cols = slice(hd * HEAD_DIM, (hd + 1) * HEAD_DIM)
        kh = head_norm(k[:, cols], gk_ref[...])
        k_ref[:, cols] = kh
        v_ref[:, cols] = v[:, cols]
        if not latent:
            kstate_ref[:, hd, :] = kh
            vstate_ref[:, hd, :] = v[:, cols]
    o += D_KV
    bg_ref[...] = _dot(hb, w_ref[:, o:o + D_CONV])
    o += D_CONV
    cg = _dot(hb, w_ref[:, o:o + D_CONV])
    o += D_CONV
    cu_ref[...] = cg * _dot(hb, w_ref[:, o:o + D_CONV])


def _inproj(x, mod, g_norm1, w_in, g_q, g_k, latent, seq, w_out=None, ada=None):
    n_tok = x.shape[0]
    rows = LATENT_INPROJ_TILE if latent else INPROJ_TILE
    seq_tiles = seq // rows
    n_steps = n_tok // rows
    tile = lambda w: pl.BlockSpec((rows, w), lambda i: (i, 0))
    in_specs = [
        tile(D_MODEL),
        _resident((N_MOD_EARLY, COND_ROWS, D_MODEL)),
        _resident((1, D_MODEL)),
        _resident((D_MODEL, D_IN_PROJ)) if latent else pl.BlockSpec(memory_space=pl.ANY),
        _resident((1, HEAD_DIM)),
        _resident((1, HEAD_DIM)),
    ]
    args = [x, mod, g_norm1.reshape(1, D_MODEL), w_in, g_q.reshape(1, HEAD_DIM), g_k.reshape(1, HEAD_DIM)]
    if latent:
        cos, sin = _rope_tables(seq)
        rope_spec = pl.BlockSpec((rows, HEAD_DIM), lambda i: (i % seq_tiles, 0))
        in_specs += [rope_spec, rope_spec]
        args += [cos, sin]
        cond, w_ada, b_ada = ada
        k_rows = D_MODEL // n_steps
        assert k_rows % LANES == 0
        in_specs += [pl.BlockSpec((COND_ROWS, k_rows), lambda i: (0, i))]
        in_specs += [pl.BlockSpec((k_rows, D_MODEL), lambda i, j=j: (i, N_MOD_EARLY + j)) for j in range(N_MOD_LATE)]
        in_specs += [_resident((1, N_MOD * D_MODEL))]
        args += [cond] + [w_ada] * N_MOD_LATE + [b_ada.reshape(1, N_MOD * D_MODEL)]
    else:
        in_specs += [pl.BlockSpec(memory_space=pl.ANY)]
        args += [w_out]
    scratch_shapes = []
    out_specs = [tile(D_ATTN), tile(D_KV), tile(D_KV), tile(D_CONV), tile(D_CONV)]
    out_shape = [
        jax.ShapeDtypeStruct((n_tok, D_ATTN), _BF16),
        jax.ShapeDtypeStruct((n_tok, D_KV), _F32),
        jax.ShapeDtypeStruct((n_tok, D_KV), _F32),
        jax.ShapeDtypeStruct((n_tok, D_CONV), _F32),
        jax.ShapeDtypeStruct((n_tok, D_CONV), _F32),
    ]
    if latent:
        late = (N_MOD_LATE, COND_ROWS, D_MODEL)
        out_specs += [pl.BlockSpec(late, lambda i: (0, 0, 0))]
        out_shape += [jax.ShapeDtypeStruct(late, _F32)]
        scratch_shapes = [pltpu.VMEM(late, _F32)]
    else:
        state_spec = pl.BlockSpec((rows, N_KV_HEADS, HEAD_DIM), lambda i: (i, 0, 0))
        out_specs += [state_spec, state_spec, pl.BlockSpec(memory_space=pl.ANY), pl.BlockSpec(memory_space=pl.ANY)]
        out_shape += [jax.ShapeDtypeStruct((n_tok, N_KV_HEADS, HEAD_DIM), _F32)] * 2
        out_shape += [jax.ShapeDtypeStruct((D_MODEL, D_IN_PROJ), _BF16), jax.ShapeDtypeStruct(w_out.shape, _BF16)]
        assert n_steps >= 2 and w_out.shape[0] % (16 * n_steps) == 0
        scratch_shapes = _weight_scratch(D_MODEL, D_IN_PROJ) + _stream_scratch(*w_out.shape, n_steps)
    return pl.pallas_call(
        functools.partial(_inproj_kernel, latent, seq_tiles),
        grid=(n_steps,),
        in_specs=in_specs,
        out_specs=out_specs,
        out_shape=out_shape,
        scratch_shapes=scratch_shapes,
        compiler_params=_params(1),
        name="inproj_lat" if latent else "inproj_ctx",
    )(*args)


def _mix_kernel(latent, seqs, seq_tiles, n_tiles, x_ref, q_ref, *rest):
    n_kv = 2 if latent else 1
    kv_refs = rest[:2 * n_kv]
    rest = rest[2 * n_kv:]
    bg_ref, cu_ref = rest[:2]
    rest = rest[2:]
    if latent:
        cu_prev_ref, cu_next_ref = rest[:2]
        rest = rest[2:]
    (convw_ref, ga_ref, gc_ref, wout_ref, mod_ref, g2_ref, wr_ref, xo_ref, h2_ref, lg_ref, *rest) = rest
    (mixed_scr,) = rest

    s = pl.program_id(0)

    @pl.when(s == 0)
    def _():
        mixed_scr[...] = jnp.zeros_like(mixed_scr)

    r = 1 + jnp.maximum(s - 1, 0) // seq_tiles if latent else 0
    y = _dot(mixed_scr[...], wout_ref[...])
    x_new = x_ref[...] + mod_ref[0, pl.ds(r, 1), :] * y
    xo_ref[...] = x_new
    h2 = _rms(x_new) * (g2_ref[...] * (1.0 + mod_ref[2, pl.ds(r, 1), :])) + mod_ref[1, pl.ds(r, 1), :]
    h2_hi = h2.astype(_BF16)
    h2_ref[...] = h2_hi
    lg_ref[...] = _dot_nt(wr_ref[...].astype(_BF16), h2_hi)

    rows_per_seq = x_ref.shape[0] // seqs
    row_sum_on_mxu = latent
    stacked = 1 if latent else CONTEXT_HEADS_STACKED
    sub = lax.broadcasted_iota(jnp.int32, (HALO_ROWS, D_CONV), 0)
    for i in range(seqs):
        rs = slice(i * rows_per_seq, (i + 1) * rows_per_seq)

        cu = cu_ref[rs, :]
        if latent:
            pos = jnp.minimum(s, n_tiles - 1) % seq_tiles
            edge_prev = jnp.where(pos == 0, 0.0, 1.0) * cu_prev_ref[HALO_ROWS - 1:HALO_ROWS, :]
            edge_next = jnp.where(pos == seq_tiles - 1, 0.0, 1.0) * cu_next_ref[0:1, :]
        else:
            edge_prev = jnp.zeros((1, D_CONV), _F32)
            edge_next = edge_prev
        prev = pltpu.roll(cu, 1, axis=0)
        nxt = pltpu.roll(cu, rows_per_seq - 1, axis=0)
        prev = jnp.concatenate([jnp.where(sub == 0, edge_prev, prev[0:HALO_ROWS]), prev[HALO_ROWS:]], axis=0)
        nxt = jnp.concatenate([nxt[:rows_per_seq - HALO_ROWS],
                               jnp.where(sub == HALO_ROWS - 1, edge_next, nxt[rows_per_seq - HALO_ROWS:])], axis=0)
        conv = bg_ref[rs, :] * (prev * convw_ref[0:1, :] + cu * convw_ref[1:2, :] + nxt * convw_ref[2:3, :])
        mixed_scr[rs, D_ATTN:D_ATTN + D_CONV] = (_rms(conv) * gc_ref[...]).astype(_BF16)

        heads = []
        for kvh in range(N_KV_HEADS):
            cols = slice(kvh * HEAD_DIM, (kvh + 1) * HEAD_DIM)
            if latent:
                ks = [kv_refs[2 * p][:, cols].astype(_BF16) for p in range(n_kv)]
                vs = [kv_refs[2 * p + 1][:, cols].astype(_BF16) for p in range(n_kv)]
            else:
                ks = [kv_refs[0][i, :, cols].astype(_BF16)]
                vs = [kv_refs[1][i, :, cols].astype(_BF16)]
            if row_sum_on_mxu:
                vs = [jnp.concatenate([vp, jnp.ones_like(vp)], axis=1) for vp in vs]
            for g0 in range(0, GQA_GROUP, stacked):
                first = kvh * GQA_GROUP + g0
                def attend(qh):
                    sc = [_dot_nt(qh, kp) for kp in ks]
                    m = functools.reduce(jnp.maximum, [jnp.max(sp, axis=-1, keepdims=True) for sp in sc])
                    if row_sum_on_mxu:
                        ol = functools.reduce(jnp.add,
                                              [_dot(jnp.exp2(sp - m).astype(_BF16), vp) for sp, vp in zip(sc, vs)])
                        return ol[:, 0:HEAD_DIM] / ol[:, HEAD_DIM:HEAD_DIM + 1]
                    p = [jnp.exp2(sp - m) for sp in sc]
                    l = functools.reduce(jnp.add, [jnp.sum(pp, axis=-1, keepdims=True) for pp in p])
                    return functools.reduce(jnp.add, [_dot(pp.astype(_BF16), vp) for pp, vp in zip(p, vs)]) / l

                if latent:
                    part = rows_per_seq // LATENT_QUERY_SPLIT
                    hc = slice(first * HEAD_DIM, (first + 1) * HEAD_DIM)
                    o = jnp.concatenate([attend(q_ref[k * part:(k + 1) * part, hc])
                                         for k in range(LATENT_QUERY_SPLIT)], axis=0)
                else:
                    o = attend(jnp.concatenate([q_ref[rs, (first + j) * HEAD_DIM:(first + j + 1) * HEAD_DIM]
                                                for j in range(stacked)], axis=0))
                heads += [o[j * rows_per_seq:(j + 1) * rows_per_seq] for j in range(stacked)]
        ssq = functools.reduce(jnp.add, [jnp.sum(h * h, axis=-1, keepdims=True) for h in heads])
        inv = lax.rsqrt(ssq * (1.0 / D_ATTN) + RMS_EPS)
        for hd, h in enumerate(heads):
            hc = slice(hd * HEAD_DIM, (hd + 1) * HEAD_DIM)
            mixed_scr[rs, hc] = (h * inv * ga_ref[:, hc]).astype(_BF16)


def _mix(x, q, kv_parts, bg, cu, mod, conv_w, g_attn_out, g_conv_out, w_out, g_norm2, w_router_t,
         latent, seq):
    n_tok = x.shape[0]
    tile = LATENT_MIX_TOKENS if latent else CONTEXT_MIX_TOKENS
    seqs = max(1, tile // seq)
    seq_tiles = max(1, seq // tile)
    n_tiles = n_tok // tile
    started = lambda i: jnp.minimum(i, n_tiles - 1)
    finished = lambda i: jnp.maximum(i - 1, 0)
    start_tile = lambda w: pl.BlockSpec((tile, w), lambda i: (started(i), 0))
    finish_tile = lambda w: pl.BlockSpec((tile, w), lambda i: (finished(i), 0))
    in_specs = [finish_tile(D_MODEL), start_tile(D_ATTN)]
    args = [x, q]
    for k, v in kv_parts:
        if latent:
            spec = pl.BlockSpec((None, k.shape[1], D_KV), lambda i: (started(i) // seq_tiles, 0, 0))
        else:
            spec = pl.BlockSpec((seqs, k.shape[1], D_KV), lambda i: (started(i), 0, 0))
        in_specs += [spec, spec]
        args += [k, v]
    in_specs += [start_tile(D_CONV), start_tile(D_CONV)]
    args += [bg, cu]
    if latent:
        per_tile = tile // HALO_ROWS
        last = n_tok // HALO_ROWS - 1
        in_specs += [
            pl.BlockSpec((HALO_ROWS, D_CONV), lambda i: (jnp.maximum(started(i) * per_tile - 1, 0), 0)),
            pl.BlockSpec((HALO_ROWS, D_CONV), lambda i: (jnp.minimum((started(i) + 1) * per_tile, last), 0)),
        ]
        args += [cu, cu]
    in_specs += [
        _resident((3, D_CONV)),
        _resident((1, D_ATTN)),
        _resident((1, D_CONV)),
        _resident((D_ATTN + D_CONV, D_MODEL)),
        _resident((N_MOD_LATE, COND_ROWS, D_MODEL)),
        _resident((1, D_MODEL)),
        _resident((N_EXPERTS, D_MODEL)),
    ]
    args += [conv_w, g_attn_out.reshape(1, D_ATTN), g_conv_out.reshape(1, D_CONV), w_out, mod,
             g_norm2.reshape(1, D_MODEL), w_router_t]
    out_specs = [finish_tile(D_MODEL), finish_tile(D_MODEL),
                 pl.BlockSpec((N_EXPERTS, tile), lambda i: (0, finished(i)))]
    out_shape = [
        jax.ShapeDtypeStruct((n_tok, D_MODEL), _F32),
        jax.ShapeDtypeStruct((n_tok, D_MODEL), _BF16),
        jax.ShapeDtypeStruct((N_EXPERTS, n_tok), _F32),
    ]
    scratch_shapes = [pltpu.VMEM((tile, D_ATTN + D_CONV), _BF16)]
    return pl.pallas_call(
        functools.partial(_mix_kernel, latent, seqs, seq_tiles, n_tiles),
        grid=(n_tiles + 1,),
        in_specs=in_specs,
        out_specs=out_specs,
        out_shape=out_shape,
        scratch_shapes=scratch_shapes,
        compiler_params=_params(1),
        name="mix_lat" if latent else "mix_ctx",
    )(*args)


def _route_kernel(seq, cap, sets, e_blk, lg_ref, h2_ref, xs_ref, gw_ref, aff_scr, key_scr):
    j = pl.program_id(1)

    @pl.when(j == 0)
    def _():
        other = lax.broadcasted_iota(jnp.int32, (seq, seq), 0)
        token = lax.broadcasted_iota(jnp.int32, (seq, seq), 1)
        earlier = jnp.where(other < token, 1.0, 0.0)
        tie = earlier[0:LANES, 0:LANES]
        before = earlier.astype(_BF16)
        for s in range(sets):
            lg = lg_ref[:, s * seq:(s + 1) * seq]
            ex = jnp.exp(lg - jnp.max(lg, axis=0, keepdims=True))
            aff = ex / jnp.sum(ex, axis=0, keepdims=True)
            aff_t = jnp.concatenate([aff, jnp.zeros((LANES - N_EXPERTS, seq), _F32)], axis=0).T
            ranks = []
            for e in range(N_EXPERTS):
                theirs = aff_t[:, e:e + 1]
                blocks = []
                for lo in range(0, seq, LANES):
                    hi = lo + LANES
                    mine = aff[e:e + 1, lo:hi]
                    diag = theirs[lo:hi]
                    cnt = jnp.sum(jnp.where(diag > mine, 1.0, jnp.where(diag == mine, tie, 0.0)),
                                  axis=0, keepdims=True)
                    if lo > 0:
                        cnt += jnp.sum(jnp.where(theirs[0:lo] >= mine, 1.0, 0.0), axis=0, keepdims=True)
                    if hi < seq:
                        cnt += jnp.sum(jnp.where(theirs[hi:seq] > mine, 1.0, 0.0), axis=0, keepdims=True)
                    blocks.append(cnt)
                ranks.append(jnp.concatenate(blocks, axis=1))
            rank = jnp.concatenate(ranks, axis=0)
            sel = rank < float(cap)
            slot = _dot(jnp.where(sel, 1.0, 0.0).astype(_BF16), before)
            rows = slice(s * N_EXPERTS, (s + 1) * N_EXPERTS)
            aff_scr[rows, :] = aff
            key_scr[rows, :] = jnp.where(sel, slot, -1.0)

    c_idx = lax.broadcasted_iota(jnp.int32, (cap, seq), 0).astype(_F32)
    for s in range(sets):
        onehot, gated = [], []
        for i in range(e_blk):
            e = s * N_EXPERTS + j * e_blk + i
            hit = c_idx == key_scr[pl.ds(e, 1), :]
            onehot.append(jnp.where(hit, 1.0, 0.0))
            gated.append(jnp.where(hit, aff_scr[pl.ds(e, 1), :], 0.0))
        xs = _dot(jnp.concatenate(onehot, axis=0).astype(_BF16), h2_ref[s * seq:(s + 1) * seq, :]).astype(xs_ref.dtype)
        for i in range(e_blk):
            xs_ref[i, s * cap:(s + 1) * cap, :] = xs[i * cap:(i + 1) * cap, :]
        gw_ref[s] = jnp.concatenate(gated, axis=0).astype(gw_ref.dtype)


def _route(logits_t, h2, seq, sets, e_blk):
    n_tok = h2.shape[0]
    n_sets = n_tok // seq
    cap = EC_CAPACITY_FACTOR * seq // N_EXPERTS
    return pl.pallas_call(
        functools.partial(_route_kernel, seq, cap, sets, e_blk),
        grid=(n_sets // sets, N_EXPERTS // e_blk),
        in_specs=[
            pl.BlockSpec((N_EXPERTS, sets * seq), lambda s, j: (0, s)),
            pl.BlockSpec((sets * seq, D_MODEL), lambda s, j: (s, 0)),
        ],
        out_specs=[
            pl.BlockSpec((e_blk, sets * cap, D_MODEL), lambda s, j: (j, s, 0)),
            pl.BlockSpec((sets, e_blk * cap, seq), lambda s, j: (s, j, 0)),
        ],
        out_shape=[
            jax.ShapeDtypeStruct((N_EXPERTS, n_sets * cap, D_MODEL), _BF16),
            jax.ShapeDtypeStruct((n_sets, N_EXPERTS * cap, seq), _BF16),
        ],
        scratch_shapes=[pltpu.VMEM((sets * N_EXPERTS, seq), _F32), pltpu.VMEM((sets * N_EXPERTS, seq), _F32)],
        compiler_params=_params(2),
        name="route_%d" % seq,
    )(logits_t, h2)


def _ffn_kernel(xa_ref, xb_ref, wg_ref, wu_ref, wd_ref, ya_ref, yb_ref):
    def run(first):
        wg = wg_ref[...].astype(_BF16)
        wu = wu_ref[...].astype(_BF16)
        wd = wd_ref[...].astype(_BF16)
        for x_ref, y_ref in ((xa_ref, ya_ref), (xb_ref, yb_ref)):
            chunk = min(FFN_ROW_CHUNK, x_ref.shape[0])
            for c in range(x_ref.shape[0] // chunk):
                rows = slice(c * chunk, (c + 1) * chunk)
                x = x_ref[rows, :]
                a = _dot(x, wg)
                u = _dot(x, wu)
                total = _dot((a * jax.nn.sigmoid(a) * u).astype(_BF16), wd)
                if not first:
                    total = y_ref[rows, :].astype(_F32) + total
                y_ref[rows, :] = total.astype(y_ref.dtype)

    @pl.when(pl.program_id(1) == 0)
    def _():
        run(True)

    @pl.when(pl.program_id(1) > 0)
    def _():
        run(False)


def _ffn(xs_a, xs_b, w_gate, w_up, w_down):
    rows_a, rows_b = xs_a.shape[1], xs_b.shape[1]
    rows = lambda n: pl.BlockSpec((None, n, D_MODEL), lambda e, f: (e, 0, 0))
    return pl.pallas_call(
        _ffn_kernel,
        grid=(N_EXPERTS, D_FF // FF_TILE),
        in_specs=[
            rows(rows_a), rows(rows_b),
            pl.BlockSpec((None, D_MODEL, FF_TILE), lambda e, f: (e, 0, f)),
            pl.BlockSpec((None, D_MODEL, FF_TILE), lambda e, f: (e, 0, f)),
            pl.BlockSpec((None, FF_TILE, D_MODEL), lambda e, f: (e, f, 0)),
        ],
        out_specs=[rows(rows_a), rows(rows_b)],
        out_shape=[jax.ShapeDtypeStruct(xs_a.shape, _BF16), jax.ShapeDtypeStruct(xs_b.shape, _BF16)],
        compiler_params=_params(2),
        name="ffn",
    )(xs_a, xs_b, w_gate, w_up, w_down)


def _combine_kernel(latent, x_ref, y_ref, gw_ref, mod_ref, o_ref):
    r = 1 + pl.program_id(0) if latent else 0
    sets, rows, tile = gw_ref.shape
    cap = rows // N_EXPERTS
    gate = mod_ref[3, pl.ds(r, 1), :]
    for i in range(sets):
        y = y_ref[:, i * cap:(i + 1) * cap, :].reshape(rows, D_MODEL)
        moe = lax.dot_general(gw_ref[i], y, (((0,), (0,)), ((), ())), preferred_element_type=_F32)
        tok = slice(i * tile, (i + 1) * tile)
        o_ref[tok, :] = x_ref[tok, :] + gate * moe


def _combine(x_new, y, gw, mod, latent, seq, step_tokens):
    n_tok = x_new.shape[0]
    n_sets = n_tok // seq
    cap = EC_CAPACITY_FACTOR * seq // N_EXPERTS
    sets = max(1, step_tokens // seq)
    tile = step_tokens // sets
    per_set = seq // tile
    tokens = pl.BlockSpec((step_tokens, D_MODEL), lambda s, t: (s * per_set + t, 0))
    return pl.pallas_call(
        functools.partial(_combine_kernel, latent),
        grid=(n_sets // sets, per_set),
        in_specs=[
            tokens,
            pl.BlockSpec((N_EXPERTS, sets * cap, D_MODEL), lambda s, t: (0, s, 0)),
            pl.BlockSpec((sets, N_EXPERTS * cap, tile), lambda s, t: (s, 0, t)),
            _resident((N_MOD_LATE, COND_ROWS, D_MODEL)),
        ],
        out_specs=tokens,
        out_shape=jax.ShapeDtypeStruct((n_tok, D_MODEL), _F32),
        compiler_params=_params(2),
        name="combine_lat" if latent else "combine_ctx",
    )(x_new, y, gw, mod)


def kernel(x_prompt, x_sample, cache_k, cache_v, c, c_ctx, w_ada, b_ada, g_norm1, w_in, g_q, g_k, conv_w,
           g_attn_out, g_conv_out, w_out, g_norm2, w_router, w_gate, w_up, w_down):
    batch, seq, _ = x_prompt.shape
    dec_batch, dec_seq, _ = x_sample.shape
    depth = w_ada.shape[0]
    assert 1 + dec_batch <= COND_ROWS

    cond = jnp.concatenate([c_ctx[None, :], c, jnp.zeros((COND_ROWS - 1 - dec_batch, D_MODEL), _F32)], axis=0)
    xp = x_prompt.reshape(batch * seq, D_MODEL)
    xl = x_sample.reshape(dec_batch * dec_seq, D_MODEL)
    past = cache_k.shape[2]

    new_k, new_v = [], []
    for l in range(depth):
        mod = _ada(cond, w_ada[l], b_ada[l])
        w_router_t = w_router[l].T

        q_p, k_p, v_p, bg_p, cu_p, kstate, vstate, w_in_b, w_out_b = _inproj(
            xp, mod, g_norm1[l], w_in[l], g_q[l], g_k[l], False, seq, w_out[l])
        q_l, k_l, v_l, bg_l, cu_l, mod_late = _inproj(xl, mod, g_norm1[l], w_in_b, g_q[l], g_k[l], True, dec_seq,
                                                      ada=(cond, w_ada[l], b_ada[l]))

        shared = (mod_late, conv_w[l], g_attn_out[l], g_conv_out[l], w_out_b, g_norm2[l], w_router_t)
        kv_p = [(k_p.reshape(batch, seq, D_KV), v_p.reshape(batch, seq, D_KV))]
        kv_l = [(cache_k[:, l].reshape(dec_batch, past, D_KV), cache_v[:, l].reshape(dec_batch, past, D_KV)),
                (k_l.reshape(dec_batch, dec_seq, D_KV), v_l.reshape(dec_batch, dec_seq, D_KV))]
        xp1, h2_p, lg_p = _mix(xp, q_p, kv_p, bg_p, cu_p, *shared, False, seq)
        xl1, h2_l, lg_l = _mix(xl, q_l, kv_l, bg_l, cu_l, *shared, True, dec_seq)

        xs_p, gw_p = _route(lg_p, h2_p, seq, CONTEXT_ROUTE_SETS, N_EXPERTS)
        xs_l, gw_l = _route(lg_l, h2_l, dec_seq, 1, LATENT_ROUTE_EXPERTS)
        y_p, y_l = _ffn(xs_p, xs_l, w_gate[l], w_up[l], w_down[l])
        xp = _combine(xp1, y_p, gw_p, mod_late, False, seq, CONTEXT_COMBINE_TOKENS)
        xl = _combine(xl1, y_l, gw_l, mod_late, True, dec_seq, LATENT_COMBINE_TOKENS)

        new_k.append(kstate.reshape(batch, seq, N_KV_HEADS, HEAD_DIM))
        new_v.append(vstate.reshape(batch, seq, N_KV_HEADS, HEAD_DIM))

    return (xp.reshape(batch, seq, D_MODEL), xl.reshape(dec_batch, dec_seq, D_MODEL),
            jnp.stack(new_k, axis=1), jnp.stack(new_v, axis=1))
```
